```python
import jax, jax.numpy as jnp
from jax import lax
import numpy as np

D_MODEL = 1024
BATCH = 8
SEQ = 8192
DEPTH = 4

N_META = 16
N_MIXERS = 2
N_ATTN_LAYERS = (DEPTH + N_MIXERS - 1) // N_MIXERS
N_REC_LAYERS = DEPTH // N_MIXERS
HEAD_DIM = 128
N_Q_HEADS = D_MODEL // HEAD_DIM
N_KV_HEADS = 2
GROUP = N_Q_HEADS // N_KV_HEADS
ATTN_IN = (N_Q_HEADS + 2 * N_KV_HEADS) * HEAD_DIM
WINDOW = 128
ATTN_BLOCK = 128
ROPE_THETA = 10000.0
REC_HEAD_DIM = 128
N_REC_HEADS = D_MODEL // REC_HEAD_DIM
REC_IN = 5 * D_MODEL
REC_CHUNK = 64
D_FF = 4 * D_MODEL
EPS = 1e-6

kernel_name = "hybrid_swa_hgrn2_meta_encoder"


def rms_norm(x, w):
    xf = x.astype(jnp.float32)
    y = xf * lax.rsqrt(jnp.mean(xf * xf, axis=-1, keepdims=True) + EPS) * w.astype(jnp.float32)
    return y.astype(x.dtype)


def rope_tables(length):
    inv_freq = ROPE_THETA ** (-jnp.arange(0, HEAD_DIM, 2, dtype=jnp.float32) / HEAD_DIM)
    ang = jnp.arange(length, dtype=jnp.float32)[:, None] * inv_freq[None, :]
    return jnp.cos(ang), jnp.sin(ang)


def apply_rope(x, cos, sin):
    xf = x.astype(jnp.float32)
    x1, x2 = jnp.split(xf, 2, axis=-1)
    c = cos[None, :, None, :]
    s = sin[None, :, None, :]
    return jnp.concatenate([x1 * c - x2 * s, x2 * c + x1 * s], axis=-1).astype(x.dtype)


def band_blocks(t, nb):
    tp = jnp.pad(t, ((0, 0), (ATTN_BLOCK, ATTN_BLOCK), (0, 0), (0, 0)))
    tb = tp.reshape(t.shape[0], nb + 2, ATTN_BLOCK, t.shape[2], t.shape[3])
    return jnp.concatenate([tb[:, :-2], tb[:, 1:-1], tb[:, 2:]], axis=2)


def band_mask(nb):
    a = jnp.arange(ATTN_BLOCK)[None, :, None]
    c = jnp.arange(3 * ATTN_BLOCK)[None, None, :]
    n = jnp.arange(nb)[:, None, None]
    key_idx = (n - 1) * ATTN_BLOCK + c
    rel = c - ATTN_BLOCK - a
    return (jnp.abs(rel) <= WINDOW) & (key_idx >= 0) & (key_idx < nb * ATTN_BLOCK)


def window_gqa_attention(h, w_in, w_out, q_gain, k_gain, sink, cos, sin):
    B, L, _ = h.shape
    S = L - N_META
    nb = S // ATTN_BLOCK
    qkv = h @ w_in
    q, k, v = jnp.split(qkv, [N_Q_HEADS * HEAD_DIM, (N_Q_HEADS + N_KV_HEADS) * HEAD_DIM], axis=-1)
    q = q.reshape(B, L, N_Q_HEADS, HEAD_DIM)
    k = k.reshape(B, L, N_KV_HEADS, HEAD_DIM)
    v = v.reshape(B, L, N_KV_HEADS, HEAD_DIM)
    q = apply_rope(rms_norm(q, q_gain), cos, sin) * (HEAD_DIM ** -0.5)
    k = apply_rope(rms_norm(k, k_gain), cos, sin)
    q = q.reshape(B, L, N_KV_HEADS, GROUP, HEAD_DIM)
    qm, qr = q[:, :N_META], q[:, N_META:]
    km, kr = k[:, :N_META], k[:, N_META:]
    vm, vr = v[:, :N_META], v[:, N_META:]
    sink_l = sink.astype(jnp.float32).reshape(N_KV_HEADS, GROUP)
    nk = 3 * ATTN_BLOCK

    qb = qr.reshape(B, nb, ATTN_BLOCK, N_KV_HEADS, GROUP, HEAD_DIM)
    kw = band_blocks(kr, nb)
    vw = band_blocks(vr, nb)
    s_band = jnp.einsum('bnqhgd,bnkhd->bnhgqk', qb, kw, preferred_element_type=jnp.float32)
    s_band = jnp.where(band_mask(nb)[None, :, None, None], s_band, -jnp.inf)
    s_meta = jnp.einsum('bnqhgd,bmhd->bnhgqm', qb, km, preferred_element_type=jnp.float32)
    s_sink = jnp.broadcast_to(sink_l[None, None, :, :, None, None], s_meta.shape[:-1] + (1,))
    p = jax.nn.softmax(jnp.concatenate([s_band, s_meta, s_sink], axis=-1), axis=-1)
    p_band = p[..., :nk].astype(v.dtype)
    p_meta = p[..., nk:nk + N_META].astype(v.dtype)
    o_r = (jnp.einsum('bnhgqk,bnkhd->bnqhgd', p_band, vw)
           + jnp.einsum('bnhgqm,bmhd->bnqhgd', p_meta, vm)).reshape(B, S, D_MODEL)

    kmw = jnp.concatenate([km, kr[:, :ATTN_BLOCK]], axis=1)
    vmw = jnp.concatenate([vm, vr[:, :ATTN_BLOCK]], axis=1)
    q_pos = jnp.arange(N_META)[:, None]
    k_pos = jnp.arange(N_META + ATTN_BLOCK)[None, :]
    mask_m = (k_pos < N_META) | (k_pos - q_pos <= WINDOW)
    s_m = jnp.einsum('bqhgd,bkhd->bhgqk', qm, kmw, preferred_element_type=jnp.float32)
    s_m = jnp.where(mask_m[None, None, None], s_m, -jnp.inf)
    s_m_sink = jnp.broadcast_to(sink_l[None, :, :, None, None], s_m.shape[:-1] + (1,))
    p_m = jax.nn.softmax(jnp.concatenate([s_m, s_m_sink], axis=-1), axis=-1)[..., :-1].astype(v.dtype)
    o_m = jnp.einsum('bhgqk,bkhd->bqhgd', p_m, vmw).reshape(B, N_META, D_MODEL)

    o = jnp.concatenate([o_m, o_r], axis=1)
    return o @ w_out


def gla_chunk_step(state, xs):
    q, k, v, g = xs
    G = jnp.cumsum(g, axis=2)
    G_last = G[:, :, -1:, :]
    o_inter = jnp.einsum('bhtd,bhde->bhte', q * jnp.exp(G), state)
    C = q.shape[2]
    lower = jnp.arange(C)[:, None] >= jnp.arange(C)[None, :]
    diff = jnp.where(lower[None, None, :, :, None], G[:, :, :, None, :] - G[:, :, None, :, :], -jnp.inf)
    A = jnp.einsum('bhtd,bhsd,bhtsd->bhts', q, k, jnp.exp(diff))
    o_intra = jnp.einsum('bhts,bhse->bhte', A, v)
    new_state = (state * jnp.exp(G_last)[:, :, 0, :, None]
                 + jnp.einsum('bhsd,bhse->bhde', k * jnp.exp(G_last - G), v))
    return new_state, o_inter + o_intra


def chunk_scan(q, k, v, g):
    nc, B, H, C, dk = q.shape
    state0 = jnp.zeros((B, H, dk, v.shape[-1]), jnp.float32)
    _, o = lax.scan(gla_chunk_step, state0, (q, k, v, g))
    return o


def lower_bound(table, layer):
    p = jax.nn.softmax(table.astype(jnp.float32), axis=0)
    cs = jnp.cumsum(p, axis=0)
    return cs[layer] - cs[0]


def hgrn2_bidirectional(h, w_in, w_out, lb_fwd_table, lb_bwd_table, o_gain, layer):
    B, L, _ = h.shape
    pad = REC_CHUNK - N_META
    P = L + pad
    nc = P // REC_CHUNK
    proj = h @ w_in
    zq, zi, zf_f, zf_b, zg = jnp.split(proj, 5, axis=-1)
    q = jax.nn.silu(zq.astype(jnp.float32))
    v = zi.astype(jnp.float32)

    def forget(z, table):
        lb = lower_bound(table, layer)
        f = lb + (1.0 - lb) * jax.nn.sigmoid(z.astype(jnp.float32))
        return jnp.log(f), 1.0 - f

    g_f, k_f = forget(zf_f, lb_fwd_table)
    g_b, k_b = forget(zf_b, lb_bwd_table)

    def to_chunks(t):
        t = jnp.pad(t, ((0, 0), (pad, 0), (0, 0)))
        t = t.reshape(B, nc, REC_CHUNK, N_REC_HEADS, REC_HEAD_DIM)
        return t.transpose(1, 0, 3, 2, 4)

    qc, vc = to_chunks(q), to_chunks(v)
    o_f = chunk_scan(qc, to_chunks(k_f), vc, to_chunks(g_f))
    flip = lambda t: jnp.flip(t, axis=(0, 3))
    o_b = flip(chunk_scan(flip(qc), flip(to_chunks(k_b)), flip(vc), flip(to_chunks(g_b))))
    o = (o_f + o_b).transpose(1, 0, 3, 2, 4).reshape(B, P, N_REC_HEADS, REC_HEAD_DIM)[:, pad:]
    o = rms_norm(o, o_gain).reshape(B, L, D_MODEL)
    o = (o * jax.nn.silu(zg.astype(jnp.float32))).astype(h.dtype)
    return o @ w_out


def sq_relu_mlp(h, w_up, w_down):
    z = jax.nn.relu(h @ w_up)
    return (z * z) @ w_down


def setup_inputs(seed: int = 0) -> dict:
    key = jax.random.key(seed)
    ks = jax.random.split(key, 16)
    nrm = lambda k, shape, scale: jax.random.normal(k, shape, jnp.float32) * scale
    return {
        "x": nrm(ks[0], (BATCH, SEQ, D_MODEL), 1.0),
        "meta_tokens": nrm(ks[1], (N_META, D_MODEL), 1.0),
        "mix_norm_w": 1.0 + nrm(ks[2], (DEPTH, D_MODEL), 0.02),
        "mlp_norm_w": 1.0 + nrm(ks[3], (DEPTH, D_MODEL), 0.02),
        "attn_w_in": nrm(ks[4], (N_ATTN_LAYERS, D_MODEL, ATTN_IN), D_MODEL ** -0.5),
        "attn_w_out": nrm(ks[5], (N_ATTN_LAYERS, N_Q_HEADS * HEAD_DIM, D_MODEL), (N_Q_HEADS * HEAD_DIM) ** -0.5),
        "attn_q_norm_w": 1.0 + nrm(ks[6], (N_ATTN_LAYERS, HEAD_DIM), 0.02),
        "attn_k_norm_w": 1.0 + nrm(ks[7], (N_ATTN_LAYERS, HEAD_DIM), 0.02),
        "attn_sink": nrm(ks[8], (N_ATTN_LAYERS, N_Q_HEADS), 0.5),
        "rec_w_in": nrm(ks[9], (N_REC_LAYERS, D_MODEL, REC_IN), D_MODEL ** -0.5),
        "rec_w_out": nrm(ks[10], (N_REC_LAYERS, D_MODEL, D_MODEL), D_MODEL ** -0.5),
        "rec_lb_fwd": nrm(ks[11], (DEPTH, D_MODEL), 0.5),
        "rec_lb_bwd": nrm(ks[12], (DEPTH, D_MODEL), 0.5),
        "rec_out_norm_w": 1.0 + nrm(ks[13], (N_REC_LAYERS, REC_HEAD_DIM), 0.02),
        "mlp_w_up": nrm(ks[14], (DEPTH, D_MODEL, D_FF), D_MODEL ** -0.5),
        "mlp_w_down": nrm(ks[15], (DEPTH, D_FF, D_MODEL), D_FF ** -0.5),
    }


def reference(x, meta_tokens, mix_norm_w, mlp_norm_w, attn_w_in, attn_w_out, attn_q_norm_w,
              attn_k_norm_w, attn_sink, rec_w_in, rec_w_out, rec_lb_fwd, rec_lb_bwd,
              rec_out_norm_w, mlp_w_up, mlp_w_down):
    B = x.shape[0]
    meta = jnp.broadcast_to(meta_tokens.astype(x.dtype)[None], (B, N_META, D_MODEL))
    h = jnp.concatenate([meta, x], axis=1)
    cos, sin = rope_tables(h.shape[1])
    for layer in range(DEPTH):
        j = layer // N_MIXERS
        hn = rms_norm(h, mix_norm_w[layer])
        if layer % N_MIXERS == 0:
            h = h + window_gqa_attention(hn, attn_w_in[j], attn_w_out[j], attn_q_norm_w[j],
                                         attn_k_norm_w[j], attn_sink[j], cos, sin)
        else:
            h = h + hgrn2_bidirectional(hn, rec_w_in[j], rec_w_out[j], rec_lb_fwd, rec_lb_bwd,
                                        rec_out_norm_w[j], layer)
        h = h + sq_relu_mlp(rms_norm(h, mlp_norm_w[layer]), mlp_w_up[layer], mlp_w_down[layer])
    return h[:, N_META:]
```

```python
import functools

import numpy as np
import jax
import jax.numpy as jnp
from jax import lax
from jax.experimental import pallas as pl
from jax.experimental.pallas import tpu as pltpu

N_META = 16
HEAD_DIM = 128
N_KV_HEADS = 2
GROUP = 4
WINDOW = 128
ATTN_BLOCK = 128
ROPE_THETA = 10000.0
REC_CHUNK = 64
EPS = 1e-6

F32 = jnp.float32
BF16 = jnp.bfloat16

V7X_VMEM_LIMIT_BYTES = 56 * 1024 * 1024
TOKEN_TILE = 512


def _params(n_axes):
    return pltpu.CompilerParams(
        dimension_semantics=("arbitrary",) * n_axes,
        vmem_limit_bytes=V7X_VMEM_LIMIT_BYTES,
    )


def _resident(shape):
    zeros = (0,) * len(shape)
    return pl.BlockSpec(shape, lambda *_: zeros, pipeline_mode=pl.Buffered(1))


def _rms(x, w):
    ms = jnp.mean(x * x, axis=-1, keepdims=True)
    return x * lax.rsqrt(ms + EPS) * w


def _dot(a, b):
    return jnp.dot(a, b, preferred_element_type=F32)


def _dot_nt(a, b):
    return lax.dot_general(a, b, (((1,), (1,)), ((), ())), preferred_element_type=F32)


def _dot_tn(a, b):
    return lax.dot_general(a, b, (((0,), (0,)), ((), ())), preferred_element_type=F32)


def _attn_in_kernel(h_ref, nw_ref, w_ref, qg_ref, kg_ref, cc_ref, ss_ref, q_ref, k_ref, v_ref):
    hn = _rms(h_ref[0], nw_ref[...]).astype(BF16)
    qkv = _dot(hn, w_ref[...])
    cc = cc_ref[...]
    ss = ss_ref[...]
    n_q = q_ref.shape[-1] // HEAD_DIM
    n_kv = k_ref.shape[-1] // HEAD_DIM

    def norm_rope(x, gain):
        y = _rms(x, gain)
        return y * cc + pltpu.roll(y, HEAD_DIM // 2, 1) * ss

    for j in range(n_q):
        sl = slice(j * HEAD_DIM, (j + 1) * HEAD_DIM)
        q_ref[0, :, sl] = (norm_rope(qkv[:, sl], qg_ref[...]) * (HEAD_DIM ** -0.5)).astype(BF16)
    for j in range(n_kv):
        sl = slice(j * HEAD_DIM, (j + 1) * HEAD_DIM)
        src = slice((n_q + j) * HEAD_DIM, (n_q + j + 1) * HEAD_DIM)
        k_ref[0, :, sl] = norm_rope(qkv[:, src], kg_ref[...]).astype(BF16)
    v_ref[0] = qkv[:, (n_q + n_kv) * HEAD_DIM:].astype(BF16)


def _attn_in(h, nw, w_in, q_gain, k_gain, cc, ss, tm):
    bx, sx, d = h.shape
    n_in = w_in.shape[1]
    dq = d
    dkv = (n_in - dq) // 2
    grid = (bx, sx // tm)
    tok = lambda width: pl.BlockSpec((1, tm, width), lambda b, i: (b, i, 0))
    return pl.pallas_call(
        _attn_in_kernel,
        grid=grid,
        in_specs=[
            tok(d),
            _resident((1, d)),
            _resident((d, n_in)),
            _resident((1, HEAD_DIM)),
            _resident((1, HEAD_DIM)),
            pl.BlockSpec((tm, HEAD_DIM), lambda b, i: (i, 0)),
            pl.BlockSpec((tm, HEAD_DIM), lambda b, i: (i, 0)),
        ],
        out_specs=[tok(dq), tok(dkv), tok(dkv)],
        out_shape=[
            jax.ShapeDtypeStruct((bx, sx, dq), BF16),
            jax.ShapeDtypeStruct((bx, sx, dkv), BF16),
            jax.ShapeDtypeStruct((bx, sx, dkv), BF16),
        ],
        compiler_params=_params(2),
        name="attn_in",
    )(h, nw, w_in, q_gain, k_gain, cc, ss)


def _softmax_pv(scores, values, sink_col):
    m = sink_col
    for s in scores:
        m = jnp.maximum(m, jnp.max(s, axis=-1, keepdims=True))
    den = jnp.exp(sink_col - m)
    acc = None
    for s, v in zip(scores, values):
        p = jnp.exp(s - m)
        den = den + jnp.sum(p, axis=-1, keepdims=True)
        pv = _dot(p.astype(BF16), v)
        acc = pv if acc is None else acc + pv
    return acc / den


def _attn_kernel(sink_ref, q_ref, kp_ref, kc_ref, kn_ref, vp_ref, vc_ref, vn_ref, km_ref, vm_ref,
                 o_ref, *, n_blocks):
    n = pl.program_id(1)
    blk = ATTN_BLOCK
    a = lax.broadcasted_iota(jnp.int32, (GROUP * blk, 3 * blk), 0) % blk
    c = lax.broadcasted_iota(jnp.int32, (GROUP * blk, 3 * blk), 1)
    in_prev = (c < blk) & (c >= a) & (n > 0)
    in_own = (c >= blk) & (c < 2 * blk)
    in_next = (c >= 2 * blk) & (c - 2 * blk <= a) & (n < n_blocks - 1)
    mask = in_prev | in_own | in_next
    neg_inf = jnp.float32(-jnp.inf)
    for hk in range(N_KV_HEADS):
        ks = slice(hk * HEAD_DIM, (hk + 1) * HEAD_DIM)
        keys = jnp.concatenate([kp_ref[0, :, ks], kc_ref[0, :, ks], kn_ref[0, :, ks]], axis=0)
        vals = jnp.concatenate([vp_ref[0, :, ks], vc_ref[0, :, ks], vn_ref[0, :, ks]], axis=0)
        heads = [hk * GROUP + g for g in range(GROUP)]
        q4 = jnp.concatenate(
            [q_ref[0, :, h * HEAD_DIM:(h + 1) * HEAD_DIM] for h in heads], axis=0)
        s_band = jnp.where(mask, _dot_nt(q4, keys), neg_inf)
        s_meta = _dot_nt(q4, km_ref[0, :, ks])
        sink_col = jnp.concatenate(
            [jnp.full((blk, 1), sink_ref[h], F32) for h in heads], axis=0)
        o4 = _softmax_pv([s_band, s_meta], [vals, vm_ref[0, :, ks]], sink_col)
        for g, h in enumerate(heads):
            o_ref[0, :, h * HEAD_DIM:(h + 1) * HEAD_DIM] = o4[g * blk:(g + 1) * blk].astype(BF16)


def _attn(q, k, v, km, vm, sink):
    b, s, dq = q.shape
    dkv = k.shape[-1]
    nb = s // ATTN_BLOCK
    blk = ATTN_BLOCK
    prev = pl.BlockSpec((1, blk, dkv), lambda bi, n: (bi, jnp.maximum(n - 1, 0), 0))
    own = pl.BlockSpec((1, blk, dkv), lambda bi, n: (bi, n, 0))
    nxt = pl.BlockSpec((1, blk, dkv), lambda bi, n: (bi, jnp.minimum(n + 1, nb - 1), 0))
    meta = pl.BlockSpec((1, N_META, dkv), lambda bi, n: (bi, 0, 0))
    return pl.pallas_call(
        functools.partial(_attn_kernel, n_blocks=nb),
        grid=(b, nb),
        in_specs=[
            pl.BlockSpec(memory_space=pltpu.SMEM),
            pl.BlockSpec((1, blk, dq), lambda bi, n: (bi, n, 0)),
            prev, own, nxt, prev, own, nxt, meta, meta,
        ],
        out_specs=pl.BlockSpec((1, blk, dq), lambda bi, n: (bi, n, 0)),
        out_shape=jax.ShapeDtypeStruct((b, s, dq), BF16),
        compiler_params=_params(2),
        name="band_attn",
    )(sink, q, k, k, k, v, v, v, km, vm)


def _attn_meta_kernel(sink_ref, qm_ref, km_ref, vm_ref, kr_ref, vr_ref, o_ref):
    rows = GROUP * N_META
    i = lax.broadcasted_iota(jnp.int32, (rows, ATTN_BLOCK), 0) % N_META
    j = lax.broadcasted_iota(jnp.int32, (rows, ATTN_BLOCK), 1)
    allowed = (N_META + j) - i <= WINDOW
    neg_inf = jnp.float32(-jnp.inf)
    for hk in range(N_KV_HEADS):
        ks = slice(hk * HEAD_DIM, (hk + 1) * HEAD_DIM)
        heads = [hk * GROUP + g for g in range(GROUP)]
        q4 = jnp.concatenate(
            [qm_ref[0, :, h * HEAD_DIM:(h + 1) * HEAD_DIM] for h in heads], axis=0)
        s_meta = _dot_nt(q4, km_ref[0, :, ks])
        s_real = jnp.where(allowed, _dot_nt(q4, kr_ref[0, :, ks]), neg_inf)
        sink_col = jnp.concatenate(
            [jnp.full((N_META, 1), sink_ref[h], F32) for h in heads], axis=0)
        o4 = _softmax_pv([s_meta, s_real], [vm_ref[0, :, ks], vr_ref[0, :, ks]], sink_col)
        for g, h in enumerate(heads):
            o_ref[0, :, h * HEAD_DIM:(h + 1) * HEAD_DIM] = (
                o4[g * N_META:(g + 1) * N_META].astype(BF16))


def _attn_meta(qm, km, vm, k, v, sink):
    b, _, dq = qm.shape
    dkv = km.shape[-1]
    meta = lambda width: pl.BlockSpec((1, N_META, width), lambda bi: (bi, 0, 0))
    first = pl.BlockSpec((1, ATTN_BLOCK, dkv), lambda bi: (bi, 0, 0))
    return pl.pallas_call(
        _attn_meta_kernel,
        grid=(b,),
        in_specs=[pl.BlockSpec(memory_space=pltpu.SMEM), meta(dq), meta(dkv), meta(dkv),
                  first, first],
        out_specs=meta(dq),
        out_shape=jax.ShapeDtypeStruct((b, N_META, dq), BF16),
        compiler_params=_params(1),
        name="meta_attn",
    )(sink, qm, km, vm, k, v)


def _post_kernel(*refs, rec, ff_chunk):
    if rec:
        h_ref, of_ref, ob_ref, gate_ref, og_ref, wo_ref, nw_ref, wu_ref, wd_ref, out_ref = refs
        o = of_ref[0].astype(F32) + ob_ref[0].astype(F32)
        n_heads = o.shape[-1] // HEAD_DIM
        o = jnp.concatenate(
            [_rms(o[:, j * HEAD_DIM:(j + 1) * HEAD_DIM], og_ref[...]) for j in range(n_heads)],
            axis=1)
        o = (o * gate_ref[0].astype(F32)).astype(BF16)
    else:
        h_ref, o_ref, wo_ref, nw_ref, wu_ref, wd_ref, out_ref = refs
        o = o_ref[0]
    h1 = h_ref[0] + _dot(o, wo_ref[...])
    hn = _rms(h1, nw_ref[...]).astype(BF16)
    d_ff = wu_ref.shape[1]
    acc = h1
    for c in range(d_ff // ff_chunk):
        sl = slice(c * ff_chunk, (c + 1) * ff_chunk)
        z = jnp.maximum(_dot(hn, wu_ref[:, sl]), 0.0)
        acc = acc + _dot((z * z).astype(BF16), wd_ref[sl, :])
    out_ref[0] = acc


def _post(h, mixer_out, w_out, nw, w_up, w_down, tm, rec_extra=None):
    bx, sx, d = h.shape
    d_ff = w_up.shape[1]
    tok = pl.BlockSpec((1, tm, d), lambda b, i: (b, i, 0))
    rec = rec_extra is not None
    if rec:
        ins = [h, *mixer_out, rec_extra, w_out, nw, w_up, w_down]
        specs = [tok, tok, tok, tok, _resident((1, HEAD_DIM))]
    else:
        ins = [h, mixer_out, w_out, nw, w_up, w_down]
        specs = [tok, tok]
    specs += [_resident((d, d)), _resident((1, d)), _resident((d, d_ff)), _resident((d_ff, d))]
    return pl.pallas_call(
        functools.partial(_post_kernel, rec=rec, ff_chunk=min(1024, d_ff)),
        grid=(bx, sx // tm),
        in_specs=specs,
        out_specs=tok,
        out_shape=jax.ShapeDtypeStruct((bx, sx, d), F32),
        compiler_params=_params(2),
        name="post_rec" if rec else "post_attn",
    )(*ins)


def _lower_bound(tab_ref, layer):
    t = tab_ref[...]
    e = jnp.exp(t - jnp.max(t, axis=0, keepdims=True))
    p = e / jnp.sum(e, axis=0, keepdims=True)
    return jnp.sum(p[1:layer + 1], axis=0, keepdims=True)


def _rec_in_rows(h, nw_ref, w_ref, lbf_ref, lbb_ref, layer):
    d = h.shape[-1]
    hn = _rms(h, nw_ref[...]).astype(BF16)
    proj = lambda j: _dot(hn, w_ref[:, j * d:(j + 1) * d])

    def log_forget(z, tab_ref):
        lb = _lower_bound(tab_ref, layer)
        return jnp.log(lb + (1.0 - lb) * jax.nn.sigmoid(z))

    zq = proj(0)
    q = zq * jax.nn.sigmoid(zq)
    v = proj(1)
    gf = log_forget(proj(2), lbf_ref)
    gb = log_forget(proj(3), lbb_ref)
    zg = proj(4)
    gate = zg * jax.nn.sigmoid(zg)
    return q, v, gf, gb, gate


def _rec_in_kernel(h_ref, nw_ref, w_ref, lbf_ref, lbb_ref,
                   q_ref, v_ref, gf_ref, gb_ref, gate_ref, *, layer):
    q, v, gf, gb, gate = _rec_in_rows(h_ref[0], nw_ref, w_ref, lbf_ref, lbb_ref, layer)
    q_ref[0] = q.astype(BF16)
    v_ref[0] = v.astype(BF16)
    gf_ref[0] = gf
    gb_ref[0] = gb
    gate_ref[0] = gate.astype(BF16)


def _rec_in(h, nw, w_in, lb_f, lb_b, layer, tm):
    b, s, d = h.shape
    depth = lb_f.shape[0]
    tok = pl.BlockSpec((1, tm, d), lambda bi, i: (bi, i, 0))
    sp = s + REC_CHUNK
    return pl.pallas_call(
        functools.partial(_rec_in_kernel, layer=layer),
        grid=(b, s // tm),
        in_specs=[tok, _resident((1, d)), _resident((d, w_in.shape[1])),
                  _resident((depth, d)), _resident((depth, d))],
        out_specs=[tok, tok, tok, tok, tok],
        out_shape=[
            jax.ShapeDtypeStruct((b, sp, d), BF16),
            jax.ShapeDtypeStruct((b, sp, d), BF16),
            jax.ShapeDtypeStruct((b, sp, d), F32),
            jax.ShapeDtypeStruct((b, sp, d), F32),
            jax.ShapeDtypeStruct((b, s, d), BF16),
        ],
        compiler_params=_params(2),
        name="rec_in",
    )(h, nw, w_in, lb_f, lb_b)


def _rec_in_meta_kernel(hm_ref, nw_ref, w_ref, lbf_ref, lbb_ref, q_any, v_any, gf_any, gb_any,
                        q_ref, v_ref, gf_ref, gb_ref, gate_ref, *, layer):
    del q_any, v_any, gf_any, gb_any
    b, n_meta, d = hm_ref.shape
    h = hm_ref[...].reshape(b * n_meta, d)
    q, v, gf, gb, gate = _rec_in_rows(h, nw_ref, w_ref, lbf_ref, lbb_ref, layer)
    pad = REC_CHUNK - n_meta
    for ref, val in ((q_ref, q), (v_ref, v), (gf_ref, gf), (gb_ref, gb)):
        ref[:, :pad, :] = jnp.zeros((b, pad, d), ref.dtype)
        ref[:, pad:, :] = val.reshape(b, n_meta, d).astype(ref.dtype)
    gate_ref[...] = gate.reshape(b, n_meta, d).astype(BF16)


def _rec_in_meta(hm, nw, w_in, lb_f, lb_b, layer, q, v, gf, gb):
    b, n_meta, d = hm.shape
    depth = lb_f.shape[0]
    last_chunk = q.shape[1] // REC_CHUNK - 1
    chunk = pl.BlockSpec((b, REC_CHUNK, d), lambda i: (0, last_chunk, 0))
    whole = pl.BlockSpec((b, n_meta, d), lambda i: (0, 0, 0))
    any_spec = pl.BlockSpec(memory_space=pl.ANY)
    return pl.pallas_call(
        functools.partial(_rec_in_meta_kernel, layer=layer),
        grid=(1,),
        in_specs=[whole, _resident((1, d)), _resident((d, w_in.shape[1])),
                  _resident((depth, d)), _resident((depth, d)),
                  any_spec, any_spec, any_spec, any_spec],
        out_specs=[chunk, chunk, chunk, chunk, whole],
        out_shape=[
            jax.ShapeDtypeStruct(q.shape, q.dtype),
            jax.ShapeDtypeStruct(v.shape, v.dtype),
            jax.ShapeDtypeStruct(gf.shape, gf.dtype),
            jax.ShapeDtypeStruct(gb.shape, gb.dtype),
            jax.ShapeDtypeStruct((b, n_meta, d), BF16),
        ],
        input_output_aliases={5: 0, 6: 1, 7: 2, 8: 3},
        compiler_params=_params(1),
        name="rec_in_meta",
    )(hm, nw, w_in, lb_f, lb_b, q, v, gf, gb)


_LEVEL_HALVES = (32, 16, 8, 4, 2, 1)
_N_LEVELS = len(_LEVEL_HALVES)
_ROWS_Q_INTER = _N_LEVELS * REC_CHUNK
_ROWS_K_STATE = (_N_LEVELS + 1) * REC_CHUNK
_N_SUM_ROWS = (_N_LEVELS + 2) * REC_CHUNK


def _scan_constants(backward):
    c = REC_CHUNK
    w = np.zeros((_N_SUM_ROWS, c), np.float32)
    masks = np.zeros((_N_LEVELS + 1, c, c), np.float32)
    u = np.arange(c)
    for lvl, half in enumerate(_LEVEL_HALVES):
        for t in range(c):
            start = (t // (2 * half)) * 2 * half
            mid = start + half - 1
            if t - start >= half:
                w[lvl * c + t] = (u > mid) & (u <= t)
                masks[lvl, t, start:start + half] = 1.0
            else:
                w[lvl * c + t] = (u > t) & (u <= mid)
    for t in range(c):
        w[_ROWS_Q_INTER + t] = u <= t
        w[_ROWS_K_STATE + t] = u > t
    masks[_N_LEVELS] = np.eye(c, dtype=np.float32)
    if backward:
        w = w.reshape(_N_LEVELS + 2, c, c)[:, ::-1, ::-1].reshape(_N_SUM_ROWS, c)
        masks = masks[:, ::-1, ::-1]
    return jnp.asarray(w, BF16), jnp.asarray(np.ascontiguousarray(masks), F32)


def _scan_kernel(qf_ref, vf_ref, gf_ref, qb_ref, vb_ref, gb_ref, wf_ref, wb_ref, mf_ref, mb_ref,
                 of_ref, ob_ref, st_ref, e_ref):
    c = REC_CHUNK
    n_heads = qf_ref.shape[-1] // HEAD_DIM

    @pl.when(pl.program_id(1) == 0)
    def _():
        st_ref[...] = jnp.zeros(st_ref.shape, F32)

    dirs = ((qf_ref, vf_ref, gf_ref, wf_ref, mf_ref, of_ref, c - 1),
            (qb_ref, vb_ref, gb_ref, wb_ref, mb_ref, ob_ref, 0))

    for d, (_, _, g_ref, w_ref, _, _, _) in enumerate(dirs):
        g = g_ref[0]
        g_hi = g.astype(BF16)
        g_lo = (g - g_hi.astype(F32)).astype(BF16)
        w = w_ref[...]
        e_ref[d] = jnp.exp(_dot(w, g_hi) + _dot(w, g_lo))

    def head_body(hd, carry):
        sl = pl.ds(pl.multiple_of(hd * HEAD_DIM, HEAD_DIM), HEAD_DIM)
        for d, (q_ref, v_ref, g_ref, _, m_ref, o_ref, whole_row) in enumerate(dirs):
            q = q_ref[0, :, sl].astype(F32)
            v = v_ref[0, :, sl]
            k = 1.0 - jnp.exp(g_ref[0, :, sl])
            pair = m_ref[_N_LEVELS] * _dot_nt(q.astype(BF16), k.astype(BF16))
            for lvl in range(_N_LEVELS):
                e = e_ref[d, lvl * c:(lvl + 1) * c, sl]
                pair = pair + m_ref[lvl] * _dot_nt((q * e).astype(BF16), (k * e).astype(BF16))
            q_in = (q * e_ref[d, _ROWS_Q_INTER:_ROWS_Q_INTER + c, sl]).astype(BF16)
            k_out = (k * e_ref[d, _ROWS_K_STATE:_ROWS_K_STATE + c, sl]).astype(BF16)
            state_t = st_ref[d, hd]
            o = _dot(pair.astype(BF16), v) + _dot_nt(q_in, state_t.astype(BF16))
            o_ref[0, :, sl] = o.astype(BF16)
            decay = e_ref[d, pl.ds(_ROWS_Q_INTER + whole_row, 1), sl]
            st_ref[d, hd] = state_t * decay + _dot_tn(v, k_out)
        return carry

    lax.fori_loop(0, n_heads, head_body, 0)


def _scan(q, v, gf, gb):
    b, sp, d = q.shape
    n_all = sp // REC_CHUNK
    n_real = n_all - 1
    n_heads = d // HEAD_DIM
    wf, mf = _scan_constants(False)
    wb, mb = _scan_constants(True)
    fwd = pl.BlockSpec((1, REC_CHUNK, d), lambda bi, j: (bi, (j + n_real) % n_all, 0))
    bwd = pl.BlockSpec((1, REC_CHUNK, d), lambda bi, j: (bi, (2 * n_real - j) % n_all, 0))
    return pl.pallas_call(
        _scan_kernel,
        grid=(b, n_all),
        in_specs=[fwd, fwd, fwd, bwd, bwd, bwd,
                  _resident(wf.shape), _resident(wb.shape),
                  _resident(mf.shape), _resident(mb.shape)],
        out_specs=[fwd, bwd],
        out_shape=[jax.ShapeDtypeStruct((b, sp, d), BF16)] * 2,
        scratch_shapes=[
            pltpu.VMEM((2, n_heads, HEAD_DIM, HEAD_DIM), F32),
            pltpu.VMEM((2, _N_SUM_ROWS, d), F32),
        ],
        compiler_params=_params(2),
        name="gla_scan",
    )(q, v, gf, q, v, gb, wf, wb, mf, mb)


def _rope_tables(length):
    inv_freq = ROPE_THETA ** (-jnp.arange(0, HEAD_DIM, 2, dtype=F32) / HEAD_DIM)
    ang = jnp.arange(length, dtype=F32)[:, None] * inv_freq[None, :]
    cos, sin = jnp.cos(ang), jnp.sin(ang)
    return jnp.concatenate([cos, cos], axis=-1), jnp.concatenate([-sin, sin], axis=-1)


def kernel(x, meta_tokens, mix_norm_w, mlp_norm_w, attn_w_in, attn_w_out, attn_q_norm_w,
           attn_k_norm_w, attn_sink, rec_w_in, rec_w_out, rec_lb_fwd, rec_lb_bwd,
           rec_out_norm_w, mlp_w_up, mlp_w_down):
    b, s, d = x.shape
    depth = mix_norm_w.shape[0]
    n_mixers = 2
    tm = min(TOKEN_TILE, s)
    n_meta_rows = b * N_META

    h = x
    hm = jnp.broadcast_to(meta_tokens.astype(x.dtype)[None], (b, N_META, d))
    cc, ss = _rope_tables(N_META + s)
    cc_meta, ss_meta = jnp.tile(cc[:N_META], (b, 1)), jnp.tile(ss[:N_META], (b, 1))
    cc_real, ss_real = cc[N_META:], ss[N_META:]
    row = lambda w: w.reshape(1, -1)
    flat_meta = lambda t: t.reshape(1, n_meta_rows, t.shape[-1])
    per_batch = lambda t: t.reshape(b, N_META, t.shape[-1])

    for layer in range(depth):
        j = layer // n_mixers
        nw = row(mix_norm_w[layer])
        mlp = (row(mlp_norm_w[layer]), mlp_w_up[layer].astype(BF16),
               mlp_w_down[layer].astype(BF16))
        if layer % n_mixers == 0:
            w_in = attn_w_in[j].astype(BF16)
            w_out = attn_w_out[j].astype(BF16)
            gains = (row(attn_q_norm_w[j]), row(attn_k_norm_w[j]))
            q, k, v = _attn_in(h, nw, w_in, *gains, cc_real, ss_real, tm)
            qm, km, vm = map(per_batch, _attn_in(
                flat_meta(hm), nw, w_in, *gains, cc_meta, ss_meta, n_meta_rows))
            sink = attn_sink[j].astype(F32)
            o = _attn(q, k, v, km, vm, sink)
            om = _attn_meta(qm, km, vm, k, v, sink)
            h = _post(h, o, w_out, *mlp, tm)
            hm = per_batch(_post(flat_meta(hm), flat_meta(om), w_out, *mlp, n_meta_rows))
        else:
            w_in = rec_w_in[j].astype(BF16)
            w_out = rec_w_out[j].astype(BF16)
            gain = row(rec_out_norm_w[j])
            q, v, gf, gb, gate = _rec_in(h, nw, w_in, rec_lb_fwd, rec_lb_bwd, layer, tm)
            q, v, gf, gb, gate_m = _rec_in_meta(hm, nw, w_in, rec_lb_fwd, rec_lb_bwd, layer,
                                                q, v, gf, gb)
            o_f, o_b = _scan(q, v, gf, gb)
            h = _post(h, (o_f, o_b, gate), w_out, *mlp, tm, rec_extra=gain)
            meta_rows = lambda t: flat_meta(t[:, s + REC_CHUNK - N_META:])
            hm = per_batch(_post(flat_meta(hm), (meta_rows(o_f), meta_rows(o_b),
                                                 flat_meta(gate_m)),
                                 w_out, *mlp, n_meta_rows, rec_extra=gain))
    return h
```

```python
import functools

import numpy as np
import jax
import jax.numpy as jnp
from jax import lax
from jax.experimental import pallas as pl
from jax.experimental.pallas import tpu as pltpu

N_META = 16
HEAD_DIM = 128
N_KV_HEADS = 2
GROUP = 4
WINDOW = 128
ATTN_BLOCK = 128
ROPE_THETA = 10000.0
REC_CHUNK = 64
EPS = 1e-6

F32 = jnp.float32
BF16 = jnp.bfloat16

V7X_VMEM_LIMIT_BYTES = 56 * 1024 * 1024
TOKEN_TILE = 512


def _params(n_axes):
    return pltpu.CompilerParams(
        dimension_semantics=("arbitrary",) * n_axes,
        vmem_limit_bytes=V7X_VMEM_LIMIT_BYTES,
    )


def _resident(shape):
    zeros = (0,) * len(shape)
    return pl.BlockSpec(shape, lambda *_: zeros, pipeline_mode=pl.Buffered(1))


def _rms(x, w):
    ms = jnp.mean(x * x, axis=-1, keepdims=True)
    return x * lax.rsqrt(ms + EPS) * w


def _dot(a, b):
    return jnp.dot(a, b, preferred_element_type=F32)


def _dot_nt(a, b):
    return lax.dot_general(a, b, (((1,), (1,)), ((), ())), preferred_element_type=F32)


def _dot_tn(a, b):
    return lax.dot_general(a, b, (((0,), (0,)), ((), ())), preferred_element_type=F32)


def _attn_in_kernel(h_ref, nw_ref, w_ref, qg_ref, kg_ref, cc_ref, ss_ref, q_ref, k_ref, v_ref):
    hn = _rms(h_ref[0], nw_ref[...]).astype(BF16)
    qkv = _dot(hn, w_ref[...])
    cc = cc_ref[...]
    ss = ss_ref[...]
    n_q = q_ref.shape[-1] // HEAD_DIM
    n_kv = k_ref.shape[-1] // HEAD_DIM

    def norm_rope(x, gain):
        y = _rms(x, gain)
        return y * cc + pltpu.roll(y, HEAD_DIM // 2, 1) * ss

    for j in range(n_q):
        sl = slice(j * HEAD_DIM, (j + 1) * HEAD_DIM)
        q_ref[0, :, sl] = (norm_rope(qkv[:, sl], qg_ref[...]) * (HEAD_DIM ** -0.5)).astype(BF16)
    for j in range(n_kv):
        sl = slice(j * HEAD_DIM, (j + 1) * HEAD_DIM)
        src = slice((n_q + j) * HEAD_DIM, (n_q + j + 1) * HEAD_DIM)
        k_ref[0, :, sl] = norm_rope(qkv[:, src], kg_ref[...]).astype(BF16)
    v_ref[0] = qkv[:, (n_q + n_kv) * HEAD_DIM:].astype(BF16)


def _attn_in(h, nw, w_in, q_gain, k_gain, cc, ss, tm):
    bx, sx, d = h.shape
    n_in = w_in.shape[1]
    dq = d
    dkv = (n_in - dq) // 2
    grid = (bx, sx // tm)
    tok = lambda width: pl.BlockSpec((1, tm, width), lambda b, i: (b, i, 0))
    return pl.pallas_call(
        _attn_in_kernel,
        grid=grid,
        in_specs=[
            tok(d),
            _resident((1, d)),
            _resident((d, n_in)),
            _resident((1, HEAD_DIM)),
            _resident((1, HEAD_DIM)),
            pl.BlockSpec((tm, HEAD_DIM), lambda b, i: (i, 0)),
            pl.BlockSpec((tm, HEAD_DIM), lambda b, i: (i, 0)),
        ],
        out_specs=[tok(dq), tok(dkv), tok(dkv)],
        out_shape=[
            jax.ShapeDtypeStruct((bx, sx, dq), BF16),
            jax.ShapeDtypeStruct((bx, sx, dkv), BF16),
            jax.ShapeDtypeStruct((bx, sx, dkv), BF16),
        ],
        compiler_params=_params(2),
        name="attn_in",
    )(h, nw, w_in, q_gain, k_gain, cc, ss)


def _softmax_pv(scores, values, sink_col):
    m = sink_col
    for s in scores:
        m = jnp.maximum(m, jnp.max(s, axis=-1, keepdims=True))
    den = jnp.exp(sink_col - m)
    acc = None
    for s, v in zip(scores, values):
        p = jnp.exp(s - m)
        den = den + jnp.sum(p, axis=-1, keepdims=True)
        pv = _dot(p.astype(BF16), v)
        acc = pv if acc is None else acc + pv
    return acc / den


def _attn_kernel(sink_ref, q_ref, kp_ref, kc_ref, kn_ref, vp_ref, vc_ref, vn_ref, km_ref, vm_ref,
                 o_ref, *, n_blocks):
    n = pl.program_id(1)
    blk = ATTN_BLOCK
    a = lax.broadcasted_iota(jnp.int32, (GROUP * blk, 3 * blk), 0) % blk
    c = lax.broadcasted_iota(jnp.int32, (GROUP * blk, 3 * blk), 1)
    in_prev = (c < blk) & (c >= a) & (n > 0)
    in_own = (c >= blk) & (c < 2 * blk)
    in_next = (c >= 2 * blk) & (c - 2 * blk <= a) & (n < n_blocks - 1)
    mask = in_prev | in_own | in_next
    neg_inf = jnp.float32(-jnp.inf)
    for hk in range(N_KV_HEADS):
        ks = slice(hk * HEAD_DIM, (hk + 1) * HEAD_DIM)
        keys = jnp.concatenate([kp_ref[0, :, ks], kc_ref[0, :, ks], kn_ref[0, :, ks]], axis=0)
        vals = jnp.concatenate([vp_ref[0, :, ks], vc_ref[0, :, ks], vn_ref[0, :, ks]], axis=0)
        heads = [hk * GROUP + g for g in range(GROUP)]
        q4 = jnp.concatenate(
            [q_ref[0, :, h * HEAD_DIM:(h + 1) * HEAD_DIM] for h in heads], axis=0)
        s_band = jnp.where(mask, _dot_nt(q4, keys), neg_inf)
        s_meta = _dot_nt(q4, km_ref[0, :, ks])
        sink_col = jnp.concatenate(
            [jnp.full((blk, 1), sink_ref[h], F32) for h in heads], axis=0)
        o4 = _softmax_pv([s_band, s_meta], [vals, vm_ref[0, :, ks]], sink_col)
        for g, h in enumerate(heads):
            o_ref[0, :, h * HEAD_DIM:(h + 1) * HEAD_DIM] = o4[g * blk:(g + 1) * blk].astype(BF16)


def _attn(q, k, v, km, vm, sink):
    b, s, dq = q.shape
    dkv = k.shape[-1]
    nb = s // ATTN_BLOCK
    blk = ATTN_BLOCK
    prev = pl.BlockSpec((1, blk, dkv), lambda bi, n: (bi, jnp.maximum(n - 1, 0), 0))
    own = pl.BlockSpec((1, blk, dkv), lambda bi, n: (bi, n, 0))
    nxt = pl.BlockSpec((1, blk, dkv), lambda bi, n: (bi, jnp.minimum(n + 1, nb - 1), 0))
    meta = pl.BlockSpec((1, N_META, dkv), lambda bi, n: (bi, 0, 0))
    return pl.pallas_call(
        functools.partial(_attn_kernel, n_blocks=nb),
        grid=(b, nb),
        in_specs=[
            pl.BlockSpec(memory_space=pltpu.SMEM),
            pl.BlockSpec((1, blk, dq), lambda bi, n: (bi, n, 0)),
            prev, own, nxt, prev, own, nxt, meta, meta,
        ],
        out_specs=pl.BlockSpec((1, blk, dq), lambda bi, n: (bi, n, 0)),
        out_shape=jax.ShapeDtypeStruct((b, s, dq), BF16),
        compiler_params=_params(2),
        name="band_attn",
    )(sink, q, k, k, k, v, v, v, km, vm)


def _attn_meta_kernel(sink_ref, qm_ref, km_ref, vm_ref, kr_ref, vr_ref, o_ref):
    rows = GROUP * N_META
    i = lax.broadcasted_iota(jnp.int32, (rows, ATTN_BLOCK), 0) % N_META
    j = lax.broadcasted_iota(jnp.int32, (rows, ATTN_BLOCK), 1)
    allowed = (N_META + j) - i <= WINDOW
    neg_inf = jnp.float32(-jnp.inf)
    for hk in range(N_KV_HEADS):
        ks = slice(hk * HEAD_DIM, (hk + 1) * HEAD_DIM)
        heads = [hk * GROUP + g for g in range(GROUP)]
        q4 = jnp.concatenate(
            [qm_ref[0, :, h * HEAD_DIM:(h + 1) * HEAD_DIM] for h in heads], axis=0)
        s_meta = _dot_nt(q4, km_ref[0, :, ks])
        s_real = jnp.where(allowed, _dot_nt(q4, kr_ref[0, :, ks]), neg_inf)
        sink_col = jnp.concatenate(
            [jnp.full((N_META, 1), sink_ref[h], F32) for h in heads], axis=0)
        o4 = _softmax_pv([s_meta, s_real], [vm_ref[0, :, ks], vr_ref[0, :, ks]], sink_col)
        for g, h in enumerate(heads):
            o_ref[0, :, h * HEAD_DIM:(h + 1) * HEAD_DIM] = (
                o4[g * N_META:(g + 1) * N_META].astype(BF16))


def _attn_meta(qm, km, vm, k, v, sink):
    b, _, dq = qm.shape
    dkv = km.shape[-1]
    meta = lambda width: pl.BlockSpec((1, N_META, width), lambda bi: (bi, 0, 0))
    first = pl.BlockSpec((1, ATTN_BLOCK, dkv), lambda bi: (bi, 0, 0))
    return pl.pallas_call(
        _attn_meta_kernel,
        grid=(b,),
        in_specs=[pl.BlockSpec(memory_space=pltpu.SMEM), meta(dq), meta(dkv), meta(dkv),
                  first, first],
        out_specs=meta(dq),
        out_shape=jax.ShapeDtypeStruct((b, N_META, dq), BF16),
        compiler_params=_params(1),
        name="meta_attn",
    )(sink, qm, km, vm, k, v)


def _post_kernel(*refs, rec, ff_chunk):
    if rec:
        h_ref, of_ref, ob_ref, gate_ref, og_ref, wo_ref, nw_ref, wu_ref, wd_ref, out_ref = refs
        o = of_ref[0].astype(F32) + ob_ref[0].astype(F32)
        n_heads = o.shape[-1] // HEAD_DIM
        o = jnp.concatenate(
            [_rms(o[:, j * HEAD_DIM:(j + 1) * HEAD_DIM], og_ref[...]) for j in range(n_heads)],
            axis=1)
        o = (o * gate_ref[0].astype(F32)).astype(BF16)
    else:
        h_ref, o_ref, wo_ref, nw_ref, wu_ref, wd_ref, out_ref = refs
        o = o_ref[0]
    h1 = h_ref[0] + _dot(o, wo_ref[...])
    hn = _rms(h1, nw_ref[...]).astype(BF16)
    d_ff = wu_ref.shape[1]
    acc = h1
    for c in range(d_ff // ff_chunk):
        sl = slice(c * ff_chunk, (c + 1) * ff_chunk)
        z = jnp.maximum(_dot(hn, wu_ref[:, sl]), 0.0)
        acc = acc + _dot((z * z).astype(BF16), wd_ref[sl, :])
    out_ref[0] = acc


def _post(h, mixer_out, w_out, nw, w_up, w_down, tm, rec_extra=None):
    bx, sx, d = h.shape
    d_ff = w_up.shape[1]
    tok = pl.BlockSpec((1, tm, d), lambda b, i: (b, i, 0))
    rec = rec_extra is not None
    if rec:
        ins = [h, *mixer_out, rec_extra, w_out, nw, w_up, w_down]
        specs = [tok, tok, tok, tok, _resident((1, HEAD_DIM))]
    else:
        ins = [h, mixer_out, w_out, nw, w_up, w_down]
        specs = [tok, tok]
    specs += [_resident((d, d)), _resident((1, d)), _resident((d, d_ff)), _resident((d_ff, d))]
    return pl.pallas_call(
        functools.partial(_post_kernel, rec=rec, ff_chunk=min(1024, d_ff)),
        grid=(bx, sx // tm),
        in_specs=specs,
        out_specs=tok,
        out_shape=jax.ShapeDtypeStruct((bx, sx, d), F32),
        compiler_params=_params(2),
        name="post_rec" if rec else "post_attn",
    )(*ins)


def _lower_bound(tab_ref, layer):
    t = tab_ref[...]
    e = jnp.exp(t - jnp.max(t, axis=0, keepdims=True))
    p = e / jnp.sum(e, axis=0, keepdims=True)
    return jnp.sum(p[1:layer + 1], axis=0, keepdims=True)


def _rec_in_rows(h, nw_ref, w_ref, lbf_ref, lbb_ref, layer):
    d = h.shape[-1]
    hn = _rms(h, nw_ref[...]).astype(BF16)
    proj = lambda j: _dot(hn, w_ref[:, j * d:(j + 1) * d])

    def log_forget(z, tab_ref):
        lb = _lower_bound(tab_ref, layer)
        return jnp.log(lb + (1.0 - lb) * jax.nn.sigmoid(z))

    zq = proj(0)
    q = zq * jax.nn.sigmoid(zq)
    v = proj(1)
    gf = log_forget(proj(2), lbf_ref)
    gb = log_forget(proj(3), lbb_ref)
    zg = proj(4)
    gate = zg * jax.nn.sigmoid(zg)
    return q, v, gf, gb, gate


def _rec_in_kernel(h_ref, nw_ref, w_ref, lbf_ref, lbb_ref,
                   q_ref, v_ref, gf_ref, gb_ref, gate_ref, *, layer):
    q, v, gf, gb, gate = _rec_in_rows(h_ref[0], nw_ref, w_ref, lbf_ref, lbb_ref, layer)
    q_ref[0] = q.astype(BF16)
    v_ref[0] = v.astype(BF16)
    gf_ref[0] = gf
    gb_ref[0] = gb
    gate_ref[0] = gate.astype(BF16)


def _rec_in(h, nw, w_in, lb_f, lb_b, layer, tm):
    b, s, d = h.shape
    depth = lb_f.shape[0]
    tok = pl.BlockSpec((1, tm, d), lambda bi, i: (bi, i, 0))
    sp = s + REC_CHUNK
    return pl.pallas_call(
        functools.partial(_rec_in_kernel, layer=layer),
        grid=(b, s // tm),
        in_specs=[tok, _resident((1, d)), _resident((d, w_in.shape[1])),
                  _resident((depth, d)), _resident((depth, d))],
        out_specs=[tok, tok, tok, tok, tok],
        out_shape=[
            jax.ShapeDtypeStruct((b, sp, d), BF16),
            jax.ShapeDtypeStruct((b, sp, d), BF16),
            jax.ShapeDtypeStruct((b, sp, d), F32),
            jax.ShapeDtypeStruct((b, sp, d), F32),
            jax.ShapeDtypeStruct((b, s, d), BF16),
        ],
        compiler_params=_params(2),
        name="rec_in",
    )(h, nw, w_in, lb_f, lb_b)


def _rec_in_meta_kernel(hm_ref, nw_ref, w_ref, lbf_ref, lbb_ref, q_any, v_any, gf_any, gb_any,
                        q_ref, v_ref, gf_ref, gb_ref, gate_ref, *, layer):
    del q_any, v_any, gf_any, gb_any
    b, n_meta, d = hm_ref.shape
    h = hm_ref[...].reshape(b * n_meta, d)
    q, v, gf, gb, gate = _rec_in_rows(h, nw_ref, w_ref, lbf_ref, lbb_ref, layer)
    pad = REC_CHUNK - n_meta
    for ref, val in ((q_ref, q), (v_ref, v), (gf_ref, gf), (gb_ref, gb)):
        ref[:, :pad, :] = jnp.zeros((b, pad, d), ref.dtype)
        ref[:, pad:, :] = val.reshape(b, n_meta, d).astype(ref.dtype)
    gate_ref[...] = gate.reshape(b, n_meta, d).astype(BF16)


def _rec_in_meta(hm, nw, w_in, lb_f, lb_b, layer, q, v, gf, gb):
    b, n_meta, d = hm.shape
    depth = lb_f.shape[0]
    last_chunk = q.shape[1] // REC_CHUNK - 1
    chunk = pl.BlockSpec((b, REC_CHUNK, d), lambda i: (0, last_chunk, 0))
    whole = pl.BlockSpec((b, n_meta, d), lambda i: (0, 0, 0))
    any_spec = pl.BlockSpec(memory_space=pl.ANY)
    return pl.pallas_call(
        functools.partial(_rec_in_meta_kernel, layer=layer),
        grid=(1,),
        in_specs=[whole, _resident((1, d)), _resident((d, w_in.shape[1])),
                  _resident((depth, d)), _resident((depth, d)),
                  any_spec, any_spec, any_spec, any_spec],
        out_specs=[chunk, chunk, chunk, chunk, whole],
        out_shape=[
            jax.ShapeDtypeStruct(q.shape, q.dtype),
            jax.ShapeDtypeStruct(v.shape, v.dtype),
            jax.ShapeDtypeStruct(gf.shape, gf.dtype),
            jax.ShapeDtypeStruct(gb.shape, gb.dtype),
            jax.ShapeDtypeStruct((b, n_meta, d), BF16),
        ],
        input_output_aliases={5: 0, 6: 1, 7: 2, 8: 3},
        compiler_params=_params(1),
        name="rec_in_meta",
    )(hm, nw, w_in, lb_f, lb_b, q, v, gf, gb)


_SUM_LEVEL_HALVES = (32, 16, 8, 4, 2)
_N_SUM_LEVELS = len(_SUM_LEVEL_HALVES)
_MASK_ADJACENT = _N_SUM_LEVELS
_MASK_DIAG = _N_SUM_LEVELS + 1
_ROWS_Q_INTER = _N_SUM_LEVELS * REC_CHUNK
_ROWS_K_STATE = (_N_SUM_LEVELS + 1) * REC_CHUNK
_N_SUM_ROWS = (_N_SUM_LEVELS + 2) * REC_CHUNK
_LOG2E = 1.4426950408889634


def _scan_constants(backward):
    c = REC_CHUNK
    w = np.zeros((_N_SUM_ROWS, c), np.float32)
    masks = np.zeros((_N_SUM_LEVELS + 2, c, c), np.float32)
    u = np.arange(c)
    for lvl, half in enumerate(_SUM_LEVEL_HALVES + (1,)):
        for t in range(c):
            start = (t // (2 * half)) * 2 * half
            mid = start + half - 1
            if t - start >= half:
                masks[lvl, t, start:start + half] = 1.0
                if half > 1:
                    w[lvl * c + t] = (u > mid) & (u <= t)
            elif half > 1:
                w[lvl * c + t] = (u > t) & (u <= mid)
    for t in range(c):
        w[_ROWS_Q_INTER + t] = u <= t
        w[_ROWS_K_STATE + t] = u > t
    masks[_MASK_DIAG] = np.eye(c, dtype=np.float32)
    if backward:
        w = w.reshape(-1, c, c)[:, ::-1, ::-1].reshape(_N_SUM_ROWS, c)
        masks = masks[:, ::-1, ::-1]
    w = np.concatenate([w, w], axis=1)
    return jnp.asarray(w, BF16), jnp.asarray(np.ascontiguousarray(masks), F32)


def _scan_kernel(qf_ref, vf_ref, gf_ref, qb_ref, vb_ref, gb_ref, wf_ref, wb_ref, mf_ref, mb_ref,
                 of_ref, ob_ref, st_ref, e_ref):
    c = REC_CHUNK
    n_heads = qf_ref.shape[-1] // HEAD_DIM

    @pl.when(pl.program_id(1) == 0)
    def _():
        st_ref[...] = jnp.zeros(st_ref.shape, F32)

    dirs = ((qf_ref, vf_ref, gf_ref, wf_ref, mf_ref, of_ref, c - 1),
            (qb_ref, vb_ref, gb_ref, wb_ref, mb_ref, ob_ref, 0))

    for d, (_, _, g_ref, w_ref, _, _, _) in enumerate(dirs):
        g2 = g_ref[0] * _LOG2E
        g_hi = g2.astype(BF16)
        g_lo = (g2 - g_hi.astype(F32)).astype(BF16)
        e_ref[d] = jnp.exp2(_dot(w_ref[...], jnp.concatenate([g_hi, g_lo], axis=0)))

    for hd in range(n_heads):
        sl = slice(hd * HEAD_DIM, (hd + 1) * HEAD_DIM)
        for d, (q_ref, v_ref, g_ref, _, m_ref, o_ref, whole_row) in enumerate(dirs):
            q = q_ref[0, :, sl].astype(F32)
            v = v_ref[0, :, sl]
            f = jnp.exp(g_ref[0, :, sl])
            k = 1.0 - f
            k16 = k.astype(BF16)
            pair = (m_ref[_MASK_DIAG] * _dot_nt(q.astype(BF16), k16)
                    + m_ref[_MASK_ADJACENT] * _dot_nt((q * f).astype(BF16), k16))
            for lvl in range(_N_SUM_LEVELS):
                e = e_ref[d, lvl * c:(lvl + 1) * c, sl]
                pair = pair + m_ref[lvl] * _dot_nt((q * e).astype(BF16), (k * e).astype(BF16))
            q_in = (q * e_ref[d, _ROWS_Q_INTER:_ROWS_Q_INTER + c, sl]).astype(BF16)
            k_out = (k * e_ref[d, _ROWS_K_STATE:_ROWS_K_STATE + c, sl]).astype(BF16)
            state_t = st_ref[d, hd]
            o = _dot(pair.astype(BF16), v) + _dot_nt(q_in, state_t.astype(BF16))
            o_ref[0, :, sl] = o.astype(BF16)
            row = _ROWS_Q_INTER + whole_row
            decay = e_ref[d, row:row + 1, sl]
            st_ref[d, hd] = state_t * decay + _dot_tn(v, k_out)


def _scan(q, v, gf, gb):
    b, sp, d = q.shape
    n_all = sp // REC_CHUNK
    n_real = n_all - 1
    n_heads = d // HEAD_DIM
    wf, mf = _scan_constants(False)
    wb, mb = _scan_constants(True)
    fwd = pl.BlockSpec((1, REC_CHUNK, d), lambda bi, j: (bi, (j + n_real) % n_all, 0))
    bwd = pl.BlockSpec((1, REC_CHUNK, d), lambda bi, j: (bi, (2 * n_real - j) % n_all, 0))
    return pl.pallas_call(
        _scan_kernel,
        grid=(b, n_all),
        in_specs=[fwd, fwd, fwd, bwd, bwd, bwd,
                  _resident(wf.shape), _resident(wb.shape),
                  _resident(mf.shape), _resident(mb.shape)],
        out_specs=[fwd, bwd],
        out_shape=[jax.ShapeDtypeStruct((b, sp, d), BF16)] * 2,
        scratch_shapes=[
            pltpu.VMEM((2, n_heads, HEAD_DIM, HEAD_DIM), F32),
            pltpu.VMEM((2, _N_SUM_ROWS, d), F32),
        ],
        compiler_params=_params(2),
        name="gla_scan",
    )(q, v, gf, q, v, gb, wf, wb, mf, mb)


def _rope_tables(length):
    inv_freq = ROPE_THETA ** (-jnp.arange(0, HEAD_DIM, 2, dtype=F32) / HEAD_DIM)
    ang = jnp.arange(length, dtype=F32)[:, None] * inv_freq[None, :]
    cos, sin = jnp.cos(ang), jnp.sin(ang)
    return jnp.concatenate([cos, cos], axis=-1), jnp.concatenate([-sin, sin], axis=-1)


def kernel(x, meta_tokens, mix_norm_w, mlp_norm_w, attn_w_in, attn_w_out, attn_q_norm_w,
           attn_k_norm_w, attn_sink, rec_w_in, rec_w_out, rec_lb_fwd, rec_lb_bwd,
           rec_out_norm_w, mlp_w_up, mlp_w_down):
    b, s, d = x.shape
    depth = mix_norm_w.shape[0]
    n_mixers = 2
    tm = min(TOKEN_TILE, s)
    n_meta_rows = b * N_META

    h = x
    hm = jnp.broadcast_to(meta_tokens.astype(x.dtype)[None], (b, N_META, d))
    cc, ss = _rope_tables(N_META + s)
    cc_meta, ss_meta = jnp.tile(cc[:N_META], (b, 1)), jnp.tile(ss[:N_META], (b, 1))
    cc_real, ss_real = cc[N_META:], ss[N_META:]
    row = lambda w: w.reshape(1, -1)
    flat_meta = lambda t: t.reshape(1, n_meta_rows, t.shape[-1])
    per_batch = lambda t: t.reshape(b, N_META, t.shape[-1])

    for layer in range(depth):
        j = layer // n_mixers
        nw = row(mix_norm_w[layer])
        mlp = (row(mlp_norm_w[layer]), mlp_w_up[layer].astype(BF16),
               mlp_w_down[layer].astype(BF16))
        if layer % n_mixers == 0:
            w_in = attn_w_in[j].astype(BF16)
            w_out = attn_w_out[j].astype(BF16)
            gains = (row(attn_q_norm_w[j]), row(attn_k_norm_w[j]))
            q, k, v = _attn_in(h, nw, w_in, *gains, cc_real, ss_real, tm)
            qm, km, vm = map(per_batch, _attn_in(
                flat_meta(hm), nw, w_in, *gains, cc_meta, ss_meta, n_meta_rows))
            sink = attn_sink[j].astype(F32)
            o = _attn(q, k, v, km, vm, sink)
            om = _attn_meta(qm, km, vm, k, v, sink)
            h = _post(h, o, w_out, *mlp, tm)
            hm = per_batch(_post(flat_meta(hm), flat_meta(om), w_out, *mlp, n_meta_rows))
        else:
            w_in = rec_w_in[j].astype(BF16)
            w_out = rec_w_out[j].astype(BF16)
            gain = row(rec_out_norm_w[j])
            q, v, gf, gb, gate = _rec_in(h, nw, w_in, rec_lb_fwd, rec_lb_bwd, layer, tm)
            q, v, gf, gb, gate_m = _rec_in_meta(hm, nw, w_in, rec_lb_fwd, rec_lb_bwd, layer,
                                                q, v, gf, gb)
            o_f, o_b = _scan(q, v, gf, gb)
            h = _post(h, (o_f, o_b, gate), w_out, *mlp, tm, rec_extra=gain)
            meta_rows = lambda t: flat_meta(t[:, s + REC_CHUNK - N_META:])
            hm = per_batch(_post(flat_meta(hm), (meta_rows(o_f), meta_rows(o_b),
                                                 flat_meta(gate_m)),
                                 w_out, *mlp, n_meta_rows, rec_extra=gain))
    return h
```

```python
import functools

import numpy as np
import jax
import jax.numpy as jnp
from jax import lax
from jax.experimental import pallas as pl
from jax.experimental.pallas import tpu as pltpu

N_META = 16
HEAD_DIM = 128
N_KV_HEADS = 2
GROUP = 4
WINDOW = 128
ATTN_BLOCK = 128
ROPE_THETA = 10000.0
REC_CHUNK = 64
EPS = 1e-6

F32 = jnp.float32
BF16 = jnp.bfloat16

V7X_VMEM_LIMIT_BYTES = 56 * 1024 * 1024
TOKEN_TILE = 512
SUB_TILE = 256


def _params(n_axes):
    return pltpu.CompilerParams(
        dimension_semantics=("arbitrary",) * n_axes,
        vmem_limit_bytes=V7X_VMEM_LIMIT_BYTES,
    )


def _resident(shape):
    zeros = (0,) * len(shape)
    return pl.BlockSpec(shape, lambda *_: zeros, pipeline_mode=pl.Buffered(1))


def _rms(x, w):
    ms = jnp.mean(x * x, axis=-1, keepdims=True)
    return x * lax.rsqrt(ms + EPS) * w


def _dot(a, b):
    return jnp.dot(a, b, preferred_element_type=F32)


def _dot_nt(a, b):
    return lax.dot_general(a, b, (((1,), (1,)), ((), ())), preferred_element_type=F32)


def _dot_tn(a, b):
    return lax.dot_general(a, b, (((0,), (0,)), ((), ())), preferred_element_type=F32)


def _attn_in_kernel(h_ref, nw_ref, w_ref, qg_ref, kg_ref, cc_ref, ss_ref, q_ref, k_ref, v_ref):
    n_q = q_ref.shape[-1] // HEAD_DIM
    n_kv = k_ref.shape[-1] // HEAD_DIM
    tm = h_ref.shape[1]
    sub = min(tm, SUB_TILE)
    half = HEAD_DIM // 2
    q_gain = qg_ref[...] * (HEAD_DIM ** -0.5)
    k_gain = kg_ref[...]

    for r in range(tm // sub):
        rows = slice(r * sub, (r + 1) * sub)
        hn = _rms(h_ref[0, rows, :], nw_ref[...]).astype(BF16)
        qkv = _dot(hn, w_ref[...])
        cc = cc_ref[rows, :]
        ss = ss_ref[rows, :]
        tabs_q = (q_gain * cc, pltpu.roll(q_gain, half, 1) * ss)
        tabs_k = (k_gain * cc, pltpu.roll(k_gain, half, 1) * ss)

        def norm_rope(x, tabs):
            y = x * lax.rsqrt(jnp.mean(x * x, axis=-1, keepdims=True) + EPS)
            return y * tabs[0] + pltpu.roll(y, half, 1) * tabs[1]

        for j in range(n_q):
            sl = slice(j * HEAD_DIM, (j + 1) * HEAD_DIM)
            q_ref[0, rows, sl] = norm_rope(qkv[:, sl], tabs_q).astype(BF16)
        for j in range(n_kv):
            sl = slice(j * HEAD_DIM, (j + 1) * HEAD_DIM)
            src = slice((n_q + j) * HEAD_DIM, (n_q + j + 1) * HEAD_DIM)
            k_ref[0, rows, sl] = norm_rope(qkv[:, src], tabs_k).astype(BF16)
        v_ref[0, rows, :] = qkv[:, (n_q + n_kv) * HEAD_DIM:].astype(BF16)


def _attn_in(h, nw, w_in, q_gain, k_gain, cc, ss, tm):
    bx, sx, d = h.shape
    n_in = w_in.shape[1]
    dq = d
    dkv = (n_in - dq) // 2
    grid = (bx, sx // tm)
    tok = lambda width: pl.BlockSpec((1, tm, width), lambda b, i: (b, i, 0))
    return pl.pallas_call(
        _attn_in_kernel,
        grid=grid,
        in_specs=[
            tok(d),
            _resident((1, d)),
            _resident((d, n_in)),
            _resident((1, HEAD_DIM)),
            _resident((1, HEAD_DIM)),
            pl.BlockSpec((tm, HEAD_DIM), lambda b, i: (i, 0)),
            pl.BlockSpec((tm, HEAD_DIM), lambda b, i: (i, 0)),
        ],
        out_specs=[tok(dq), tok(dkv), tok(dkv)],
        out_shape=[
            jax.ShapeDtypeStruct((bx, sx, dq), BF16),
            jax.ShapeDtypeStruct((bx, sx, dkv), BF16),
            jax.ShapeDtypeStruct((bx, sx, dkv), BF16),
        ],
        compiler_params=_params(2),
        name="attn_in",
    )(h, nw, w_in, q_gain, k_gain, cc, ss)


def _softmax_pv(scores, values, sink_col):
    m = sink_col
    for s in scores:
        m = jnp.maximum(m, jnp.max(s, axis=-1, keepdims=True))
    den = jnp.exp(sink_col - m)
    acc = None
    for s, v in zip(scores, values):
        p = jnp.exp(s - m)
        den = den + jnp.sum(p, axis=-1, keepdims=True)
        pv = _dot(p.astype(BF16), v)
        acc = pv if acc is None else acc + pv
    return acc / den


def _attn_kernel(sink_ref, q_ref, kp_ref, kc_ref, kn_ref, vp_ref, vc_ref, vn_ref, km_ref, vm_ref,
                 o_ref, *, n_blocks):
    n = pl.program_id(1)
    blk = ATTN_BLOCK
    a = lax.broadcasted_iota(jnp.int32, (GROUP * blk, 3 * blk), 0) % blk
    c = lax.broadcasted_iota(jnp.int32, (GROUP * blk, 3 * blk), 1)
    in_prev = (c < blk) & (c >= a) & (n > 0)
    in_own = (c >= blk) & (c < 2 * blk)
    in_next = (c >= 2 * blk) & (c - 2 * blk <= a) & (n < n_blocks - 1)
    mask = in_prev | in_own | in_next
    neg_inf = jnp.float32(-jnp.inf)
    for hk in range(N_KV_HEADS):
        ks = slice(hk * HEAD_DIM, (hk + 1) * HEAD_DIM)
        keys = jnp.concatenate([kp_ref[0, :, ks], kc_ref[0, :, ks], kn_ref[0, :, ks]], axis=0)
        vals = jnp.concatenate([vp_ref[0, :, ks], vc_ref[0, :, ks], vn_ref[0, :, ks]], axis=0)
        heads = [hk * GROUP + g for g in range(GROUP)]
        q4 = jnp.concatenate(
            [q_ref[0, :, h * HEAD_DIM:(h + 1) * HEAD_DIM] for h in heads], axis=0)
        s_band = jnp.where(mask, _dot_nt(q4, keys), neg_inf)
        s_meta = _dot_nt(q4, km_ref[0, :, ks])
        sink_col = jnp.concatenate(
            [jnp.full((blk, 1), sink_ref[h], F32) for h in heads], axis=0)
        o4 = _softmax_pv([s_band, s_meta], [vals, vm_ref[0, :, ks]], sink_col)
        for g, h in enumerate(heads):
            o_ref[0, :, h * HEAD_DIM:(h + 1) * HEAD_DIM] = o4[g * blk:(g + 1) * blk].astype(BF16)


def _attn(q, k, v, km, vm, sink):
    b, s, dq = q.shape
    dkv = k.shape[-1]
    nb = s // ATTN_BLOCK
    blk = ATTN_BLOCK
    prev = pl.BlockSpec((1, blk, dkv), lambda bi, n: (bi, jnp.maximum(n - 1, 0), 0))
    own = pl.BlockSpec((1, blk, dkv), lambda bi, n: (bi, n, 0))
    nxt = pl.BlockSpec((1, blk, dkv), lambda bi, n: (bi, jnp.minimum(n + 1, nb - 1), 0))
    meta = pl.BlockSpec((1, N_META, dkv), lambda bi, n: (bi, 0, 0))
    return pl.pallas_call(
        functools.partial(_attn_kernel, n_blocks=nb),
        grid=(b, nb),
        in_specs=[
            pl.BlockSpec(memory_space=pltpu.SMEM),
            pl.BlockSpec((1, blk, dq), lambda bi, n: (bi, n, 0)),
            prev, own, nxt, prev, own, nxt, meta, meta,
        ],
        out_specs=pl.BlockSpec((1, blk, dq), lambda bi, n: (bi, n, 0)),
        out_shape=jax.ShapeDtypeStruct((b, s, dq), BF16),
        compiler_params=_params(2),
        name="band_attn",
    )(sink, q, k, k, k, v, v, v, km, vm)


def _attn_meta_kernel(sink_ref, qm_ref, km_ref, vm_ref, kr_ref, vr_ref, o_ref):
    rows = GROUP * N_META
    i = lax.broadcasted_iota(jnp.int32, (rows, ATTN_BLOCK), 0) % N_META
    j = lax.broadcasted_iota(jnp.int32, (rows, ATTN_BLOCK), 1)
    allowed = (N_META + j) - i <= WINDOW
    neg_inf = jnp.float32(-jnp.inf)
    for hk in range(N_KV_HEADS):
        ks = slice(hk * HEAD_DIM, (hk + 1) * HEAD_DIM)
        heads = [hk * GROUP + g for g in range(GROUP)]
        q4 = jnp.concatenate(
            [qm_ref[0, :, h * HEAD_DIM:(h + 1) * HEAD_DIM] for h in heads], axis=0)
        s_meta = _dot_nt(q4, km_ref[0, :, ks])
        s_real = jnp.where(allowed, _dot_nt(q4, kr_ref[0, :, ks]), neg_inf)
        sink_col = jnp.concatenate(
            [jnp.full((N_META, 1), sink_ref[h], F32) for h in heads], axis=0)
        o4 = _softmax_pv([s_meta, s_real], [vm_ref[0, :, ks], vr_ref[0, :, ks]], sink_col)
        for g, h in enumerate(heads):
            o_ref[0, :, h * HEAD_DIM:(h + 1) * HEAD_DIM] = (
                o4[g * N_META:(g + 1) * N_META].astype(BF16))


def _attn_meta(qm, km, vm, k, v, sink):
    b, _, dq = qm.shape
    dkv = km.shape[-1]
    meta = lambda width: pl.BlockSpec((1, N_META, width), lambda bi: (bi, 0, 0))
    first = pl.BlockSpec((1, ATTN_BLOCK, dkv), lambda bi: (bi, 0, 0))
    return pl.pallas_call(
        _attn_meta_kernel,
        grid=(b,),
        in_specs=[pl.BlockSpec(memory_space=pltpu.SMEM), meta(dq), meta(dkv), meta(dkv),
                  first, first],
        out_specs=meta(dq),
        out_shape=jax.ShapeDtypeStruct((b, N_META, dq), BF16),
        compiler_params=_params(1),
        name="meta_attn",
    )(sink, qm, km, vm, k, v)


def _post_kernel(*refs, rec, ff_chunk):
    if rec:
        h_ref, of_ref, ob_ref, gate_ref, og_ref, wo_ref, nw_ref, wu_ref, wd_ref, out_ref = refs
        o = of_ref[0].astype(F32) + ob_ref[0].astype(F32)
        n_heads = o.shape[-1] // HEAD_DIM
        o = jnp.concatenate(
            [_rms(o[:, j * HEAD_DIM:(j + 1) * HEAD_DIM], og_ref[...]) for j in range(n_heads)],
            axis=1)
        o = (o * gate_ref[0].astype(F32)).astype(BF16)
    else:
        h_ref, o_ref, wo_ref, nw_ref, wu_ref, wd_ref, out_ref = refs
        o = o_ref[0]
    h1 = h_ref[0] + _dot(o, wo_ref[...])
    hn = _rms(h1, nw_ref[...]).astype(BF16)
    d_ff = wu_ref.shape[1]
    acc = h1
    for c in range(d_ff // ff_chunk):
        sl = slice(c * ff_chunk, (c + 1) * ff_chunk)
        z = jnp.maximum(_dot(hn, wu_ref[:, sl]), 0.0)
        acc = acc + _dot((z * z).astype(BF16), wd_ref[sl, :])
    out_ref[0] = acc


def _post(h, mixer_out, w_out, nw, w_up, w_down, tm, rec_extra=None):
    bx, sx, d = h.shape
    d_ff = w_up.shape[1]
    tok = pl.BlockSpec((1, tm, d), lambda b, i: (b, i, 0))
    rec = rec_extra is not None
    if rec:
        ins = [h, *mixer_out, rec_extra, w_out, nw, w_up, w_down]
        specs = [tok, tok, tok, tok, _resident((1, HEAD_DIM))]
    else:
        ins = [h, mixer_out, w_out, nw, w_up, w_down]
        specs = [tok, tok]
    specs += [_resident((d, d)), _resident((1, d)), _resident((d, d_ff)), _resident((d_ff, d))]
    return pl.pallas_call(
        functools.partial(_post_kernel, rec=rec, ff_chunk=min(1024, d_ff)),
        grid=(bx, sx // tm),
        in_specs=specs,
        out_specs=tok,
        out_shape=jax.ShapeDtypeStruct((bx, sx, d), F32),
        compiler_params=_params(2),
        name="post_rec" if rec else "post_attn",
    )(*ins)


def _lower_bound(tab_ref, layer):
    t = tab_ref[...]
    e = jnp.exp(t - jnp.max(t, axis=0, keepdims=True))
    p = e / jnp.sum(e, axis=0, keepdims=True)
    return jnp.sum(p[1:layer + 1], axis=0, keepdims=True)


def _rec_in_rows(h, nw_ref, w_ref, lbf_ref, lbb_ref, layer):
    d = h.shape[-1]
    hn = _rms(h, nw_ref[...]).astype(BF16)
    proj = lambda j: _dot(hn, w_ref[:, j * d:(j + 1) * d])

    def log_forget(z, tab_ref):
        lb = _lower_bound(tab_ref, layer)
        return jnp.log(lb + (1.0 - lb) * jax.nn.sigmoid(z))

    zq = proj(0)
    q = zq * jax.nn.sigmoid(zq)
    v = proj(1)
    gf = log_forget(proj(2), lbf_ref)
    gb = log_forget(proj(3), lbb_ref)
    zg = proj(4)
    gate = zg * jax.nn.sigmoid(zg)
    return q, v, gf, gb, gate


def _rec_in_kernel(h_ref, nw_ref, w_ref, lbf_ref, lbb_ref,
                   q_ref, v_ref, gf_ref, gb_ref, gate_ref, *, layer):
    tm = h_ref.shape[1]
    sub = min(tm, SUB_TILE)
    for r in range(tm // sub):
        rows = slice(r * sub, (r + 1) * sub)
        q, v, gf, gb, gate = _rec_in_rows(h_ref[0, rows, :], nw_ref, w_ref, lbf_ref, lbb_ref,
                                          layer)
        q_ref[0, rows, :] = q.astype(BF16)
        v_ref[0, rows, :] = v.astype(BF16)
        gf_ref[0, rows, :] = gf
        gb_ref[0, rows, :] = gb
        gate_ref[0, rows, :] = gate.astype(BF16)


def _rec_in(h, nw, w_in, lb_f, lb_b, layer, tm):
    b, s, d = h.shape
    depth = lb_f.shape[0]
    tok = pl.BlockSpec((1, tm, d), lambda bi, i: (bi, i, 0))
    sp = s + REC_CHUNK
    return pl.pallas_call(
        functools.partial(_rec_in_kernel, layer=layer),
        grid=(b, s // tm),
        in_specs=[tok, _resident((1, d)), _resident((d, w_in.shape[1])),
                  _resident((depth, d)), _resident((depth, d))],
        out_specs=[tok, tok, tok, tok, tok],
        out_shape=[
            jax.ShapeDtypeStruct((b, sp, d), BF16),
            jax.ShapeDtypeStruct((b, sp, d), BF16),
            jax.ShapeDtypeStruct((b, sp, d), F32),
            jax.ShapeDtypeStruct((b, sp, d), F32),
            jax.ShapeDtypeStruct((b, s, d), BF16),
        ],
        compiler_params=_params(2),
        name="rec_in",
    )(h, nw, w_in, lb_f, lb_b)


def _rec_in_meta_kernel(hm_ref, nw_ref, w_ref, lbf_ref, lbb_ref, q_any, v_any, gf_any, gb_any,
                        q_ref, v_ref, gf_ref, gb_ref, gate_ref, *, layer):
    del q_any, v_any, gf_any, gb_any
    b, n_meta, d = hm_ref.shape
    h = hm_ref[...].reshape(b * n_meta, d)
    q, v, gf, gb, gate = _rec_in_rows(h, nw_ref, w_ref, lbf_ref, lbb_ref, layer)
    pad = REC_CHUNK - n_meta
    for ref, val in ((q_ref, q), (v_ref, v), (gf_ref, gf), (gb_ref, gb)):
        ref[:, :pad, :] = jnp.zeros((b, pad, d), ref.dtype)
        ref[:, pad:, :] = val.reshape(b, n_meta, d).astype(ref.dtype)
    gate_ref[...] = gate.reshape(b, n_meta, d).astype(BF16)


def _rec_in_meta(hm, nw, w_in, lb_f, lb_b, layer, q, v, gf, gb):
    b, n_meta, d = hm.shape
    depth = lb_f.shape[0]
    last_chunk = q.shape[1] // REC_CHUNK - 1
    chunk = pl.BlockSpec((b, REC_CHUNK, d), lambda i: (0, last_chunk, 0))
    whole = pl.BlockSpec((b, n_meta, d), lambda i: (0, 0, 0))
    any_spec = pl.BlockSpec(memory_space=pl.ANY)
    return pl.pallas_call(
        functools.partial(_rec_in_meta_kernel, layer=layer),
        grid=(1,),
        in_specs=[whole, _resident((1, d)), _resident((d, w_in.shape[1])),
                  _resident((depth, d)), _resident((depth, d)),
                  any_spec, any_spec, any_spec, any_spec],
        out_specs=[chunk, chunk, chunk, chunk, whole],
        out_shape=[
            jax.ShapeDtypeStruct(q.shape, q.dtype),
            jax.ShapeDtypeStruct(v.shape, v.dtype),
            jax.ShapeDtypeStruct(gf.shape, gf.dtype),
            jax.ShapeDtypeStruct(gb.shape, gb.dtype),
            jax.ShapeDtypeStruct((b, n_meta, d), BF16),
        ],
        input_output_aliases={5: 0, 6: 1, 7: 2, 8: 3},
        compiler_params=_params(1),
        name="rec_in_meta",
    )(hm, nw, w_in, lb_f, lb_b, q, v, gf, gb)


_SUM_LEVEL_HALVES = (32, 16, 8, 4, 2)
_N_SUM_LEVELS = len(_SUM_LEVEL_HALVES)
_MASK_ADJACENT = _N_SUM_LEVELS
_MASK_DIAG = _N_SUM_LEVELS + 1
_ROWS_Q_INTER = _N_SUM_LEVELS * REC_CHUNK
_ROWS_K_STATE = (_N_SUM_LEVELS + 1) * REC_CHUNK
_N_SUM_ROWS = (_N_SUM_LEVELS + 2) * REC_CHUNK
_LOG2E = 1.4426950408889634
SCAN_PIPELINE_LAG = 2


def _scan_constants(backward):
    c = REC_CHUNK
    w = np.zeros((_N_SUM_ROWS, c), np.float32)
    masks = np.zeros((_N_SUM_LEVELS + 2, c, c), np.float32)
    u = np.arange(c)
    for lvl, half in enumerate(_SUM_LEVEL_HALVES + (1,)):
        for t in range(c):
            start = (t // (2 * half)) * 2 * half
            mid = start + half - 1
            if t - start >= half:
                masks[lvl, t, start:start + half] = 1.0
                if half > 1:
                    w[lvl * c + t] = (u > mid) & (u <= t)
            elif half > 1:
                w[lvl * c + t] = (u > t) & (u <= mid)
    for t in range(c):
        w[_ROWS_Q_INTER + t] = u <= t
        w[_ROWS_K_STATE + t] = u > t
    masks[_MASK_DIAG] = np.eye(c, dtype=np.float32)
    if backward:
        w = w.reshape(-1, c, c)[:, ::-1, ::-1].reshape(_N_SUM_ROWS, c)
        masks = masks[:, ::-1, ::-1]
    w = np.concatenate([w, w], axis=1)
    return jnp.asarray(w, BF16), jnp.asarray(np.ascontiguousarray(masks), F32)


def _scan_kernel(qf_ref, vf_ref, gf_ref, qb_ref, vb_ref, gb_ref, wf_ref, wb_ref, mf_ref, mb_ref,
                 of_ref, ob_ref, st_ref, e_ref):
    c = REC_CHUNK
    n_heads = qf_ref.shape[-1] // HEAD_DIM

    @pl.when(pl.program_id(1) == 0)
    def _():
        st_ref[...] = jnp.zeros(st_ref.shape, F32)

    dirs = ((qf_ref, vf_ref, gf_ref, wf_ref, mf_ref, of_ref, c - 1),
            (qb_ref, vb_ref, gb_ref, wb_ref, mb_ref, ob_ref, 0))

    for d, (_, _, g_ref, w_ref, _, _, _) in enumerate(dirs):
        g2 = g_ref[0] * _LOG2E
        g_hi = g2.astype(BF16)
        g_lo = (g2 - g_hi.astype(F32)).astype(BF16)
        e_ref[d] = jnp.exp2(_dot(w_ref[...], jnp.concatenate([g_hi, g_lo], axis=0)))

    def pair_stage(hd, d):
        q_ref, _, g_ref, _, m_ref, _, _ = dirs[d]
        sl = slice(hd * HEAD_DIM, (hd + 1) * HEAD_DIM)
        q = q_ref[0, :, sl].astype(F32)
        f = jnp.exp(g_ref[0, :, sl])
        k = 1.0 - f
        near = _dot_nt(jnp.concatenate([q, q * f], axis=0).astype(BF16), k.astype(BF16))
        pair = m_ref[_MASK_DIAG] * near[:c] + m_ref[_MASK_ADJACENT] * near[c:]
        for lvl in range(_N_SUM_LEVELS):
            e = e_ref[d, lvl * c:(lvl + 1) * c, sl]
            pair = pair + m_ref[lvl] * _dot_nt((q * e).astype(BF16), (k * e).astype(BF16))
        q_in = (q * e_ref[d, _ROWS_Q_INTER:_ROWS_Q_INTER + c, sl]).astype(BF16)
        k_out = (k * e_ref[d, _ROWS_K_STATE:_ROWS_K_STATE + c, sl]).astype(BF16)
        return pair.astype(BF16), q_in, k_out

    def output_stage(hd, d, pair, q_in, k_out):
        _, v_ref, _, _, _, o_ref, whole_row = dirs[d]
        sl = slice(hd * HEAD_DIM, (hd + 1) * HEAD_DIM)
        v = v_ref[0, :, sl]
        state_t = st_ref[d, hd]
        o = _dot(pair, v) + _dot_nt(q_in, state_t.astype(BF16))
        o_ref[0, :, sl] = o.astype(BF16)
        row = _ROWS_Q_INTER + whole_row
        decay = e_ref[d, row:row + 1, sl]
        st_ref[d, hd] = state_t * decay + _dot_tn(v, k_out)

    pending = []
    for hd in range(n_heads):
        for d in range(len(dirs)):
            pending.append((hd, d) + pair_stage(hd, d))
            if len(pending) > SCAN_PIPELINE_LAG:
                output_stage(*pending.pop(0))
    for item in pending:
        output_stage(*item)


def _scan(q, v, gf, gb):
    b, sp, d = q.shape
    n_all = sp // REC_CHUNK
    n_real = n_all - 1
    n_heads = d // HEAD_DIM
    wf, mf = _scan_constants(False)
    wb, mb = _scan_constants(True)
    fwd = pl.BlockSpec((1, REC_CHUNK, d), lambda bi, j: (bi, (j + n_real) % n_all, 0))
    bwd = pl.BlockSpec((1, REC_CHUNK, d), lambda bi, j: (bi, (2 * n_real - j) % n_all, 0))
    return pl.pallas_call(
        _scan_kernel,
        grid=(b, n_all),
        in_specs=[fwd, fwd, fwd, bwd, bwd, bwd,
                  _resident(wf.shape), _resident(wb.shape),
                  _resident(mf.shape), _resident(mb.shape)],
        out_specs=[fwd, bwd],
        out_shape=[jax.ShapeDtypeStruct((b, sp, d), BF16)] * 2,
        scratch_shapes=[
            pltpu.VMEM((2, n_heads, HEAD_DIM, HEAD_DIM), F32),
            pltpu.VMEM((2, _N_SUM_ROWS, d), F32),
        ],
        compiler_params=_params(2),
        name="gla_scan",
    )(q, v, gf, q, v, gb, wf, wb, mf, mb)


def _rope_tables(length):
    inv_freq = ROPE_THETA ** (-jnp.arange(0, HEAD_DIM, 2, dtype=F32) / HEAD_DIM)
    ang = jnp.arange(length, dtype=F32)[:, None] * inv_freq[None, :]
    cos, sin = jnp.cos(ang), jnp.sin(ang)
    return jnp.concatenate([cos, cos], axis=-1), jnp.concatenate([-sin, sin], axis=-1)


def kernel(x, meta_tokens, mix_norm_w, mlp_norm_w, attn_w_in, attn_w_out, attn_q_norm_w,
           attn_k_norm_w, attn_sink, rec_w_in, rec_w_out, rec_lb_fwd, rec_lb_bwd,
           rec_out_norm_w, mlp_w_up, mlp_w_down):
    b, s, d = x.shape
    depth = mix_norm_w.shape[0]
    n_mixers = 2
    tm = min(TOKEN_TILE, s)
    n_meta_rows = b * N_META

    h = x
    hm = jnp.broadcast_to(meta_tokens.astype(x.dtype)[None], (b, N_META, d))
    cc, ss = _rope_tables(N_META + s)
    cc_meta, ss_meta = jnp.tile(cc[:N_META], (b, 1)), jnp.tile(ss[:N_META], (b, 1))
    cc_real, ss_real = cc[N_META:], ss[N_META:]
    row = lambda w: w.reshape(1, -1)
    flat_meta = lambda t: t.reshape(1, n_meta_rows, t.shape[-1])
    per_batch = lambda t: t.reshape(b, N_META, t.shape[-1])

    for layer in range(depth):
        j = layer // n_mixers
        nw = row(mix_norm_w[layer])
        mlp = (row(mlp_norm_w[layer]), mlp_w_up[layer].astype(BF16),
               mlp_w_down[layer].astype(BF16))
        if layer % n_mixers == 0:
            w_in = attn_w_in[j].astype(BF16)
            w_out = attn_w_out[j].astype(BF16)
            gains = (row(attn_q_norm_w[j]), row(attn_k_norm_w[j]))
            q, k, v = _attn_in(h, nw, w_in, *gains, cc_real, ss_real, tm)
            qm, km, vm = map(per_batch, _attn_in(
                flat_meta(hm), nw, w_in, *gains, cc_meta, ss_meta, n_meta_rows))
            sink = attn_sink[j].astype(F32)
            o = _attn(q, k, v, km, vm, sink)
            om = _attn_meta(qm, km, vm, k, v, sink)
            h = _post(h, o, w_out, *mlp, tm)
            hm = per_batch(_post(flat_meta(hm), flat_meta(om), w_out, *mlp, n_meta_rows))
        else:
            w_in = rec_w_in[j].astype(BF16)
            w_out = rec_w_out[j].astype(BF16)
            gain = row(rec_out_norm_w[j])
            q, v, gf, gb, gate = _rec_in(h, nw, w_in, rec_lb_fwd, rec_lb_bwd, layer, tm)
            q, v, gf, gb, gate_m = _rec_in_meta(hm, nw, w_in, rec_lb_fwd, rec_lb_bwd, layer,
                                                q, v, gf, gb)
            o_f, o_b = _scan(q, v, gf, gb)
            h = _post(h, (o_f, o_b, gate), w_out, *mlp, tm, rec_extra=gain)
            meta_rows = lambda t: flat_meta(t[:, s + REC_CHUNK - N_META:])
            hm = per_batch(_post(flat_meta(hm), (meta_rows(o_f), meta_rows(o_b),
                                                 flat_meta(gate_m)),
                                 w_out, *mlp, n_meta_rows, rec_extra=gain))
    return h
```

```python
import functools

import numpy as np
import jax
import jax.numpy as jnp
from jax import lax
from jax.experimental import pallas as pl
from jax.experimental.pallas import tpu as pltpu

N_META = 16
HEAD_DIM = 128
N_KV_HEADS = 2
GROUP = 4
WINDOW = 128
ATTN_BLOCK = 128
ROPE_THETA = 10000.0
REC_CHUNK = 64
EPS = 1e-6

F32 = jnp.float32
BF16 = jnp.bfloat16

V7X_VMEM_LIMIT_BYTES = 56 * 1024 * 1024
TOKEN_TILE = 512
SUB_TILE = 256


def _params(n_axes):
    return pltpu.CompilerParams(
        dimension_semantics=("arbitrary",) * n_axes,
        vmem_limit_bytes=V7X_VMEM_LIMIT_BYTES,
    )


def _resident(shape):
    zeros = (0,) * len(shape)
    return pl.BlockSpec(shape, lambda *_: zeros, pipeline_mode=pl.Buffered(1))


def _rms(x, w):
    ms = jnp.mean(x * x, axis=-1, keepdims=True)
    return x * lax.rsqrt(ms + EPS) * w


def _dot(a, b):
    return jnp.dot(a, b, preferred_element_type=F32)


def _dot_nt(a, b):
    return lax.dot_general(a, b, (((1,), (1,)), ((), ())), preferred_element_type=F32)


def _dot_tn(a, b):
    return lax.dot_general(a, b, (((0,), (0,)), ((), ())), preferred_element_type=F32)


def _attn_in_kernel(h_ref, nw_ref, w_ref, qg_ref, kg_ref, cc_ref, ss_ref, q_ref, k_ref, v_ref):
    n_q = q_ref.shape[-1] // HEAD_DIM
    n_kv = k_ref.shape[-1] // HEAD_DIM
    tm = h_ref.shape[1]
    sub = min(tm, SUB_TILE)
    half = HEAD_DIM // 2
    q_gain = qg_ref[...] * (HEAD_DIM ** -0.5)
    k_gain = kg_ref[...]

    for r in range(tm // sub):
        rows = slice(r * sub, (r + 1) * sub)
        hn = _rms(h_ref[0, rows, :], nw_ref[...]).astype(BF16)
        qkv = _dot(hn, w_ref[...])
        cc = cc_ref[rows, :]
        ss = ss_ref[rows, :]
        tabs_q = (q_gain * cc, pltpu.roll(q_gain, half, 1) * ss)
        tabs_k = (k_gain * cc, pltpu.roll(k_gain, half, 1) * ss)

        def norm_rope(x, tabs):
            y = x * lax.rsqrt(jnp.mean(x * x, axis=-1, keepdims=True) + EPS)
            return y * tabs[0] + pltpu.roll(y, half, 1) * tabs[1]

        for j in range(n_q):
            sl = slice(j * HEAD_DIM, (j + 1) * HEAD_DIM)
            q_ref[0, rows, sl] = norm_rope(qkv[:, sl], tabs_q).astype(BF16)
        for j in range(n_kv):
            sl = slice(j * HEAD_DIM, (j + 1) * HEAD_DIM)
            src = slice((n_q + j) * HEAD_DIM, (n_q + j + 1) * HEAD_DIM)
            k_ref[0, rows, sl] = norm_rope(qkv[:, src], tabs_k).astype(BF16)
        v_ref[0, rows, :] = qkv[:, (n_q + n_kv) * HEAD_DIM:].astype(BF16)


def _attn_in(h, nw, w_in, q_gain, k_gain, cc, ss, tm):
    bx, sx, d = h.shape
    n_in = w_in.shape[1]
    dq = d
    dkv = (n_in - dq) // 2
    grid = (bx, sx // tm)
    tok = lambda width: pl.BlockSpec((1, tm, width), lambda b, i: (b, i, 0))
    return pl.pallas_call(
        _attn_in_kernel,
        grid=grid,
        in_specs=[
            tok(d),
            _resident((1, d)),
            _resident((d, n_in)),
            _resident((1, HEAD_DIM)),
            _resident((1, HEAD_DIM)),
            pl.BlockSpec((tm, HEAD_DIM), lambda b, i: (i, 0)),
            pl.BlockSpec((tm, HEAD_DIM), lambda b, i: (i, 0)),
        ],
        out_specs=[tok(dq), tok(dkv), tok(dkv)],
        out_shape=[
            jax.ShapeDtypeStruct((bx, sx, dq), BF16),
            jax.ShapeDtypeStruct((bx, sx, dkv), BF16),
            jax.ShapeDtypeStruct((bx, sx, dkv), BF16),
        ],
        compiler_params=_params(2),
        name="attn_in",
    )(h, nw, w_in, q_gain, k_gain, cc, ss)


def _softmax_pv(scores, values, sink_col):
    m = sink_col
    for s in scores:
        m = jnp.maximum(m, jnp.max(s, axis=-1, keepdims=True))
    den = jnp.exp(sink_col - m)
    acc = None
    for s, v in zip(scores, values):
        p = jnp.exp(s - m)
        den = den + jnp.sum(p, axis=-1, keepdims=True)
        pv = _dot(p.astype(BF16), v)
        acc = pv if acc is None else acc + pv
    return acc / den


def _attn_kernel(sink_ref, q_ref, kp_ref, kc_ref, kn_ref, vp_ref, vc_ref, vn_ref, km_ref, vm_ref,
                 o_ref, *, n_blocks):
    n = pl.program_id(1)
    blk = ATTN_BLOCK
    a = lax.broadcasted_iota(jnp.int32, (GROUP * blk, 3 * blk), 0) % blk
    c = lax.broadcasted_iota(jnp.int32, (GROUP * blk, 3 * blk), 1)
    in_prev = (c < blk) & (c >= a) & (n > 0)
    in_own = (c >= blk) & (c < 2 * blk)
    in_next = (c >= 2 * blk) & (c - 2 * blk <= a) & (n < n_blocks - 1)
    mask = in_prev | in_own | in_next
    neg_inf = jnp.float32(-jnp.inf)
    for hk in range(N_KV_HEADS):
        ks = slice(hk * HEAD_DIM, (hk + 1) * HEAD_DIM)
        keys = jnp.concatenate([kp_ref[0, :, ks], kc_ref[0, :, ks], kn_ref[0, :, ks]], axis=0)
        vals = jnp.concatenate([vp_ref[0, :, ks], vc_ref[0, :, ks], vn_ref[0, :, ks]], axis=0)
        heads = [hk * GROUP + g for g in range(GROUP)]
        q4 = jnp.concatenate(
            [q_ref[0, :, h * HEAD_DIM:(h + 1) * HEAD_DIM] for h in heads], axis=0)
        s_band = jnp.where(mask, _dot_nt(q4, keys), neg_inf)
        s_meta = _dot_nt(q4, km_ref[0, :, ks])
        sink_col = jnp.concatenate(
            [jnp.full((blk, 1), sink_ref[h], F32) for h in heads], axis=0)
        o4 = _softmax_pv([s_band, s_meta], [vals, vm_ref[0, :, ks]], sink_col)
        for g, h in enumerate(heads):
            o_ref[0, :, h * HEAD_DIM:(h + 1) * HEAD_DIM] = o4[g * blk:(g + 1) * blk].astype(BF16)


def _attn(q, k, v, km, vm, sink):
    b, s, dq = q.shape
    dkv = k.shape[-1]
    nb = s // ATTN_BLOCK
    blk = ATTN_BLOCK
    prev = pl.BlockSpec((1, blk, dkv), lambda bi, n: (bi, jnp.maximum(n - 1, 0), 0))
    own = pl.BlockSpec((1, blk, dkv), lambda bi, n: (bi, n, 0))
    nxt = pl.BlockSpec((1, blk, dkv), lambda bi, n: (bi, jnp.minimum(n + 1, nb - 1), 0))
    meta = pl.BlockSpec((1, N_META, dkv), lambda bi, n: (bi, 0, 0))
    return pl.pallas_call(
        functools.partial(_attn_kernel, n_blocks=nb),
        grid=(b, nb),
        in_specs=[
            pl.BlockSpec(memory_space=pltpu.SMEM),
            pl.BlockSpec((1, blk, dq), lambda bi, n: (bi, n, 0)),
            prev, own, nxt, prev, own, nxt, meta, meta,
        ],
        out_specs=pl.BlockSpec((1, blk, dq), lambda bi, n: (bi, n, 0)),
        out_shape=jax.ShapeDtypeStruct((b, s, dq), BF16),
        compiler_params=_params(2),
        name="band_attn",
    )(sink, q, k, k, k, v, v, v, km, vm)


def _attn_meta_kernel(sink_ref, qm_ref, km_ref, vm_ref, kr_ref, vr_ref, o_ref):
    rows = GROUP * N_META
    i = lax.broadcasted_iota(jnp.int32, (rows, ATTN_BLOCK), 0) % N_META
    j = lax.broadcasted_iota(jnp.int32, (rows, ATTN_BLOCK), 1)
    allowed = (N_META + j) - i <= WINDOW
    neg_inf = jnp.float32(-jnp.inf)
    for hk in range(N_KV_HEADS):
        ks = slice(hk * HEAD_DIM, (hk + 1) * HEAD_DIM)
        heads = [hk * GROUP + g for g in range(GROUP)]
        q4 = jnp.concatenate(
            [qm_ref[0, :, h * HEAD_DIM:(h + 1) * HEAD_DIM] for h in heads], axis=0)
        s_meta = _dot_nt(q4, km_ref[0, :, ks])
        s_real = jnp.where(allowed, _dot_nt(q4, kr_ref[0, :, ks]), neg_inf)
        sink_col = jnp.concatenate(
            [jnp.full((N_META, 1), sink_ref[h], F32) for h in heads], axis=0)
        o4 = _softmax_pv([s_meta, s_real], [vm_ref[0, :, ks], vr_ref[0, :, ks]], sink_col)
        for g, h in enumerate(heads):
            o_ref[0, :, h * HEAD_DIM:(h + 1) * HEAD_DIM] = (
                o4[g * N_META:(g + 1) * N_META].astype(BF16))


def _attn_meta(qm, km, vm, k, v, sink):
    b, _, dq = qm.shape
    dkv = km.shape[-1]
    meta = lambda width: pl.BlockSpec((1, N_META, width), lambda bi: (bi, 0, 0))
    first = pl.BlockSpec((1, ATTN_BLOCK, dkv), lambda bi: (bi, 0, 0))
    return pl.pallas_call(
        _attn_meta_kernel,
        grid=(b,),
        in_specs=[pl.BlockSpec(memory_space=pltpu.SMEM), meta(dq), meta(dkv), meta(dkv),
                  first, first],
        out_specs=meta(dq),
        out_shape=jax.ShapeDtypeStruct((b, N_META, dq), BF16),
        compiler_params=_params(1),
        name="meta_attn",
    )(sink, qm, km, vm, k, v)


def _post_kernel(*refs, rec, ff_chunk):
    if rec:
        h_ref, of_ref, ob_ref, gate_ref, og_ref, wo_ref, nw_ref, wu_ref, wd_ref, out_ref = refs
        o = of_ref[0].astype(F32) + ob_ref[0].astype(F32)
        n_heads = o.shape[-1] // HEAD_DIM
        o = jnp.concatenate(
            [_rms(o[:, j * HEAD_DIM:(j + 1) * HEAD_DIM], og_ref[...]) for j in range(n_heads)],
            axis=1)
        o = (o * gate_ref[0].astype(F32)).astype(BF16)
    else:
        h_ref, o_ref, wo_ref, nw_ref, wu_ref, wd_ref, out_ref = refs
        o = o_ref[0]
    h1 = h_ref[0] + _dot(o, wo_ref[...])
    hn = _rms(h1, nw_ref[...]).astype(BF16)
    d_ff = wu_ref.shape[1]
    acc = h1
    for c in range(d_ff // ff_chunk):
        sl = slice(c * ff_chunk, (c + 1) * ff_chunk)
        z = jnp.maximum(_dot(hn, wu_ref[:, sl]), 0.0)
        acc = acc + _dot((z * z).astype(BF16), wd_ref[sl, :])
    out_ref[0] = acc


def _post(h, mixer_out, w_out, nw, w_up, w_down, tm, rec_extra=None):
    bx, sx, d = h.shape
    d_ff = w_up.shape[1]
    tok = pl.BlockSpec((1, tm, d), lambda b, i: (b, i, 0))
    rec = rec_extra is not None
    if rec:
        ins = [h, *mixer_out, rec_extra, w_out, nw, w_up, w_down]
        specs = [tok, tok, tok, tok, _resident((1, HEAD_DIM))]
    else:
        ins = [h, mixer_out, w_out, nw, w_up, w_down]
        specs = [tok, tok]
    specs += [_resident((d, d)), _resident((1, d)), _resident((d, d_ff)), _resident((d_ff, d))]
    return pl.pallas_call(
        functools.partial(_post_kernel, rec=rec, ff_chunk=min(1024, d_ff)),
        grid=(bx, sx // tm),
        in_specs=specs,
        out_specs=tok,
        out_shape=jax.ShapeDtypeStruct((bx, sx, d), F32),
        compiler_params=_params(2),
        name="post_rec" if rec else "post_attn",
    )(*ins)


def _lower_bound(tab_ref, layer):
    t = tab_ref[...]
    e = jnp.exp(t - jnp.max(t, axis=0, keepdims=True))
    p = e / jnp.sum(e, axis=0, keepdims=True)
    return jnp.sum(p[1:layer + 1], axis=0, keepdims=True)


def _rec_in_rows(h, nw_ref, w_ref, lbf_ref, lbb_ref, layer):
    d = h.shape[-1]
    hn = _rms(h, nw_ref[...]).astype(BF16)
    proj = lambda j: _dot(hn, w_ref[:, j * d:(j + 1) * d])

    def log_forget(z, tab_ref):
        lb = _lower_bound(tab_ref, layer)
        return jnp.log(lb + (1.0 - lb) * jax.nn.sigmoid(z))

    zq = proj(0)
    q = zq * jax.nn.sigmoid(zq)
    v = proj(1)
    gf = log_forget(proj(2), lbf_ref)
    gb = log_forget(proj(3), lbb_ref)
    zg = proj(4)
    gate = zg * jax.nn.sigmoid(zg)
    return q, v, gf, gb, gate


def _rec_in_kernel(h_ref, nw_ref, w_ref, lbf_ref, lbb_ref,
                   q_ref, v_ref, gf_ref, gb_ref, gate_ref, *, layer):
    tm = h_ref.shape[1]
    sub = min(tm, SUB_TILE)
    for r in range(tm // sub):
        rows = slice(r * sub, (r + 1) * sub)
        q, v, gf, gb, gate = _rec_in_rows(h_ref[0, rows, :], nw_ref, w_ref, lbf_ref, lbb_ref,
                                          layer)
        q_ref[0, rows, :] = q.astype(BF16)
        v_ref[0, rows, :] = v.astype(BF16)
        gf_ref[0, rows, :] = gf
        gb_ref[0, rows, :] = gb
        gate_ref[0, rows, :] = gate.astype(BF16)


def _rec_in(h, nw, w_in, lb_f, lb_b, layer, tm):
    b, s, d = h.shape
    depth = lb_f.shape[0]
    tok = pl.BlockSpec((1, tm, d), lambda bi, i: (bi, i, 0))
    sp = s + REC_CHUNK
    return pl.pallas_call(
        functools.partial(_rec_in_kernel, layer=layer),
        grid=(b, s // tm),
        in_specs=[tok, _resident((1, d)), _resident((d, w_in.shape[1])),
                  _resident((depth, d)), _resident((depth, d))],
        out_specs=[tok, tok, tok, tok, tok],
        out_shape=[
            jax.ShapeDtypeStruct((b, sp, d), BF16),
            jax.ShapeDtypeStruct((b, sp, d), BF16),
            jax.ShapeDtypeStruct((b, sp, d), F32),
            jax.ShapeDtypeStruct((b, sp, d), F32),
            jax.ShapeDtypeStruct((b, s, d), BF16),
        ],
        compiler_params=_params(2),
        name="rec_in",
    )(h, nw, w_in, lb_f, lb_b)


def _rec_in_meta_kernel(hm_ref, nw_ref, w_ref, lbf_ref, lbb_ref, q_any, v_any, gf_any, gb_any,
                        q_ref, v_ref, gf_ref, gb_ref, gate_ref, *, layer):
    del q_any, v_any, gf_any, gb_any
    b, n_meta, d = hm_ref.shape
    h = hm_ref[...].reshape(b * n_meta, d)
    q, v, gf, gb, gate = _rec_in_rows(h, nw_ref, w_ref, lbf_ref, lbb_ref, layer)
    pad = REC_CHUNK - n_meta
    for ref, val in ((q_ref, q), (v_ref, v), (gf_ref, gf), (gb_ref, gb)):
        ref[:, :pad, :] = jnp.zeros((b, pad, d), ref.dtype)
        ref[:, pad:, :] = val.reshape(b, n_meta, d).astype(ref.dtype)
    gate_ref[...] = gate.reshape(b, n_meta, d).astype(BF16)


def _rec_in_meta(hm, nw, w_in, lb_f, lb_b, layer, q, v, gf, gb):
    b, n_meta, d = hm.shape
    depth = lb_f.shape[0]
    last_chunk = q.shape[1] // REC_CHUNK - 1
    chunk = pl.BlockSpec((b, REC_CHUNK, d), lambda i: (0, last_chunk, 0))
    whole = pl.BlockSpec((b, n_meta, d), lambda i: (0, 0, 0))
    any_spec = pl.BlockSpec(memory_space=pl.ANY)
    return pl.pallas_call(
        functools.partial(_rec_in_meta_kernel, layer=layer),
        grid=(1,),
        in_specs=[whole, _resident((1, d)), _resident((d, w_in.shape[1])),
                  _resident((depth, d)), _resident((depth, d)),
                  any_spec, any_spec, any_spec, any_spec],
        out_specs=[chunk, chunk, chunk, chunk, whole],
        out_shape=[
            jax.ShapeDtypeStruct(q.shape, q.dtype),
            jax.ShapeDtypeStruct(v.shape, v.dtype),
            jax.ShapeDtypeStruct(gf.shape, gf.dtype),
            jax.ShapeDtypeStruct(gb.shape, gb.dtype),
            jax.ShapeDtypeStruct((b, n_meta, d), BF16),
        ],
        input_output_aliases={5: 0, 6: 1, 7: 2, 8: 3},
        compiler_params=_params(1),
        name="rec_in_meta",
    )(hm, nw, w_in, lb_f, lb_b, q, v, gf, gb)


_SUM_LEVEL_HALVES = (32, 16, 8, 4, 2)
_N_SUM_LEVELS = len(_SUM_LEVEL_HALVES)
_MASK_ADJACENT = _N_SUM_LEVELS
_MASK_DIAG = _N_SUM_LEVELS + 1
_ROWS_Q_INTER = _N_SUM_LEVELS * REC_CHUNK
_ROWS_K_STATE = (_N_SUM_LEVELS + 1) * REC_CHUNK
_N_SUM_ROWS = (_N_SUM_LEVELS + 2) * REC_CHUNK
_LOG2E = 1.4426950408889634
SCAN_PIPELINE_LAG = 2
SUBLANE_TILE = 8


def _scan_constants(backward):
    c = REC_CHUNK
    u = np.arange(c)
    w = (u[None, :] >= u[:, None]) if backward else (u[None, :] <= u[:, None])
    masks = np.zeros((_N_SUM_LEVELS + 2, c, c), np.float32)
    for lvl, half in enumerate(_SUM_LEVEL_HALVES + (1,)):
        for t in range(c):
            start = (t // (2 * half)) * 2 * half
            if t - start >= half:
                masks[lvl, t, start:start + half] = 1.0
    masks[_MASK_DIAG] = np.eye(c, dtype=np.float32)
    if backward:
        masks = masks[:, ::-1, ::-1]
    w = np.concatenate([w, w], axis=1).astype(np.float32)
    return jnp.asarray(w, BF16), jnp.asarray(np.ascontiguousarray(masks), F32)


def _split_decays(cum, backward):
    c, d = cum.shape
    tile = SUBLANE_TILE
    sub = lax.broadcasted_iota(jnp.int32, (tile, d), 0)
    sign4 = jnp.where((sub < 4) != backward, -1.0, 1.0).astype(F32)
    sign2 = jnp.where(((sub % 4) < 2) != backward, -1.0, 1.0).astype(F32)
    row = lambda r: cum[r:r + 1]
    tiles = [cum[t:t + tile] for t in range(0, c, tile)]
    out = []
    for half in _SUM_LEVEL_HALVES:
        for i, x in enumerate(tiles):
            t0 = i * tile
            if half >= tile:
                start = (t0 // (2 * half)) * 2 * half
                r = start + half if backward else start + half - 1
                after = (t0 - start >= half) != backward
                out.append(x - row(r) if after else row(r) - x)
            elif half == 4:
                r = t0 + 4 if backward else t0 + 3
                out.append((x - row(r)) * sign4)
            else:
                r1, r2 = (t0 + 2, t0 + 6) if backward else (t0 + 1, t0 + 5)
                out.append((x - jnp.where(sub < 4, row(r1), row(r2))) * sign2)
    out.extend(tiles)
    last = 0 if backward else c - 1
    out.extend(row(last) - x for x in tiles)
    return jnp.concatenate(out, axis=0)


def _scan_kernel(qf_ref, vf_ref, gf_ref, qb_ref, vb_ref, gb_ref, wf_ref, wb_ref, mf_ref, mb_ref,
                 of_ref, ob_ref, st_ref, e_ref):
    c = REC_CHUNK
    n_heads = qf_ref.shape[-1] // HEAD_DIM

    @pl.when(pl.program_id(1) == 0)
    def _():
        st_ref[...] = jnp.zeros(st_ref.shape, F32)

    dirs = ((qf_ref, vf_ref, gf_ref, wf_ref, mf_ref, of_ref, c - 1),
            (qb_ref, vb_ref, gb_ref, wb_ref, mb_ref, ob_ref, 0))

    for d, (_, _, g_ref, w_ref, _, _, _) in enumerate(dirs):
        g2 = g_ref[0] * _LOG2E
        g_hi = g2.astype(BF16)
        g_lo = (g2 - g_hi.astype(F32)).astype(BF16)
        cum = _dot(w_ref[...], jnp.concatenate([g_hi, g_lo], axis=0))
        e_ref[d] = jnp.exp2(_split_decays(cum, backward=d == 1))

    def pair_stage(hd, d):
        q_ref, _, g_ref, _, m_ref, _, _ = dirs[d]
        sl = slice(hd * HEAD_DIM, (hd + 1) * HEAD_DIM)
        q = q_ref[0, :, sl].astype(F32)
        f = jnp.exp(g_ref[0, :, sl])
        k = 1.0 - f
        row_id = lax.broadcasted_iota(jnp.int32, (c, HEAD_DIM), 0)
        near = _dot_nt(jnp.concatenate([q, q * f], axis=0).astype(BF16), k.astype(BF16))
        pair = m_ref[_MASK_DIAG] * near[:c] + m_ref[_MASK_ADJACENT] * near[c:]
        for lvl, half in enumerate(_SUM_LEVEL_HALVES):
            e = e_ref[d, lvl * c:(lvl + 1) * c, sl]
            is_query = ((row_id // half) % 2 == 1) != (d == 1)
            both = (jnp.where(is_query, q, k) * e).astype(BF16)
            pair = pair + m_ref[lvl] * _dot_nt(both, both)
        q_in = (q * e_ref[d, _ROWS_Q_INTER:_ROWS_Q_INTER + c, sl]).astype(BF16)
        k_out = (k * e_ref[d, _ROWS_K_STATE:_ROWS_K_STATE + c, sl]).astype(BF16)
        return pair.astype(BF16), q_in, k_out

    def output_stage(hd, d, pair, q_in, k_out):
        _, v_ref, _, _, _, o_ref, whole_row = dirs[d]
        sl = slice(hd * HEAD_DIM, (hd + 1) * HEAD_DIM)
        v = v_ref[0, :, sl]
        state_t = st_ref[d, hd]
        o = _dot(pair, v) + _dot_nt(q_in, state_t.astype(BF16))
        o_ref[0, :, sl] = o.astype(BF16)
        row = _ROWS_Q_INTER + whole_row
        decay = e_ref[d, row:row + 1, sl]
        st_ref[d, hd] = state_t * decay + _dot_tn(v, k_out)

    pending = []
    for hd in range(n_heads):
        for d in range(len(dirs)):
            pending.append((hd, d) + pair_stage(hd, d))
            if len(pending) > SCAN_PIPELINE_LAG:
                output_stage(*pending.pop(0))
    for item in pending:
        output_stage(*item)


def _scan(q, v, gf, gb):
    b, sp, d = q.shape
    n_all = sp // REC_CHUNK
    n_real = n_all - 1
    n_heads = d // HEAD_DIM
    wf, mf = _scan_constants(False)
    wb, mb = _scan_constants(True)
    fwd = pl.BlockSpec((1, REC_CHUNK, d), lambda bi, j: (bi, (j + n_real) % n_all, 0))
    bwd = pl.BlockSpec((1, REC_CHUNK, d), lambda bi, j: (bi, (2 * n_real - j) % n_all, 0))
    return pl.pallas_call(
        _scan_kernel,
        grid=(b, n_all),
        in_specs=[fwd, fwd, fwd, bwd, bwd, bwd,
                  _resident(wf.shape), _resident(wb.shape),
                  _resident(mf.shape), _resident(mb.shape)],
        out_specs=[fwd, bwd],
        out_shape=[jax.ShapeDtypeStruct((b, sp, d), BF16)] * 2,
        scratch_shapes=[
            pltpu.VMEM((2, n_heads, HEAD_DIM, HEAD_DIM), F32),
            pltpu.VMEM((2, _N_SUM_ROWS, d), F32),
        ],
        compiler_params=_params(2),
        name="gla_scan",
    )(q, v, gf, q, v, gb, wf, wb, mf, mb)


def _rope_tables(length):
    inv_freq = ROPE_THETA ** (-jnp.arange(0, HEAD_DIM, 2, dtype=F32) / HEAD_DIM)
    ang = jnp.arange(length, dtype=F32)[:, None] * inv_freq[None, :]
    cos, sin = jnp.cos(ang), jnp.sin(ang)
    return jnp.concatenate([cos, cos], axis=-1), jnp.concatenate([-sin, sin], axis=-1)


def kernel(x, meta_tokens, mix_norm_w, mlp_norm_w, attn_w_in, attn_w_out, attn_q_norm_w,
           attn_k_norm_w, attn_sink, rec_w_in, rec_w_out, rec_lb_fwd, rec_lb_bwd,
           rec_out_norm_w, mlp_w_up, mlp_w_down):
    b, s, d = x.shape
    depth = mix_norm_w.shape[0]
    n_mixers = 2
    tm = min(TOKEN_TILE, s)
    n_meta_rows = b * N_META

    h = x
    hm = jnp.broadcast_to(meta_tokens.astype(x.dtype)[None], (b, N_META, d))
    cc, ss = _rope_tables(N_META + s)
    cc_meta, ss_meta = jnp.tile(cc[:N_META], (b, 1)), jnp.tile(ss[:N_META], (b, 1))
    cc_real, ss_real = cc[N_META:], ss[N_META:]
    row = lambda w: w.reshape(1, -1)
    flat_meta = lambda t: t.reshape(1, n_meta_rows, t.shape[-1])
    per_batch = lambda t: t.reshape(b, N_META, t.shape[-1])

    for layer in range(depth):
        j = layer // n_mixers
        nw = row(mix_norm_w[layer])
        mlp = (row(mlp_norm_w[layer]), mlp_w_up[layer].astype(BF16),
               mlp_w_down[layer].astype(BF16))
        if layer % n_mixers == 0:
            w_in = attn_w_in[j].astype(BF16)
            w_out = attn_w_out[j].astype(BF16)
            gains = (row(attn_q_norm_w[j]), row(attn_k_norm_w[j]))
            q, k, v = _attn_in(h, nw, w_in, *gains, cc_real, ss_real, tm)
            qm, km, vm = map(per_batch, _attn_in(
                flat_meta(hm), nw, w_in, *gains, cc_meta, ss_meta, n_meta_rows))
            sink = attn_sink[j].astype(F32)
            o = _attn(q, k, v, km, vm, sink)
            om = _attn_meta(qm, km, vm, k, v, sink)
            h = _post(h, o, w_out, *mlp, tm)
            hm = per_batch(_post(flat_meta(hm), flat_meta(om), w_out, *mlp, n_meta_rows))
        else:
            w_in = rec_w_in[j].astype(BF16)
            w_out = rec_w_out[j].astype(BF16)
            gain = row(rec_out_norm_w[j])
            q, v, gf, gb, gate = _rec_in(h, nw, w_in, rec_lb_fwd, rec_lb_bwd, layer, tm)
            q, v, gf, gb, gate_m = _rec_in_meta(hm, nw, w_in, rec_lb_fwd, rec_lb_bwd, layer,
                                                q, v, gf, gb)
            o_f, o_b = _scan(q, v, gf, gb)
            h = _post(h, (o_f, o_b, gate), w_out, *mlp, tm, rec_extra=gain)
            meta_rows = lambda t: flat_meta(t[:, s + REC_CHUNK - N_META:])
            hm = per_batch(_post(flat_meta(hm), (meta_rows(o_f), meta_rows(o_b),
                                                 flat_meta(gate_m)),
                                 w_out, *mlp, n_meta_rows, rec_extra=gain))
    return h
```

```python
import functools

import numpy as np
import jax
import jax.numpy as jnp
from jax import lax
from jax.experimental import pallas as pl
from jax.experimental.pallas import tpu as pltpu

N_META = 16
HEAD_DIM = 128
N_KV_HEADS = 2
GROUP = 4
WINDOW = 128
ATTN_BLOCK = 128
ROPE_THETA = 10000.0
REC_CHUNK = 64
EPS = 1e-6

F32 = jnp.float32
BF16 = jnp.bfloat16

V7X_VMEM_LIMIT_BYTES = 56 * 1024 * 1024
TOKEN_TILE = 512
SUB_TILE = 256


def _params(n_axes):
    return pltpu.CompilerParams(
        dimension_semantics=("arbitrary",) * n_axes,
        vmem_limit_bytes=V7X_VMEM_LIMIT_BYTES,
    )


def _resident(shape):
    zeros = (0,) * len(shape)
    return pl.BlockSpec(shape, lambda *_: zeros, pipeline_mode=pl.Buffered(1))


def _rms(x, w):
    ms = jnp.mean(x * x, axis=-1, keepdims=True)
    return x * lax.rsqrt(ms + EPS) * w


def _dot(a, b):
    return jnp.dot(a, b, preferred_element_type=F32)


def _dot_nt(a, b):
    return lax.dot_general(a, b, (((1,), (1,)), ((), ())), preferred_element_type=F32)


def _dot_tn(a, b):
    return lax.dot_general(a, b, (((0,), (0,)), ((), ())), preferred_element_type=F32)


def _attn_in_kernel(h_ref, nw_ref, w_ref, qg_ref, kg_ref, cc_ref, ss_ref, q_ref, k_ref, v_ref):
    n_q = q_ref.shape[-1] // HEAD_DIM
    n_kv = k_ref.shape[-1] // HEAD_DIM
    tm = h_ref.shape[1]
    sub = min(tm, SUB_TILE)
    half = HEAD_DIM // 2
    q_gain = qg_ref[...] * (HEAD_DIM ** -0.5)
    k_gain = kg_ref[...]

    for r in range(tm // sub):
        rows = slice(r * sub, (r + 1) * sub)
        hn = _rms(h_ref[0, rows, :], nw_ref[...]).astype(BF16)
        qkv = _dot(hn, w_ref[...])
        cc = cc_ref[rows, :]
        ss = ss_ref[rows, :]
        tabs_q = (q_gain * cc, pltpu.roll(q_gain, half, 1) * ss)
        tabs_k = (k_gain * cc, pltpu.roll(k_gain, half, 1) * ss)

        def norm_rope(x, tabs):
            y = x * lax.rsqrt(jnp.mean(x * x, axis=-1, keepdims=True) + EPS)
            return y * tabs[0] + pltpu.roll(y, half, 1) * tabs[1]

        for j in range(n_q):
            sl = slice(j * HEAD_DIM, (j + 1) * HEAD_DIM)
            q_ref[0, rows, sl] = norm_rope(qkv[:, sl], tabs_q).astype(BF16)
        for j in range(n_kv):
            sl = slice(j * HEAD_DIM, (j + 1) * HEAD_DIM)
            src = slice((n_q + j) * HEAD_DIM, (n_q + j + 1) * HEAD_DIM)
            k_ref[0, rows, sl] = norm_rope(qkv[:, src], tabs_k).astype(BF16)
        v_ref[0, rows, :] = qkv[:, (n_q + n_kv) * HEAD_DIM:].astype(BF16)


def _attn_in(h, nw, w_in, q_gain, k_gain, cc, ss, tm):
    bx, sx, d = h.shape
    n_in = w_in.shape[1]
    dq = d
    dkv = (n_in - dq) // 2
    grid = (bx, sx // tm)
    tok = lambda width: pl.BlockSpec((1, tm, width), lambda b, i: (b, i, 0))
    return pl.pallas_call(
        _attn_in_kernel,
        grid=grid,
        in_specs=[
            tok(d),
            _resident((1, d)),
            _resident((d, n_in)),
            _resident((1, HEAD_DIM)),
            _resident((1, HEAD_DIM)),
            pl.BlockSpec((tm, HEAD_DIM), lambda b, i: (i, 0)),
            pl.BlockSpec((tm, HEAD_DIM), lambda b, i: (i, 0)),
        ],
        out_specs=[tok(dq), tok(dkv), tok(dkv)],
        out_shape=[
            jax.ShapeDtypeStruct((bx, sx, dq), BF16),
            jax.ShapeDtypeStruct((bx, sx, dkv), BF16),
            jax.ShapeDtypeStruct((bx, sx, dkv), BF16),
        ],
        compiler_params=_params(2),
        name="attn_in",
    )(h, nw, w_in, q_gain, k_gain, cc, ss)


def _softmax_pv(scores, values, sink_col):
    m = sink_col
    for s in scores:
        m = jnp.maximum(m, jnp.max(s, axis=-1, keepdims=True))
    den = jnp.exp(sink_col - m)
    acc = None
    for s, v in zip(scores, values):
        p = jnp.exp(s - m)
        den = den + jnp.sum(p, axis=-1, keepdims=True)
        pv = _dot(p.astype(BF16), v)
        acc = pv if acc is None else acc + pv
    return acc / den


def _attn_kernel(sink_ref, q_ref, kp_ref, kc_ref, kn_ref, vp_ref, vc_ref, vn_ref, km_ref, vm_ref,
                 o_ref, *, n_blocks):
    n = pl.program_id(1)
    blk = ATTN_BLOCK
    a = lax.broadcasted_iota(jnp.int32, (GROUP * blk, 3 * blk), 0) % blk
    c = lax.broadcasted_iota(jnp.int32, (GROUP * blk, 3 * blk), 1)
    in_prev = (c < blk) & (c >= a) & (n > 0)
    in_own = (c >= blk) & (c < 2 * blk)
    in_next = (c >= 2 * blk) & (c - 2 * blk <= a) & (n < n_blocks - 1)
    mask = in_prev | in_own | in_next
    neg_inf = jnp.float32(-jnp.inf)
    for hk in range(N_KV_HEADS):
        ks = slice(hk * HEAD_DIM, (hk + 1) * HEAD_DIM)
        keys = jnp.concatenate([kp_ref[0, :, ks], kc_ref[0, :, ks], kn_ref[0, :, ks]], axis=0)
        vals = jnp.concatenate([vp_ref[0, :, ks], vc_ref[0, :, ks], vn_ref[0, :, ks]], axis=0)
        heads = [hk * GROUP + g for g in range(GROUP)]
        q4 = jnp.concatenate(
            [q_ref[0, :, h * HEAD_DIM:(h + 1) * HEAD_DIM] for h in heads], axis=0)
        s_band = jnp.where(mask, _dot_nt(q4, keys), neg_inf)
        s_meta = _dot_nt(q4, km_ref[0, :, ks])
        sink_col = jnp.concatenate(
            [jnp.full((blk, 1), sink_ref[h], F32) for h in heads], axis=0)
        o4 = _softmax_pv([s_band, s_meta], [vals, vm_ref[0, :, ks]], sink_col)
        for g, h in enumerate(heads):
            o_ref[0, :, h * HEAD_DIM:(h + 1) * HEAD_DIM] = o4[g * blk:(g + 1) * blk].astype(BF16)


def _attn(q, k, v, km, vm, sink):
    b, s, dq = q.shape
    dkv = k.shape[-1]
    nb = s // ATTN_BLOCK
    blk = ATTN_BLOCK
    prev = pl.BlockSpec((1, blk, dkv), lambda bi, n: (bi, jnp.maximum(n - 1, 0), 0))
    own = pl.BlockSpec((1, blk, dkv), lambda bi, n: (bi, n, 0))
    nxt = pl.BlockSpec((1, blk, dkv), lambda bi, n: (bi, jnp.minimum(n + 1, nb - 1), 0))
    meta = pl.BlockSpec((1, N_META, dkv), lambda bi, n: (bi, 0, 0))
    return pl.pallas_call(
        functools.partial(_attn_kernel, n_blocks=nb),
        grid=(b, nb),
        in_specs=[
            pl.BlockSpec(memory_space=pltpu.SMEM),
            pl.BlockSpec((1, blk, dq), lambda bi, n: (bi, n, 0)),
            prev, own, nxt, prev, own, nxt, meta, meta,
        ],
        out_specs=pl.BlockSpec((1, blk, dq), lambda bi, n: (bi, n, 0)),
        out_shape=jax.ShapeDtypeStruct((b, s, dq), BF16),
        compiler_params=_params(2),
        name="band_attn",
    )(sink, q, k, k, k, v, v, v, km, vm)


def _attn_meta_kernel(sink_ref, qm_ref, km_ref, vm_ref, kr_ref, vr_ref, o_ref):
    rows = GROUP * N_META
    i = lax.broadcasted_iota(jnp.int32, (rows, ATTN_BLOCK), 0) % N_META
    j = lax.broadcasted_iota(jnp.int32, (rows, ATTN_BLOCK), 1)
    allowed = (N_META + j) - i <= WINDOW
    neg_inf = jnp.float32(-jnp.inf)
    for hk in range(N_KV_HEADS):
        ks = slice(hk * HEAD_DIM, (hk + 1) * HEAD_DIM)
        heads = [hk * GROUP + g for g in range(GROUP)]
        q4 = jnp.concatenate(
            [qm_ref[0, :, h * HEAD_DIM:(h + 1) * HEAD_DIM] for h in heads], axis=0)
        s_meta = _dot_nt(q4, km_ref[0, :, ks])
        s_real = jnp.where(allowed, _dot_nt(q4, kr_ref[0, :, ks]), neg_inf)
        sink_col = jnp.concatenate(
            [jnp.full((N_META, 1), sink_ref[h], F32) for h in heads], axis=0)
        o4 = _softmax_pv([s_meta, s_real], [vm_ref[0, :, ks], vr_ref[0, :, ks]], sink_col)
        for g, h in enumerate(heads):
            o_ref[0, :, h * HEAD_DIM:(h + 1) * HEAD_DIM] = (
                o4[g * N_META:(g + 1) * N_META].astype(BF16))


def _attn_meta(qm, km, vm, k, v, sink):
    b, _, dq = qm.shape
    dkv = km.shape[-1]
    meta = lambda width: pl.BlockSpec((1, N_META, width), lambda bi: (bi, 0, 0))
    first = pl.BlockSpec((1, ATTN_BLOCK, dkv), lambda bi: (bi, 0, 0))
    return pl.pallas_call(
        _attn_meta_kernel,
        grid=(b,),
        in_specs=[pl.BlockSpec(memory_space=pltpu.SMEM), meta(dq), meta(dkv), meta(dkv),
                  first, first],
        out_specs=meta(dq),
        out_shape=jax.ShapeDtypeStruct((b, N_META, dq), BF16),
        compiler_params=_params(1),
        name="meta_attn",
    )(sink, qm, km, vm, k, v)


def _post_kernel(*refs, rec, ff_chunk):
    if rec:
        h_ref, of_ref, ob_ref, gate_ref, og_ref, wo_ref, nw_ref, wu_ref, wd_ref, out_ref = refs
        o = of_ref[0].astype(F32) + ob_ref[0].astype(F32)
        n_heads = o.shape[-1] // HEAD_DIM
        o = jnp.concatenate(
            [_rms(o[:, j * HEAD_DIM:(j + 1) * HEAD_DIM], og_ref[...]) for j in range(n_heads)],
            axis=1)
        o = (o * gate_ref[0].astype(F32)).astype(BF16)
    else:
        h_ref, o_ref, wo_ref, nw_ref, wu_ref, wd_ref, out_ref = refs
        o = o_ref[0]
    h1 = h_ref[0] + _dot(o, wo_ref[...])
    hn = _rms(h1, nw_ref[...]).astype(BF16)
    d_ff = wu_ref.shape[1]
    acc = h1
    for c in range(d_ff // ff_chunk):
        sl = slice(c * ff_chunk, (c + 1) * ff_chunk)
        z = jnp.maximum(_dot(hn, wu_ref[:, sl]), 0.0)
        acc = acc + _dot((z * z).astype(BF16), wd_ref[sl, :])
    out_ref[0] = acc


def _post(h, mixer_out, w_out, nw, w_up, w_down, tm, rec_extra=None):
    bx, sx, d = h.shape
    d_ff = w_up.shape[1]
    tok = pl.BlockSpec((1, tm, d), lambda b, i: (b, i, 0))
    rec = rec_extra is not None
    if rec:
        ins = [h, *mixer_out, rec_extra, w_out, nw, w_up, w_down]
        specs = [tok, tok, tok, tok, _resident((1, HEAD_DIM))]
    else:
        ins = [h, mixer_out, w_out, nw, w_up, w_down]
        specs = [tok, tok]
    specs += [_resident((d, d)), _resident((1, d)), _resident((d, d_ff)), _resident((d_ff, d))]
    return pl.pallas_call(
        functools.partial(_post_kernel, rec=rec, ff_chunk=min(1024, d_ff)),
        grid=(bx, sx // tm),
        in_specs=specs,
        out_specs=tok,
        out_shape=jax.ShapeDtypeStruct((bx, sx, d), F32),
        compiler_params=_params(2),
        name="post_rec" if rec else "post_attn",
    )(*ins)


def _lower_bound(tab_ref, layer):
    t = tab_ref[...]
    e = jnp.exp(t - jnp.max(t, axis=0, keepdims=True))
    p = e / jnp.sum(e, axis=0, keepdims=True)
    return jnp.sum(p[1:layer + 1], axis=0, keepdims=True)


def _rec_in_rows(h, nw_ref, w_ref, lbf_ref, lbb_ref, layer):
    d = h.shape[-1]
    hn = _rms(h, nw_ref[...]).astype(BF16)
    proj = lambda j: _dot(hn, w_ref[:, j * d:(j + 1) * d])

    def log_forget(z, tab_ref):
        lb = _lower_bound(tab_ref, layer)
        return jnp.log2(lb + (1.0 - lb) * jax.nn.sigmoid(z))

    zq = proj(0)
    q = zq * jax.nn.sigmoid(zq)
    v = proj(1)
    gf = log_forget(proj(2), lbf_ref)
    gb = log_forget(proj(3), lbb_ref)
    zg = proj(4)
    gate = zg * jax.nn.sigmoid(zg)
    return q, v, gf, gb, gate


def _rec_in_kernel(h_ref, nw_ref, w_ref, lbf_ref, lbb_ref,
                   q_ref, v_ref, gf_ref, gb_ref, gate_ref, *, layer):
    tm = h_ref.shape[1]
    sub = min(tm, SUB_TILE)
    for r in range(tm // sub):
        rows = slice(r * sub, (r + 1) * sub)
        q, v, gf, gb, gate = _rec_in_rows(h_ref[0, rows, :], nw_ref, w_ref, lbf_ref, lbb_ref,
                                          layer)
        q_ref[0, rows, :] = q.astype(BF16)
        v_ref[0, rows, :] = v.astype(BF16)
        gf_ref[0, rows, :] = gf
        gb_ref[0, rows, :] = gb
        gate_ref[0, rows, :] = gate.astype(BF16)


def _rec_in(h, nw, w_in, lb_f, lb_b, layer, tm):
    b, s, d = h.shape
    depth = lb_f.shape[0]
    tok = pl.BlockSpec((1, tm, d), lambda bi, i: (bi, i, 0))
    return pl.pallas_call(
        functools.partial(_rec_in_kernel, layer=layer),
        grid=(b, s // tm),
        in_specs=[tok, _resident((1, d)), _resident((d, w_in.shape[1])),
                  _resident((depth, d)), _resident((depth, d))],
        out_specs=[tok, tok, tok, tok, tok],
        out_shape=[
            jax.ShapeDtypeStruct((b, s, d), BF16),
            jax.ShapeDtypeStruct((b, s, d), BF16),
            jax.ShapeDtypeStruct((b, s, d), F32),
            jax.ShapeDtypeStruct((b, s, d), F32),
            jax.ShapeDtypeStruct((b, s, d), BF16),
        ],
        compiler_params=_params(2),
        name="rec_in",
    )(h, nw, w_in, lb_f, lb_b)


def _rec_in_meta_kernel(hm_ref, nw_ref, w_ref, lbf_ref, lbb_ref,
                        q_ref, v_ref, gf_ref, gb_ref, gate_ref, *, layer):
    b, n_meta, d = hm_ref.shape
    h = hm_ref[...].reshape(b * n_meta, d)
    q, v, gf, gb, gate = _rec_in_rows(h, nw_ref, w_ref, lbf_ref, lbb_ref, layer)
    pad = REC_CHUNK - n_meta
    for ref, val in ((q_ref, q), (v_ref, v), (gf_ref, gf), (gb_ref, gb)):
        ref[:, :pad, :] = jnp.zeros((b, pad, d), ref.dtype)
        ref[:, pad:, :] = val.reshape(b, n_meta, d).astype(ref.dtype)
    gate_ref[...] = gate.reshape(b, n_meta, d).astype(BF16)


def _rec_in_meta(hm, nw, w_in, lb_f, lb_b, layer):
    b, n_meta, d = hm.shape
    depth = lb_f.shape[0]
    chunk = pl.BlockSpec((b, REC_CHUNK, d), lambda i: (0, 0, 0))
    whole = pl.BlockSpec((b, n_meta, d), lambda i: (0, 0, 0))
    return pl.pallas_call(
        functools.partial(_rec_in_meta_kernel, layer=layer),
        grid=(1,),
        in_specs=[whole, _resident((1, d)), _resident((d, w_in.shape[1])),
                  _resident((depth, d)), _resident((depth, d))],
        out_specs=[chunk, chunk, chunk, chunk, whole],
        out_shape=[
            jax.ShapeDtypeStruct((b, REC_CHUNK, d), BF16),
            jax.ShapeDtypeStruct((b, REC_CHUNK, d), BF16),
            jax.ShapeDtypeStruct((b, REC_CHUNK, d), F32),
            jax.ShapeDtypeStruct((b, REC_CHUNK, d), F32),
            jax.ShapeDtypeStruct((b, n_meta, d), BF16),
        ],
        compiler_params=_params(1),
        name="rec_in_meta",
    )(hm, nw, w_in, lb_f, lb_b)


_SUM_LEVEL_HALVES = (32, 16, 8, 4, 2)
_N_SUM_LEVELS = len(_SUM_LEVEL_HALVES)
_MASK_ADJACENT = _N_SUM_LEVELS
_MASK_DIAG = _N_SUM_LEVELS + 1
_ROWS_Q_INTER = _N_SUM_LEVELS * REC_CHUNK
_ROWS_K_STATE = (_N_SUM_LEVELS + 1) * REC_CHUNK
_N_SUM_ROWS = (_N_SUM_LEVELS + 2) * REC_CHUNK
SCAN_CHUNKS_PER_STEP = 4
SCAN_PIPELINE_LAG = 8
SUBLANE_TILE = 8


def _scan_constants(backward):
    c = REC_CHUNK
    u = np.arange(c)
    w = (u[None, :] >= u[:, None]) if backward else (u[None, :] <= u[:, None])
    masks = np.zeros((_N_SUM_LEVELS + 2, c, c), np.float32)
    for lvl, half in enumerate(_SUM_LEVEL_HALVES + (1,)):
        for t in range(c):
            start = (t // (2 * half)) * 2 * half
            if t - start >= half:
                masks[lvl, t, start:start + half] = 1.0
    masks[_MASK_DIAG] = np.eye(c, dtype=np.float32)
    if backward:
        masks = masks[:, ::-1, ::-1]
    w = np.concatenate([w, w], axis=1).astype(np.float32)
    return jnp.asarray(w, BF16), jnp.asarray(np.ascontiguousarray(masks), F32)


def _split_decays(cum, backward):
    c, d = cum.shape
    tile = SUBLANE_TILE
    sub = lax.broadcasted_iota(jnp.int32, (tile, d), 0)
    sign4 = jnp.where((sub < 4) != backward, -1.0, 1.0).astype(F32)
    sign2 = jnp.where(((sub % 4) < 2) != backward, -1.0, 1.0).astype(F32)
    row = lambda r: cum[r:r + 1]
    tiles = [cum[t:t + tile] for t in range(0, c, tile)]
    out = []
    for half in _SUM_LEVEL_HALVES:
        for i, x in enumerate(tiles):
            t0 = i * tile
            if half >= tile:
                start = (t0 // (2 * half)) * 2 * half
                r = start + half if backward else start + half - 1
                after = (t0 - start >= half) != backward
                out.append(x - row(r) if after else row(r) - x)
            elif half == 4:
                r = t0 + 4 if backward else t0 + 3
                out.append((x - row(r)) * sign4)
            else:
                r1, r2 = (t0 + 2, t0 + 6) if backward else (t0 + 1, t0 + 5)
                out.append((x - jnp.where(sub < 4, row(r1), row(r2))) * sign2)
    out.extend(tiles)
    last = 0 if backward else c - 1
    out.extend(row(last) - x for x in tiles)
    return jnp.concatenate(out, axis=0)


class _ChunkJob:
    def __init__(self, d, q_ref, v_ref, g_ref, o_ref, rows, slot):
        self.d, self.q_ref, self.v_ref, self.g_ref, self.o_ref = d, q_ref, v_ref, g_ref, o_ref
        self.rows, self.slot = rows, slot


def _scan_kernel(qf_ref, vf_ref, gf_ref, qb_ref, vb_ref, gb_ref, qm_ref, vm_ref, gfm_ref, gbm_ref,
                 wf_ref, wb_ref, mf_ref, mb_ref, of_ref, ob_ref, ofm_ref, obm_ref,
                 st_ref, e_ref):
    c = REC_CHUNK
    n_heads = qf_ref.shape[-1] // HEAD_DIM
    n_chunks = qf_ref.shape[1] // c
    step = pl.program_id(1)
    w_refs = (wf_ref, wb_ref)
    m_refs = (mf_ref, mb_ref)
    whole_rows = (c - 1, 0)

    @pl.when(step == 0)
    def _():
        st_ref[...] = jnp.zeros(st_ref.shape, F32)

    def decay_stage(job):
        g2 = job.g_ref[0, job.rows, :]
        g_hi = g2.astype(BF16)
        g_lo = (g2 - g_hi.astype(F32)).astype(BF16)
        cum = _dot(w_refs[job.d][...], jnp.concatenate([g_hi, g_lo], axis=0))
        e_ref[job.slot, job.d] = jnp.exp2(_split_decays(cum, backward=job.d == 1))

    def pair_stage(job, hd):
        d, m_ref = job.d, m_refs[job.d]
        sl = slice(hd * HEAD_DIM, (hd + 1) * HEAD_DIM)
        e = lambda r0: e_ref[job.slot, d, r0:r0 + c, sl]
        q = job.q_ref[0, job.rows, sl].astype(F32)
        f = jnp.exp2(job.g_ref[0, job.rows, sl])
        k = 1.0 - f
        row_id = lax.broadcasted_iota(jnp.int32, (c, HEAD_DIM), 0)
        near = _dot_nt(jnp.concatenate([q, q * f], axis=0).astype(BF16), k.astype(BF16))
        pair = m_ref[_MASK_DIAG] * near[:c] + m_ref[_MASK_ADJACENT] * near[c:]
        for lvl, half in enumerate(_SUM_LEVEL_HALVES):
            is_query = ((row_id // half) % 2 == 1) != (d == 1)
            both = (jnp.where(is_query, q, k) * e(lvl * c)).astype(BF16)
            pair = pair + m_ref[lvl] * _dot_nt(both, both)
        q_in = (q * e(_ROWS_Q_INTER)).astype(BF16)
        k_out = (k * e(_ROWS_K_STATE)).astype(BF16)
        return pair.astype(BF16), q_in, k_out

    def output_stage(job, hd, pair, q_in, k_out):
        d = job.d
        sl = slice(hd * HEAD_DIM, (hd + 1) * HEAD_DIM)
        v = job.v_ref[0, job.rows, sl]
        state = st_ref[d, hd]
        o = _dot(pair, v) + _dot(q_in, state.astype(BF16))
        job.o_ref[0, job.rows, sl] = o.astype(BF16)
        row = _ROWS_Q_INTER + whole_rows[d]
        t0 = (row // SUBLANE_TILE) * SUBLANE_TILE
        decay_col = e_ref[job.slot, d, t0:t0 + SUBLANE_TILE, sl].T[:, row - t0:row - t0 + 1]
        st_ref[d, hd] = state * decay_col + _dot_tn(k_out, v)

    def run(jobs):
        pending = []
        for group in jobs:
            for job in group:
                decay_stage(job)
            for hd in range(n_heads):
                for job in group:
                    pending.append((job, hd) + pair_stage(job, hd))
                    if len(pending) > SCAN_PIPELINE_LAG:
                        output_stage(*pending.pop(0))
        for item in pending:
            output_stage(*item)

    whole = slice(0, c)

    @pl.when(step == 0)
    def _():
        run([[_ChunkJob(0, qm_ref, vm_ref, gfm_ref, ofm_ref, whole, 0)]])

    groups = []
    for i in range(n_chunks):
        fwd_rows = slice(i * c, (i + 1) * c)
        bwd_rows = slice((n_chunks - 1 - i) * c, (n_chunks - i) * c)
        groups.append([_ChunkJob(0, qf_ref, vf_ref, gf_ref, of_ref, fwd_rows, i),
                       _ChunkJob(1, qb_ref, vb_ref, gb_ref, ob_ref, bwd_rows, i)])
    run(groups)

    @pl.when(step == pl.num_programs(1) - 1)
    def _():
        run([[_ChunkJob(1, qm_ref, vm_ref, gbm_ref, obm_ref, whole, 0)]])


def _scan(q, v, gf, gb, qm, vm, gfm, gbm):
    b, s, d = q.shape
    n_heads = d // HEAD_DIM
    chunks_per_step = min(SCAN_CHUNKS_PER_STEP, s // REC_CHUNK)
    rows = chunks_per_step * REC_CHUNK
    n_steps = s // rows
    wf, mf = _scan_constants(False)
    wb, mb = _scan_constants(True)
    fwd = pl.BlockSpec((1, rows, d), lambda bi, j: (bi, j, 0))
    bwd = pl.BlockSpec((1, rows, d), lambda bi, j: (bi, n_steps - 1 - j, 0))
    meta = pl.BlockSpec((1, REC_CHUNK, d), lambda bi, j: (bi, 0, 0))
    return pl.pallas_call(
        _scan_kernel,
        grid=(b, n_steps),
        in_specs=[fwd, fwd, fwd, bwd, bwd, bwd, meta, meta, meta, meta,
                  _resident(wf.shape), _resident(wb.shape),
                  _resident(mf.shape), _resident(mb.shape)],
        out_specs=[fwd, bwd, meta, meta],
        out_shape=[jax.ShapeDtypeStruct((b, s, d), BF16)] * 2
        + [jax.ShapeDtypeStruct((b, REC_CHUNK, d), BF16)] * 2,
        scratch_shapes=[
            pltpu.VMEM((2, n_heads, HEAD_DIM, HEAD_DIM), F32),
            pltpu.VMEM((chunks_per_step, 2, _N_SUM_ROWS, d), F32),
        ],
        compiler_params=_params(2),
        name="gla_scan",
    )(q, v, gf, q, v, gb, qm, vm, gfm, gbm, wf, wb, mf, mb)


def _rope_tables(length):
    inv_freq = ROPE_THETA ** (-jnp.arange(0, HEAD_DIM, 2, dtype=F32) / HEAD_DIM)
    ang = jnp.arange(length, dtype=F32)[:, None] * inv_freq[None, :]
    cos, sin = jnp.cos(ang), jnp.sin(ang)
    return jnp.concatenate([cos, cos], axis=-1), jnp.concatenate([-sin, sin], axis=-1)


def kernel(x, meta_tokens, mix_norm_w, mlp_norm_w, attn_w_in, attn_w_out, attn_q_norm_w,
           attn_k_norm_w, attn_sink, rec_w_in, rec_w_out, rec_lb_fwd, rec_lb_bwd,
           rec_out_norm_w, mlp_w_up, mlp_w_down):
    b, s, d = x.shape
    depth = mix_norm_w.shape[0]
    n_mixers = 2
    tm = min(TOKEN_TILE, s)
    n_meta_rows = b * N_META

    h = x
    hm = jnp.broadcast_to(meta_tokens.astype(x.dtype)[None], (b, N_META, d))
    cc, ss = _rope_tables(N_META + s)
    cc_meta, ss_meta = jnp.tile(cc[:N_META], (b, 1)), jnp.tile(ss[:N_META], (b, 1))
    cc_real, ss_real = cc[N_META:], ss[N_META:]
    row = lambda w: w.reshape(1, -1)
    flat_meta = lambda t: t.reshape(1, n_meta_rows, t.shape[-1])
    per_batch = lambda t: t.reshape(b, N_META, t.shape[-1])

    for layer in range(depth):
        j = layer // n_mixers
        nw = row(mix_norm_w[layer])
        mlp = (row(mlp_norm_w[layer]), mlp_w_up[layer].astype(BF16),
               mlp_w_down[layer].astype(BF16))
        if layer % n_mixers == 0:
            w_in = attn_w_in[j].astype(BF16)
            w_out = attn_w_out[j].astype(BF16)
            gains = (row(attn_q_norm_w[j]), row(attn_k_norm_w[j]))
            q, k, v = _attn_in(h, nw, w_in, *gains, cc_real, ss_real, tm)
            qm, km, vm = map(per_batch, _attn_in(
                flat_meta(hm), nw, w_in, *gains, cc_meta, ss_meta, n_meta_rows))
            sink = attn_sink[j].astype(F32)
            o = _attn(q, k, v, km, vm, sink)
            om = _attn_meta(qm, km, vm, k, v, sink)
            h = _post(h, o, w_out, *mlp, tm)
            hm = per_batch(_post(flat_meta(hm), flat_meta(om), w_out, *mlp, n_meta_rows))
        else:
            w_in = rec_w_in[j].astype(BF16)
            w_out = rec_w_out[j].astype(BF16)
            gain = row(rec_out_norm_w[j])
            q, v, gf, gb, gate = _rec_in(h, nw, w_in, rec_lb_fwd, rec_lb_bwd, layer, tm)
            qm, vm, gfm, gbm, gate_m = _rec_in_meta(hm, nw, w_in, rec_lb_fwd, rec_lb_bwd, layer)
            o_f, o_b, o_fm, o_bm = _scan(q, v, gf, gb, qm, vm, gfm, gbm)
            h = _post(h, (o_f, o_b, gate), w_out, *mlp, tm, rec_extra=gain)
            meta_rows = lambda t: flat_meta(t[:, REC_CHUNK - N_META:])
            hm = per_batch(_post(flat_meta(hm), (meta_rows(o_fm), meta_rows(o_bm),
                                                 flat_meta(gate_m)),
                                 w_out, *mlp, n_meta_rows, rec_extra=gain))
    return h
```

```python
import functools

import numpy as np
import jax
import jax.numpy as jnp
from jax import lax
from jax.experimental import pallas as pl
from jax.experimental.pallas import tpu as pltpu

N_META = 16
HEAD_DIM = 128
N_KV_HEADS = 2
GROUP = 4
WINDOW = 128
ATTN_BLOCK = 128
ROPE_THETA = 10000.0
REC_CHUNK = 64
EPS = 1e-6

F32 = jnp.float32
BF16 = jnp.bfloat16

V7X_VMEM_LIMIT_BYTES = 56 * 1024 * 1024
TOKEN_TILE = 512
SUB_TILE = 256
ATTN_BLOCKS_PER_STEP = 4


def _params(n_axes):
    return pltpu.CompilerParams(
        dimension_semantics=("arbitrary",) * n_axes,
        vmem_limit_bytes=V7X_VMEM_LIMIT_BYTES,
    )


def _resident(shape):
    zeros = (0,) * len(shape)
    return pl.BlockSpec(shape, lambda *_: zeros, pipeline_mode=pl.Buffered(1))


def _rms(x, w):
    ms = jnp.mean(x * x, axis=-1, keepdims=True)
    return x * lax.rsqrt(ms + EPS) * w


def _dot(a, b):
    return jnp.dot(a, b, preferred_element_type=F32)


def _dot_nt(a, b):
    return lax.dot_general(a, b, (((1,), (1,)), ((), ())), preferred_element_type=F32)


def _dot_tn(a, b):
    return lax.dot_general(a, b, (((0,), (0,)), ((), ())), preferred_element_type=F32)


def _attn_in_kernel(h_ref, nw_ref, w_ref, qg_ref, kg_ref, cc_ref, ss_ref, q_ref, kv_ref):
    n_q = q_ref.shape[-1] // HEAD_DIM
    n_kv = kv_ref.shape[-1] // (2 * HEAD_DIM)
    tm = h_ref.shape[1]
    sub = min(tm, SUB_TILE)
    half = HEAD_DIM // 2
    q_gain = qg_ref[...] * (HEAD_DIM ** -0.5)
    k_gain = kg_ref[...]

    for r in range(tm // sub):
        rows = slice(r * sub, (r + 1) * sub)
        hn = _rms(h_ref[0, rows, :], nw_ref[...]).astype(BF16)
        qkv = _dot(hn, w_ref[...])
        cc = cc_ref[rows, :]
        ss = ss_ref[rows, :]
        tabs_q = (q_gain * cc, pltpu.roll(q_gain, half, 1) * ss)
        tabs_k = (k_gain * cc, pltpu.roll(k_gain, half, 1) * ss)

        def norm_rope(x, tabs):
            y = x * lax.rsqrt(jnp.mean(x * x, axis=-1, keepdims=True) + EPS)
            return y * tabs[0] + pltpu.roll(y, half, 1) * tabs[1]

        for j in range(n_q):
            sl = slice(j * HEAD_DIM, (j + 1) * HEAD_DIM)
            q_ref[0, rows, sl] = norm_rope(qkv[:, sl], tabs_q).astype(BF16)
        for j in range(n_kv):
            sl = slice(j * HEAD_DIM, (j + 1) * HEAD_DIM)
            src = slice((n_q + j) * HEAD_DIM, (n_q + j + 1) * HEAD_DIM)
            kv_ref[0, rows, sl] = norm_rope(qkv[:, src], tabs_k).astype(BF16)
        kv_ref[0, rows, n_kv * HEAD_DIM:] = qkv[:, (n_q + n_kv) * HEAD_DIM:].astype(BF16)


def _attn_in(h, nw, w_in, q_gain, k_gain, cc, ss, tm):
    bx, sx, d = h.shape
    n_in = w_in.shape[1]
    dq = d
    dkv = (n_in - dq) // 2
    grid = (bx, sx // tm)
    tok = lambda width: pl.BlockSpec((1, tm, width), lambda b, i: (b, i, 0))
    return pl.pallas_call(
        _attn_in_kernel,
        grid=grid,
        in_specs=[
            tok(d),
            _resident((1, d)),
            _resident((d, n_in)),
            _resident((1, HEAD_DIM)),
            _resident((1, HEAD_DIM)),
            pl.BlockSpec((tm, HEAD_DIM), lambda b, i: (i, 0)),
            pl.BlockSpec((tm, HEAD_DIM), lambda b, i: (i, 0)),
        ],
        out_specs=[tok(dq), tok(2 * dkv)],
        out_shape=[
            jax.ShapeDtypeStruct((bx, sx, dq), BF16),
            jax.ShapeDtypeStruct((bx, sx, 2 * dkv), BF16),
        ],
        compiler_params=_params(2),
        name="attn_in",
    )(h, nw, w_in, q_gain, k_gain, cc, ss)


def _softmax_pv(scores, values, sink_col):
    m = sink_col
    for s in scores:
        m = jnp.maximum(m, jnp.max(s, axis=-1, keepdims=True))
    den = jnp.exp(sink_col - m)
    acc = None
    for s, v in zip(scores, values):
        p = jnp.exp(s - m)
        den = den + jnp.sum(p, axis=-1, keepdims=True)
        pv = _dot(p.astype(BF16), v)
        acc = pv if acc is None else acc + pv
    return acc / den


def _attn_kernel(sink_ref, q_ref, kvp_ref, kvc_ref, kvn_ref, kvm_ref, o_ref, *, n_steps):
    step = pl.program_id(1)
    blk = ATTN_BLOCK
    n_blk = q_ref.shape[1] // blk
    v_off = N_KV_HEADS * HEAD_DIM
    a = lax.broadcasted_iota(jnp.int32, (GROUP * blk, 3 * blk), 0) % blk
    c = lax.broadcasted_iota(jnp.int32, (GROUP * blk, 3 * blk), 1)
    in_prev = (c < blk) & (c >= a)
    in_own = (c >= blk) & (c < 2 * blk)
    in_next = (c >= 2 * blk) & (c - 2 * blk <= a)
    interior = in_prev | in_own | in_next
    neg_inf = jnp.float32(-jnp.inf)
    kv_blocks = ([kvp_ref.at[0]] + [kvc_ref.at[0, pl.ds(i * blk, blk), :] for i in range(n_blk)]
                 + [kvn_ref.at[0]])
    for i in range(n_blk):
        rows = slice(i * blk, (i + 1) * blk)
        kv_refs = kv_blocks[i:i + 3]
        mask = interior
        if i == 0:
            mask = (in_prev & (step > 0)) | in_own | in_next
        if i == n_blk - 1:
            mask = mask & (in_prev | in_own | (step < n_steps - 1))
        for hk in range(N_KV_HEADS):
            ks = slice(hk * HEAD_DIM, (hk + 1) * HEAD_DIM)
            vs = slice(v_off + hk * HEAD_DIM, v_off + (hk + 1) * HEAD_DIM)
            keys = jnp.concatenate([r[:, ks] for r in kv_refs], axis=0)
            vals = jnp.concatenate([r[:, vs] for r in kv_refs], axis=0)
            heads = [hk * GROUP + g for g in range(GROUP)]
            q4 = jnp.concatenate(
                [q_ref[0, rows, h * HEAD_DIM:(h + 1) * HEAD_DIM] for h in heads], axis=0)
            s_band = jnp.where(mask, _dot_nt(q4, keys), neg_inf)
            s_meta = _dot_nt(q4, kvm_ref[0, :, ks])
            sink_col = jnp.concatenate(
                [jnp.full((blk, 1), sink_ref[h], F32) for h in heads], axis=0)
            o4 = _softmax_pv([s_band, s_meta], [vals, kvm_ref[0, :, vs]], sink_col)
            for g, h in enumerate(heads):
                o_ref[0, rows, h * HEAD_DIM:(h + 1) * HEAD_DIM] = (
                    o4[g * blk:(g + 1) * blk].astype(BF16))


def _attn(q, kv, kvm, sink):
    b, s, dq = q.shape
    dkv2 = kv.shape[-1]
    blk = ATTN_BLOCK
    n_blk = min(ATTN_BLOCKS_PER_STEP, s // blk)
    n_steps = s // (n_blk * blk)
    last_blk = s // blk - 1
    prev = pl.BlockSpec((1, blk, dkv2), lambda bi, n: (bi, jnp.maximum(n_blk * n - 1, 0), 0))
    own = pl.BlockSpec((1, n_blk * blk, dkv2), lambda bi, n: (bi, n, 0))
    nxt = pl.BlockSpec((1, blk, dkv2),
                       lambda bi, n: (bi, jnp.minimum(n_blk * (n + 1), last_blk), 0))
    meta = pl.BlockSpec((1, N_META, dkv2), lambda bi, n: (bi, 0, 0))
    return pl.pallas_call(
        functools.partial(_attn_kernel, n_steps=n_steps),
        grid=(b, n_steps),
        in_specs=[
            pl.BlockSpec(memory_space=pltpu.SMEM),
            pl.BlockSpec((1, n_blk * blk, dq), lambda bi, n: (bi, n, 0)),
            prev, own, nxt, meta,
        ],
        out_specs=pl.BlockSpec((1, n_blk * blk, dq), lambda bi, n: (bi, n, 0)),
        out_shape=jax.ShapeDtypeStruct((b, s, dq), BF16),
        compiler_params=_params(2),
        name="band_attn",
    )(sink, q, kv, kv, kv, kvm)


def _attn_meta_kernel(sink_ref, qm_ref, kvm_ref, kvr_ref, o_ref):
    rows = GROUP * N_META
    v_off = N_KV_HEADS * HEAD_DIM
    i = lax.broadcasted_iota(jnp.int32, (rows, ATTN_BLOCK), 0) % N_META
    j = lax.broadcasted_iota(jnp.int32, (rows, ATTN_BLOCK), 1)
    allowed = (N_META + j) - i <= WINDOW
    neg_inf = jnp.float32(-jnp.inf)
    for hk in range(N_KV_HEADS):
        ks = slice(hk * HEAD_DIM, (hk + 1) * HEAD_DIM)
        vs = slice(v_off + hk * HEAD_DIM, v_off + (hk + 1) * HEAD_DIM)
        heads = [hk * GROUP + g for g in range(GROUP)]
        q4 = jnp.concatenate(
            [qm_ref[0, :, h * HEAD_DIM:(h + 1) * HEAD_DIM] for h in heads], axis=0)
        s_meta = _dot_nt(q4, kvm_ref[0, :, ks])
        s_real = jnp.where(allowed, _dot_nt(q4, kvr_ref[0, :, ks]), neg_inf)
        sink_col = jnp.concatenate(
            [jnp.full((N_META, 1), sink_ref[h], F32) for h in heads], axis=0)
        o4 = _softmax_pv([s_meta, s_real], [kvm_ref[0, :, vs], kvr_ref[0, :, vs]], sink_col)
        for g, h in enumerate(heads):
            o_ref[0, :, h * HEAD_DIM:(h + 1) * HEAD_DIM] = (
                o4[g * N_META:(g + 1) * N_META].astype(BF16))


def _attn_meta(qm, kvm, kv, sink):
    b, _, dq = qm.shape
    dkv2 = kvm.shape[-1]
    meta = lambda width: pl.BlockSpec((1, N_META, width), lambda bi: (bi, 0, 0))
    first = pl.BlockSpec((1, ATTN_BLOCK, dkv2), lambda bi: (bi, 0, 0))
    return pl.pallas_call(
        _attn_meta_kernel,
        grid=(b,),
        in_specs=[pl.BlockSpec(memory_space=pltpu.SMEM), meta(dq), meta(dkv2), first],
        out_specs=meta(dq),
        out_shape=jax.ShapeDtypeStruct((b, N_META, dq), BF16),
        compiler_params=_params(1),
        name="meta_attn",
    )(sink, qm, kvm, kv)


def _post_kernel(*refs, rec, ff_chunk):
    if rec:
        h_ref, of_ref, ob_ref, gate_ref, og_ref, wo_ref, nw_ref, wu_ref, wd_ref, out_ref = refs
        o = of_ref[0].astype(F32) + ob_ref[0].astype(F32)
        n_heads = o.shape[-1] // HEAD_DIM
        o = jnp.concatenate(
            [_rms(o[:, j * HEAD_DIM:(j + 1) * HEAD_DIM], og_ref[...]) for j in range(n_heads)],
            axis=1)
        o = (o * gate_ref[0].astype(F32)).astype(BF16)
    else:
        h_ref, o_ref, wo_ref, nw_ref, wu_ref, wd_ref, out_ref = refs
        o = o_ref[0]
    h1 = h_ref[0] + _dot(o, wo_ref[...])
    hn = _rms(h1, nw_ref[...]).astype(BF16)
    d_ff = wu_ref.shape[1]
    acc = h1
    for c in range(d_ff // ff_chunk):
        sl = slice(c * ff_chunk, (c + 1) * ff_chunk)
        z = jnp.maximum(_dot(hn, wu_ref[:, sl]), 0.0)
        acc = acc + _dot((z * z).astype(BF16), wd_ref[sl, :])
    out_ref[0] = acc


def _post(h, mixer_out, w_out, nw, w_up, w_down, tm, rec_extra=None):
    bx, sx, d = h.shape
    d_ff = w_up.shape[1]
    tok = pl.BlockSpec((1, tm, d), lambda b, i: (b, i, 0))
    rec = rec_extra is not None
    if rec:
        ins = [h, *mixer_out, rec_extra, w_out, nw, w_up, w_down]
        specs = [tok, tok, tok, tok, _resident((1, HEAD_DIM))]
    else:
        ins = [h, mixer_out, w_out, nw, w_up, w_down]
        specs = [tok, tok]
    specs += [_resident((d, d)), _resident((1, d)), _resident((d, d_ff)), _resident((d_ff, d))]
    return pl.pallas_call(
        functools.partial(_post_kernel, rec=rec, ff_chunk=min(1024, d_ff)),
        grid=(bx, sx // tm),
        in_specs=specs,
        out_specs=tok,
        out_shape=jax.ShapeDtypeStruct((bx, sx, d), F32),
        compiler_params=_params(2),
        name="post_rec" if rec else "post_attn",
    )(*ins)


def _lower_bound(tab_ref, layer):
    t = tab_ref[...]
    e = jnp.exp(t - jnp.max(t, axis=0, keepdims=True))
    p = e / jnp.sum(e, axis=0, keepdims=True)
    return jnp.sum(p[1:layer + 1], axis=0, keepdims=True)


def _rec_in_rows(h, nw_ref, w_ref, lbf_ref, lbb_ref, layer):
    d = h.shape[-1]
    hn = _rms(h, nw_ref[...]).astype(BF16)
    proj = lambda j: _dot(hn, w_ref[:, j * d:(j + 1) * d])

    def log_forget(z, tab_ref):
        lb = _lower_bound(tab_ref, layer)
        return jnp.log2(lb + (1.0 - lb) * jax.nn.sigmoid(z))

    zq = proj(0)
    q = zq * jax.nn.sigmoid(zq)
    v = proj(1)
    gf = log_forget(proj(2), lbf_ref)
    gb = log_forget(proj(3), lbb_ref)
    zg = proj(4)
    gate = zg * jax.nn.sigmoid(zg)
    return q, v, gf, gb, gate


def _rec_in_kernel(h_ref, nw_ref, w_ref, lbf_ref, lbb_ref,
                   q_ref, v_ref, gf_ref, gb_ref, gate_ref, *, layer):
    tm = h_ref.shape[1]
    sub = min(tm, SUB_TILE)
    for r in range(tm // sub):
        rows = slice(r * sub, (r + 1) * sub)
        q, v, gf, gb, gate = _rec_in_rows(h_ref[0, rows, :], nw_ref, w_ref, lbf_ref, lbb_ref,
                                          layer)
        q_ref[0, rows, :] = q.astype(BF16)
        v_ref[0, rows, :] = v.astype(BF16)
        gf_ref[0, rows, :] = gf
        gb_ref[0, rows, :] = gb
        gate_ref[0, rows, :] = gate.astype(BF16)


def _rec_in(h, nw, w_in, lb_f, lb_b, layer, tm):
    b, s, d = h.shape
    depth = lb_f.shape[0]
    tok = pl.BlockSpec((1, tm, d), lambda bi, i: (bi, i, 0))
    return pl.pallas_call(
        functools.partial(_rec_in_kernel, layer=layer),
        grid=(b, s // tm),
        in_specs=[tok, _resident((1, d)), _resident((d, w_in.shape[1])),
                  _resident((depth, d)), _resident((depth, d))],
        out_specs=[tok, tok, tok, tok, tok],
        out_shape=[
            jax.ShapeDtypeStruct((b, s, d), BF16),
            jax.ShapeDtypeStruct((b, s, d), BF16),
            jax.ShapeDtypeStruct((b, s, d), F32),
            jax.ShapeDtypeStruct((b, s, d), F32),
            jax.ShapeDtypeStruct((b, s, d), BF16),
        ],
        compiler_params=_params(2),
        name="rec_in",
    )(h, nw, w_in, lb_f, lb_b)


def _rec_in_meta_kernel(hm_ref, nw_ref, w_ref, lbf_ref, lbb_ref,
                        q_ref, v_ref, gf_ref, gb_ref, gate_ref, *, layer):
    b, n_meta, d = hm_ref.shape
    h = hm_ref[...].reshape(b * n_meta, d)
    q, v, gf, gb, gate = _rec_in_rows(h, nw_ref, w_ref, lbf_ref, lbb_ref, layer)
    pad = REC_CHUNK - n_meta
    for ref, val in ((q_ref, q), (v_ref, v), (gf_ref, gf), (gb_ref, gb)):
        ref[:, :pad, :] = jnp.zeros((b, pad, d), ref.dtype)
        ref[:, pad:, :] = val.reshape(b, n_meta, d).astype(ref.dtype)
    gate_ref[...] = gate.reshape(b, n_meta, d).astype(BF16)


def _rec_in_meta(hm, nw, w_in, lb_f, lb_b, layer):
    b, n_meta, d = hm.shape
    depth = lb_f.shape[0]
    chunk = pl.BlockSpec((b, REC_CHUNK, d), lambda i: (0, 0, 0))
    whole = pl.BlockSpec((b, n_meta, d), lambda i: (0, 0, 0))
    return pl.pallas_call(
        functools.partial(_rec_in_meta_kernel, layer=layer),
        grid=(1,),
        in_specs=[whole, _resident((1, d)), _resident((d, w_in.shape[1])),
                  _resident((depth, d)), _resident((depth, d))],
        out_specs=[chunk, chunk, chunk, chunk, whole],
        out_shape=[
            jax.ShapeDtypeStruct((b, REC_CHUNK, d), BF16),
            jax.ShapeDtypeStruct((b, REC_CHUNK, d), BF16),
            jax.ShapeDtypeStruct((b, REC_CHUNK, d), F32),
            jax.ShapeDtypeStruct((b, REC_CHUNK, d), F32),
            jax.ShapeDtypeStruct((b, n_meta, d), BF16),
        ],
        compiler_params=_params(1),
        name="rec_in_meta",
    )(hm, nw, w_in, lb_f, lb_b)


_SUM_LEVEL_HALVES = (32, 16, 8, 4, 2)
_N_SUM_LEVELS = len(_SUM_LEVEL_HALVES)
_MASK_ADJACENT = _N_SUM_LEVELS
_MASK_DIAG = _N_SUM_LEVELS + 1
_ROWS_Q_INTER = _N_SUM_LEVELS * REC_CHUNK
_ROWS_K_STATE = (_N_SUM_LEVELS + 1) * REC_CHUNK
_N_SUM_ROWS = (_N_SUM_LEVELS + 2) * REC_CHUNK
SCAN_CHUNKS_PER_STEP = 4
SCAN_PIPELINE_LAG = 8
SUBLANE_TILE = 8


def _scan_constants(backward):
    c = REC_CHUNK
    u = np.arange(c)
    w = (u[None, :] >= u[:, None]) if backward else (u[None, :] <= u[:, None])
    masks = np.zeros((_N_SUM_LEVELS + 2, c, c), np.float32)
    for lvl, half in enumerate(_SUM_LEVEL_HALVES + (1,)):
        for t in range(c):
            start = (t // (2 * half)) * 2 * half
            if t - start >= half:
                masks[lvl, t, start:start + half] = 1.0
    masks[_MASK_DIAG] = np.eye(c, dtype=np.float32)
    if backward:
        masks = masks[:, ::-1, ::-1]
    w = np.concatenate([w, w], axis=1).astype(np.float32)
    return jnp.asarray(w, BF16), jnp.asarray(np.ascontiguousarray(masks), F32)


def _split_decays(cum, backward):
    c, d = cum.shape
    tile = SUBLANE_TILE
    sub = lax.broadcasted_iota(jnp.int32, (tile, d), 0)
    sign4 = jnp.where((sub < 4) != backward, -1.0, 1.0).astype(F32)
    sign2 = jnp.where(((sub % 4) < 2) != backward, -1.0, 1.0).astype(F32)
    row = lambda r: cum[r:r + 1]
    tiles = [cum[t:t + tile] for t in range(0, c, tile)]
    out = []
    for half in _SUM_LEVEL_HALVES:
        for i, x in enumerate(tiles):
            t0 = i * tile
            if half >= tile:
                start = (t0 // (2 * half)) * 2 * half
                r = start + half if backward else start + half - 1
                after = (t0 - start >= half) != backward
                out.append(x - row(r) if after else row(r) - x)
            elif half == 4:
                r = t0 + 4 if backward else t0 + 3
                out.append((x - row(r)) * sign4)
            else:
                r1, r2 = (t0 + 2, t0 + 6) if backward else (t0 + 1, t0 + 5)
                out.append((x - jnp.where(sub < 4, row(r1), row(r2))) * sign2)
    out.extend(tiles)
    last = 0 if backward else c - 1
    out.extend(row(last) - x for x in tiles)
    return jnp.concatenate(out, axis=0)


class _ChunkJob:
    def __init__(self, d, q_ref, v_ref, g_ref, o_ref, rows, slot):
        self.d, self.q_ref, self.v_ref, self.g_ref, self.o_ref = d, q_ref, v_ref, g_ref, o_ref
        self.rows, self.slot = rows, slot


def _scan_kernel(qf_ref, vf_ref, gf_ref, qb_ref, vb_ref, gb_ref, qm_ref, vm_ref, gfm_ref, gbm_ref,
                 wf_ref, wb_ref, mf_ref, mb_ref, of_ref, ob_ref, ofm_ref, obm_ref,
                 st_ref, e_ref):
    c = REC_CHUNK
    n_heads = qf_ref.shape[-1] // HEAD_DIM
    n_chunks = qf_ref.shape[1] // c
    step = pl.program_id(1)
    w_refs = (wf_ref, wb_ref)
    m_refs = (mf_ref, mb_ref)
    whole_rows = (c - 1, 0)

    @pl.when(step == 0)
    def _():
        st_ref[...] = jnp.zeros(st_ref.shape, F32)

    def decay_stage(job):
        g2 = job.g_ref[0, job.rows, :]
        g_hi = g2.astype(BF16)
        g_lo = (g2 - g_hi.astype(F32)).astype(BF16)
        cum = _dot(w_refs[job.d][...], jnp.concatenate([g_hi, g_lo], axis=0))
        e_ref[job.slot, job.d] = jnp.exp2(_split_decays(cum, backward=job.d == 1))

    def pair_stage(job, hd):
        d, m_ref = job.d, m_refs[job.d]
        sl = slice(hd * HEAD_DIM, (hd + 1) * HEAD_DIM)
        e = lambda r0: e_ref[job.slot, d, r0:r0 + c, sl]
        q = job.q_ref[0, job.rows, sl].astype(F32)
        f = jnp.exp2(job.g_ref[0, job.rows, sl])
        k = 1.0 - f
        row_id = lax.broadcasted_iota(jnp.int32, (c, HEAD_DIM), 0)
        near = _dot_nt(jnp.concatenate([q, q * f], axis=0).astype(BF16), k.astype(BF16))
        pair = m_ref[_MASK_DIAG] * near[:c] + m_ref[_MASK_ADJACENT] * near[c:]
        for lvl, half in enumerate(_SUM_LEVEL_HALVES):
            is_query = ((row_id // half) % 2 == 1) != (d == 1)
            both = (jnp.where(is_query, q, k) * e(lvl * c)).astype(BF16)
            pair = pair + m_ref[lvl] * _dot_nt(both, both)
        q_in = (q * e(_ROWS_Q_INTER)).astype(BF16)
        k_out = (k * e(_ROWS_K_STATE)).astype(BF16)
        return pair.astype(BF16), q_in, k_out

    def output_stage(job, hd, pair, q_in, k_out):
        d = job.d
        sl = slice(hd * HEAD_DIM, (hd + 1) * HEAD_DIM)
        v = job.v_ref[0, job.rows, sl]
        state = st_ref[d, hd]
        o = _dot(pair, v) + _dot(q_in, state.astype(BF16))
        job.o_ref[0, job.rows, sl] = o.astype(BF16)
        row = _ROWS_Q_INTER + whole_rows[d]
        t0 = (row // SUBLANE_TILE) * SUBLANE_TILE
        decay_col = e_ref[job.slot, d, t0:t0 + SUBLANE_TILE, sl].T[:, row - t0:row - t0 + 1]
        st_ref[d, hd] = state * decay_col + _dot_tn(k_out, v)

    def run(jobs):
        pending = []
        for group in jobs:
            for job in group:
                decay_stage(job)
            for hd in range(n_heads):
                for job in group:
                    pending.append((job, hd) + pair_stage(job, hd))
                    if len(pending) > SCAN_PIPELINE_LAG:
                        output_stage(*pending.pop(0))
        for item in pending:
            output_stage(*item)

    whole = slice(0, c)

    @pl.when(step == 0)
    def _():
        run([[_ChunkJob(0, qm_ref, vm_ref, gfm_ref, ofm_ref, whole, 0)]])

    groups = []
    for i in range(n_chunks):
        fwd_rows = slice(i * c, (i + 1) * c)
        bwd_rows = slice((n_chunks - 1 - i) * c, (n_chunks - i) * c)
        groups.append([_ChunkJob(0, qf_ref, vf_ref, gf_ref, of_ref, fwd_rows, i),
                       _ChunkJob(1, qb_ref, vb_ref, gb_ref, ob_ref, bwd_rows, i)])
    run(groups)

    @pl.when(step == pl.num_programs(1) - 1)
    def _():
        run([[_ChunkJob(1, qm_ref, vm_ref, gbm_ref, obm_ref, whole, 0)]])


def _scan(q, v, gf, gb, qm, vm, gfm, gbm):
    b, s, d = q.shape
    n_heads = d // HEAD_DIM
    chunks_per_step = min(SCAN_CHUNKS_PER_STEP, s // REC_CHUNK)
    rows = chunks_per_step * REC_CHUNK
    n_steps = s // rows
    wf, mf = _scan_constants(False)
    wb, mb = _scan_constants(True)
    fwd = pl.BlockSpec((1, rows, d), lambda bi, j: (bi, j, 0))
    bwd = pl.BlockSpec((1, rows, d), lambda bi, j: (bi, n_steps - 1 - j, 0))
    meta = pl.BlockSpec((1, REC_CHUNK, d), lambda bi, j: (bi, 0, 0))
    return pl.pallas_call(
        _scan_kernel,
        grid=(b, n_steps),
        in_specs=[fwd, fwd, fwd, bwd, bwd, bwd, meta, meta, meta, meta,
                  _resident(wf.shape), _resident(wb.shape),
                  _resident(mf.shape), _resident(mb.shape)],
        out_specs=[fwd, bwd, meta, meta],
        out_shape=[jax.ShapeDtypeStruct((b, s, d), BF16)] * 2
        + [jax.ShapeDtypeStruct((b, REC_CHUNK, d), BF16)] * 2,
        scratch_shapes=[
            pltpu.VMEM((2, n_heads, HEAD_DIM, HEAD_DIM), F32),
            pltpu.VMEM((chunks_per_step, 2, _N_SUM_ROWS, d), F32),
        ],
        compiler_params=_params(2),
        name="gla_scan",
    )(q, v, gf, q, v, gb, qm, vm, gfm, gbm, wf, wb, mf, mb)


def _rope_tables(length):
    inv_freq = ROPE_THETA ** (-jnp.arange(0, HEAD_DIM, 2, dtype=F32) / HEAD_DIM)
    ang = jnp.arange(length, dtype=F32)[:, None] * inv_freq[None, :]
    cos, sin = jnp.cos(ang), jnp.sin(ang)
    return jnp.concatenate([cos, cos], axis=-1), jnp.concatenate([-sin, sin], axis=-1)


def kernel(x, meta_tokens, mix_norm_w, mlp_norm_w, attn_w_in, attn_w_out, attn_q_norm_w,
           attn_k_norm_w, attn_sink, rec_w_in, rec_w_out, rec_lb_fwd, rec_lb_bwd,
           rec_out_norm_w, mlp_w_up, mlp_w_down):
    b, s, d = x.shape
    depth = mix_norm_w.shape[0]
    n_mixers = 2
    tm = min(TOKEN_TILE, s)
    n_meta_rows = b * N_META

    h = x
    hm = jnp.broadcast_to(meta_tokens.astype(x.dtype)[None], (b, N_META, d))
    cc, ss = _rope_tables(N_META + s)
    cc_meta, ss_meta = jnp.tile(cc[:N_META], (b, 1)), jnp.tile(ss[:N_META], (b, 1))
    cc_real, ss_real = cc[N_META:], ss[N_META:]
    row = lambda w: w.reshape(1, -1)
    flat_meta = lambda t: t.reshape(1, n_meta_rows, t.shape[-1])
    per_batch = lambda t: t.reshape(b, N_META, t.shape[-1])

    for layer in range(depth):
        j = layer // n_mixers
        nw = row(mix_norm_w[layer])
        mlp = (row(mlp_norm_w[layer]), mlp_w_up[layer].astype(BF16),
               mlp_w_down[layer].astype(BF16))
        if layer % n_mixers == 0:
            w_in = attn_w_in[j].astype(BF16)
            w_out = attn_w_out[j].astype(BF16)
            gains = (row(attn_q_norm_w[j]), row(attn_k_norm_w[j]))
            q, kv = _attn_in(h, nw, w_in, *gains, cc_real, ss_real, tm)
            qm, kvm = map(per_batch, _attn_in(
                flat_meta(hm), nw, w_in, *gains, cc_meta, ss_meta, n_meta_rows))
            sink = attn_sink[j].astype(F32)
            o = _attn(q, kv, kvm, sink)
            om = _attn_meta(qm, kvm, kv, sink)
            h = _post(h, o, w_out, *mlp, tm)
            hm = per_batch(_post(flat_meta(hm), flat_meta(om), w_out, *mlp, n_meta_rows))
        else:
            w_in = rec_w_in[j].astype(BF16)
            w_out = rec_w_out[j].astype(BF16)
            gain = row(rec_out_norm_w[j])
            q, v, gf, gb, gate = _rec_in(h, nw, w_in, rec_lb_fwd, rec_lb_bwd, layer, tm)
            qm, vm, gfm, gbm, gate_m = _rec_in_meta(hm, nw, w_in, rec_lb_fwd, rec_lb_bwd, layer)
            o_f, o_b, o_fm, o_bm = _scan(q, v, gf, gb, qm, vm, gfm, gbm)
            h = _post(h, (o_f, o_b, gate), w_out, *mlp, tm, rec_extra=gain)
            meta_rows = lambda t: flat_meta(t[:, REC_CHUNK - N_META:])
            hm = per_batch(_post(flat_meta(hm), (meta_rows(o_fm), meta_rows(o_bm),
                                                 flat_meta(gate_m)),
                                 w_out, *mlp, n_meta_rows, rec_extra=gain))
    return h
```

```python
import functools

import numpy as np
import jax
import jax.numpy as jnp
from jax import lax
from jax.experimental import pallas as pl
from jax.experimental.pallas import tpu as pltpu

N_META = 16
HEAD_DIM = 128
N_KV_HEADS = 2
GROUP = 4
WINDOW = 128
ATTN_BLOCK = 128
ROPE_THETA = 10000.0
REC_CHUNK = 64
EPS = 1e-6

F32 = jnp.float32
BF16 = jnp.bfloat16

V7X_VMEM_BYTES = 64 * 1024 * 1024
V7X_VMEM_LIMIT_BYTES = V7X_VMEM_BYTES * 7 // 8
TOKEN_TILE = 512
REC_IN_TOKEN_TILE = 1024
SUB_TILE = 256
ATTN_BLOCKS_PER_STEP = 4


def _params(n_axes):
    return pltpu.CompilerParams(
        dimension_semantics=("arbitrary",) * n_axes,
        vmem_limit_bytes=V7X_VMEM_LIMIT_BYTES,
    )


def _resident(shape):
    zeros = (0,) * len(shape)
    return pl.BlockSpec(shape, lambda *_: zeros, pipeline_mode=pl.Buffered(1))


def _rms(x, w):
    ms = jnp.mean(x * x, axis=-1, keepdims=True)
    return x * lax.rsqrt(ms + EPS) * w


def _dot(a, b):
    return jnp.dot(a, b, preferred_element_type=F32)


def _dot_nt(a, b):
    return lax.dot_general(a, b, (((1,), (1,)), ((), ())), preferred_element_type=F32)


def _dot_tn(a, b):
    return lax.dot_general(a, b, (((0,), (0,)), ((), ())), preferred_element_type=F32)


def _attn_in_kernel(h_ref, nw_ref, w_ref, qg_ref, kg_ref, cc_ref, ss_ref, q_ref, kv_ref):
    n_q = q_ref.shape[-1] // HEAD_DIM
    n_kv = kv_ref.shape[-1] // (2 * HEAD_DIM)
    tm = h_ref.shape[1]
    sub = min(tm, SUB_TILE)
    half = HEAD_DIM // 2
    q_gain = qg_ref[...] * (HEAD_DIM ** -0.5)
    k_gain = kg_ref[...]

    for r in range(tm // sub):
        rows = slice(r * sub, (r + 1) * sub)
        hn = _rms(h_ref[0, rows, :], nw_ref[...]).astype(BF16)
        qkv = _dot(hn, w_ref[...])
        cc = cc_ref[rows, :]
        ss = ss_ref[rows, :]
        tabs_q = (q_gain * cc, pltpu.roll(q_gain, half, 1) * ss)
        tabs_k = (k_gain * cc, pltpu.roll(k_gain, half, 1) * ss)

        def norm_rope(x, tabs):
            y = x * lax.rsqrt(jnp.mean(x * x, axis=-1, keepdims=True) + EPS)
            return y * tabs[0] + pltpu.roll(y, half, 1) * tabs[1]

        for j in range(n_q):
            sl = slice(j * HEAD_DIM, (j + 1) * HEAD_DIM)
            q_ref[0, rows, sl] = norm_rope(qkv[:, sl], tabs_q).astype(BF16)
        for j in range(n_kv):
            sl = slice(j * HEAD_DIM, (j + 1) * HEAD_DIM)
            src = slice((n_q + j) * HEAD_DIM, (n_q + j + 1) * HEAD_DIM)
            kv_ref[0, rows, sl] = norm_rope(qkv[:, src], tabs_k).astype(BF16)
        kv_ref[0, rows, n_kv * HEAD_DIM:] = qkv[:, (n_q + n_kv) * HEAD_DIM:].astype(BF16)


def _attn_in(h, nw, w_in, q_gain, k_gain, cc, ss, tm):
    bx, sx, d = h.shape
    n_in = w_in.shape[1]
    dq = d
    dkv = (n_in - dq) // 2
    grid = (bx, sx // tm)
    tok = lambda width: pl.BlockSpec((1, tm, width), lambda b, i: (b, i, 0))
    return pl.pallas_call(
        _attn_in_kernel,
        grid=grid,
        in_specs=[
            tok(d),
            _resident((1, d)),
            _resident((d, n_in)),
            _resident((1, HEAD_DIM)),
            _resident((1, HEAD_DIM)),
            pl.BlockSpec((tm, HEAD_DIM), lambda b, i: (i, 0)),
            pl.BlockSpec((tm, HEAD_DIM), lambda b, i: (i, 0)),
        ],
        out_specs=[tok(dq), tok(2 * dkv)],
        out_shape=[
            jax.ShapeDtypeStruct((bx, sx, dq), BF16),
            jax.ShapeDtypeStruct((bx, sx, 2 * dkv), BF16),
        ],
        compiler_params=_params(2),
        name="attn_in",
    )(h, nw, w_in, q_gain, k_gain, cc, ss)


def _softmax_pv(scores, values, sink_col):
    m = sink_col
    for s in scores:
        m = jnp.maximum(m, jnp.max(s, axis=-1, keepdims=True))
    den = jnp.exp(sink_col - m)
    acc = None
    for s, v in zip(scores, values):
        p = jnp.exp(s - m)
        den = den + jnp.sum(p, axis=-1, keepdims=True)
        pv = _dot(p.astype(BF16), v)
        acc = pv if acc is None else acc + pv
    return acc / den


def _attn_kernel(sink_ref, q_ref, kvp_ref, kvc_ref, kvn_ref, kvm_ref, o_ref, *, n_steps):
    step = pl.program_id(1)
    blk = ATTN_BLOCK
    n_blk = q_ref.shape[1] // blk
    v_off = N_KV_HEADS * HEAD_DIM
    a = lax.broadcasted_iota(jnp.int32, (GROUP * blk, 3 * blk), 0) % blk
    c = lax.broadcasted_iota(jnp.int32, (GROUP * blk, 3 * blk), 1)
    in_prev = (c < blk) & (c >= a)
    in_own = (c >= blk) & (c < 2 * blk)
    in_next = (c >= 2 * blk) & (c - 2 * blk <= a)
    interior = in_prev | in_own | in_next
    neg_inf = jnp.float32(-jnp.inf)
    kv_blocks = ([kvp_ref.at[0]] + [kvc_ref.at[0, pl.ds(i * blk, blk), :] for i in range(n_blk)]
                 + [kvn_ref.at[0]])
    for i in range(n_blk):
        rows = slice(i * blk, (i + 1) * blk)
        kv_refs = kv_blocks[i:i + 3]
        mask = interior
        if i == 0:
            mask = (in_prev & (step > 0)) | in_own | in_next
        if i == n_blk - 1:
            mask = mask & (in_prev | in_own | (step < n_steps - 1))
        for hk in range(N_KV_HEADS):
            ks = slice(hk * HEAD_DIM, (hk + 1) * HEAD_DIM)
            vs = slice(v_off + hk * HEAD_DIM, v_off + (hk + 1) * HEAD_DIM)
            keys = jnp.concatenate([r[:, ks] for r in kv_refs], axis=0)
            vals = jnp.concatenate([r[:, vs] for r in kv_refs], axis=0)
            heads = [hk * GROUP + g for g in range(GROUP)]
            q4 = jnp.concatenate(
                [q_ref[0, rows, h * HEAD_DIM:(h + 1) * HEAD_DIM] for h in heads], axis=0)
            s_band = jnp.where(mask, _dot_nt(q4, keys), neg_inf)
            s_meta = _dot_nt(q4, kvm_ref[0, :, ks])
            sink_col = jnp.concatenate(
                [jnp.full((blk, 1), sink_ref[h], F32) for h in heads], axis=0)
            o4 = _softmax_pv([s_band, s_meta], [vals, kvm_ref[0, :, vs]], sink_col)
            for g, h in enumerate(heads):
                o_ref[0, rows, h * HEAD_DIM:(h + 1) * HEAD_DIM] = (
                    o4[g * blk:(g + 1) * blk].astype(BF16))


def _attn(q, kv, kvm, sink):
    b, s, dq = q.shape
    dkv2 = kv.shape[-1]
    blk = ATTN_BLOCK
    n_blk = min(ATTN_BLOCKS_PER_STEP, s // blk)
    n_steps = s // (n_blk * blk)
    last_blk = s // blk - 1
    prev = pl.BlockSpec((1, blk, dkv2), lambda bi, n: (bi, jnp.maximum(n_blk * n - 1, 0), 0))
    own = pl.BlockSpec((1, n_blk * blk, dkv2), lambda bi, n: (bi, n, 0))
    nxt = pl.BlockSpec((1, blk, dkv2),
                       lambda bi, n: (bi, jnp.minimum(n_blk * (n + 1), last_blk), 0))
    meta = pl.BlockSpec((1, N_META, dkv2), lambda bi, n: (bi, 0, 0))
    return pl.pallas_call(
        functools.partial(_attn_kernel, n_steps=n_steps),
        grid=(b, n_steps),
        in_specs=[
            pl.BlockSpec(memory_space=pltpu.SMEM),
            pl.BlockSpec((1, n_blk * blk, dq), lambda bi, n: (bi, n, 0)),
            prev, own, nxt, meta,
        ],
        out_specs=pl.BlockSpec((1, n_blk * blk, dq), lambda bi, n: (bi, n, 0)),
        out_shape=jax.ShapeDtypeStruct((b, s, dq), BF16),
        compiler_params=_params(2),
        name="band_attn",
    )(sink, q, kv, kv, kv, kvm)


def _attn_meta_kernel(sink_ref, qm_ref, kvm_ref, kvr_ref, o_ref):
    rows = GROUP * N_META
    v_off = N_KV_HEADS * HEAD_DIM
    i = lax.broadcasted_iota(jnp.int32, (rows, ATTN_BLOCK), 0) % N_META
    j = lax.broadcasted_iota(jnp.int32, (rows, ATTN_BLOCK), 1)
    allowed = (N_META + j) - i <= WINDOW
    neg_inf = jnp.float32(-jnp.inf)
    for hk in range(N_KV_HEADS):
        ks = slice(hk * HEAD_DIM, (hk + 1) * HEAD_DIM)
        vs = slice(v_off + hk * HEAD_DIM, v_off + (hk + 1) * HEAD_DIM)
        heads = [hk * GROUP + g for g in range(GROUP)]
        q4 = jnp.concatenate(
            [qm_ref[0, :, h * HEAD_DIM:(h + 1) * HEAD_DIM] for h in heads], axis=0)
        s_meta = _dot_nt(q4, kvm_ref[0, :, ks])
        s_real = jnp.where(allowed, _dot_nt(q4, kvr_ref[0, :, ks]), neg_inf)
        sink_col = jnp.concatenate(
            [jnp.full((N_META, 1), sink_ref[h], F32) for h in heads], axis=0)
        o4 = _softmax_pv([s_meta, s_real], [kvm_ref[0, :, vs], kvr_ref[0, :, vs]], sink_col)
        for g, h in enumerate(heads):
            o_ref[0, :, h * HEAD_DIM:(h + 1) * HEAD_DIM] = (
                o4[g * N_META:(g + 1) * N_META].astype(BF16))


def _attn_meta(qm, kvm, kv, sink):
    b, _, dq = qm.shape
    dkv2 = kvm.shape[-1]
    meta = lambda width: pl.BlockSpec((1, N_META, width), lambda bi: (bi, 0, 0))
    first = pl.BlockSpec((1, ATTN_BLOCK, dkv2), lambda bi: (bi, 0, 0))
    return pl.pallas_call(
        _attn_meta_kernel,
        grid=(b,),
        in_specs=[pl.BlockSpec(memory_space=pltpu.SMEM), meta(dq), meta(dkv2), first],
        out_specs=meta(dq),
        out_shape=jax.ShapeDtypeStruct((b, N_META, dq), BF16),
        compiler_params=_params(1),
        name="meta_attn",
    )(sink, qm, kvm, kv)


def _post_kernel(*refs, rec, ff_chunk):
    if rec:
        h_ref, of_ref, ob_ref, gate_ref, og_ref, wo_ref, nw_ref, wu_ref, wd_ref, out_ref = refs
        o = of_ref[0].astype(F32) + ob_ref[0].astype(F32)
        n_heads = o.shape[-1] // HEAD_DIM
        o = jnp.concatenate(
            [_rms(o[:, j * HEAD_DIM:(j + 1) * HEAD_DIM], og_ref[...]) for j in range(n_heads)],
            axis=1)
        o = (o * gate_ref[0].astype(F32)).astype(BF16)
    else:
        h_ref, o_ref, wo_ref, nw_ref, wu_ref, wd_ref, out_ref = refs
        o = o_ref[0]
    h1 = h_ref[0] + _dot(o, wo_ref[...])
    hn = _rms(h1, nw_ref[...]).astype(BF16)
    d_ff = wu_ref.shape[1]
    acc = h1
    for c in range(d_ff // ff_chunk):
        sl = slice(c * ff_chunk, (c + 1) * ff_chunk)
        z = jnp.maximum(_dot(hn, wu_ref[:, sl]), 0.0)
        acc = acc + _dot((z * z).astype(BF16), wd_ref[sl, :])
    out_ref[0] = acc


def _post(h, mixer_out, w_out, nw, w_up, w_down, tm, rec_extra=None):
    bx, sx, d = h.shape
    d_ff = w_up.shape[1]
    tok = pl.BlockSpec((1, tm, d), lambda b, i: (b, i, 0))
    rec = rec_extra is not None
    if rec:
        ins = [h, *mixer_out, rec_extra, w_out, nw, w_up, w_down]
        specs = [tok, tok, tok, tok, _resident((1, HEAD_DIM))]
    else:
        ins = [h, mixer_out, w_out, nw, w_up, w_down]
        specs = [tok, tok]
    specs += [_resident((d, d)), _resident((1, d)), _resident((d, d_ff)), _resident((d_ff, d))]
    return pl.pallas_call(
        functools.partial(_post_kernel, rec=rec, ff_chunk=min(1024, d_ff)),
        grid=(bx, sx // tm),
        in_specs=specs,
        out_specs=tok,
        out_shape=jax.ShapeDtypeStruct((bx, sx, d), F32),
        compiler_params=_params(2),
        name="post_rec" if rec else "post_attn",
    )(*ins)


def _lower_bound(tab_ref, layer):
    t = tab_ref[...]
    e = jnp.exp(t - jnp.max(t, axis=0, keepdims=True))
    p = e / jnp.sum(e, axis=0, keepdims=True)
    return jnp.sum(p[1:layer + 1], axis=0, keepdims=True)


def _rec_in_rows(h, nw_ref, w_ref, lbf_ref, lbb_ref, layer):
    d = h.shape[-1]
    hn = _rms(h, nw_ref[...]).astype(BF16)
    proj = lambda j: _dot(hn, w_ref[:, j * d:(j + 1) * d])

    def log_forget(z, tab_ref):
        lb = _lower_bound(tab_ref, layer)
        return jnp.log2(lb + (1.0 - lb) * jax.nn.sigmoid(z))

    zq = proj(0)
    q = zq * jax.nn.sigmoid(zq)
    v = proj(1)
    gf = log_forget(proj(2), lbf_ref)
    gb = log_forget(proj(3), lbb_ref)
    zg = proj(4)
    gate = zg * jax.nn.sigmoid(zg)
    return q, v, gf, gb, gate


def _rec_in_kernel(h_ref, nw_ref, w_ref, lbf_ref, lbb_ref,
                   q_ref, v_ref, gf_ref, gb_ref, gate_ref, *, layer):
    tm = h_ref.shape[1]
    sub = min(tm, SUB_TILE)
    for r in range(tm // sub):
        rows = slice(r * sub, (r + 1) * sub)
        q, v, gf, gb, gate = _rec_in_rows(h_ref[0, rows, :], nw_ref, w_ref, lbf_ref, lbb_ref,
                                          layer)
        q_ref[0, rows, :] = q.astype(BF16)
        v_ref[0, rows, :] = v.astype(BF16)
        gf_ref[0, rows, :] = gf
        gb_ref[0, rows, :] = gb
        gate_ref[0, rows, :] = gate.astype(BF16)


def _rec_in(h, nw, w_in, lb_f, lb_b, layer, tm):
    b, s, d = h.shape
    depth = lb_f.shape[0]
    tok = pl.BlockSpec((1, tm, d), lambda bi, i: (bi, i, 0))
    return pl.pallas_call(
        functools.partial(_rec_in_kernel, layer=layer),
        grid=(b, s // tm),
        in_specs=[tok, _resident((1, d)), _resident((d, w_in.shape[1])),
                  _resident((depth, d)), _resident((depth, d))],
        out_specs=[tok, tok, tok, tok, tok],
        out_shape=[
            jax.ShapeDtypeStruct((b, s, d), BF16),
            jax.ShapeDtypeStruct((b, s, d), BF16),
            jax.ShapeDtypeStruct((b, s, d), F32),
            jax.ShapeDtypeStruct((b, s, d), F32),
            jax.ShapeDtypeStruct((b, s, d), BF16),
        ],
        compiler_params=_params(2),
        name="rec_in",
    )(h, nw, w_in, lb_f, lb_b)


def _rec_in_meta_kernel(hm_ref, nw_ref, w_ref, lbf_ref, lbb_ref,
                        q_ref, v_ref, gf_ref, gb_ref, gate_ref, *, layer):
    b, n_meta, d = hm_ref.shape
    h = hm_ref[...].reshape(b * n_meta, d)
    q, v, gf, gb, gate = _rec_in_rows(h, nw_ref, w_ref, lbf_ref, lbb_ref, layer)
    pad = REC_CHUNK - n_meta
    for ref, val in ((q_ref, q), (v_ref, v), (gf_ref, gf), (gb_ref, gb)):
        ref[:, :pad, :] = jnp.zeros((b, pad, d), ref.dtype)
        ref[:, pad:, :] = val.reshape(b, n_meta, d).astype(ref.dtype)
    gate_ref[...] = gate.reshape(b, n_meta, d).astype(BF16)


def _rec_in_meta(hm, nw, w_in, lb_f, lb_b, layer):
    b, n_meta, d = hm.shape
    depth = lb_f.shape[0]
    chunk = pl.BlockSpec((b, REC_CHUNK, d), lambda i: (0, 0, 0))
    whole = pl.BlockSpec((b, n_meta, d), lambda i: (0, 0, 0))
    return pl.pallas_call(
        functools.partial(_rec_in_meta_kernel, layer=layer),
        grid=(1,),
        in_specs=[whole, _resident((1, d)), _resident((d, w_in.shape[1])),
                  _resident((depth, d)), _resident((depth, d))],
        out_specs=[chunk, chunk, chunk, chunk, whole],
        out_shape=[
            jax.ShapeDtypeStruct((b, REC_CHUNK, d), BF16),
            jax.ShapeDtypeStruct((b, REC_CHUNK, d), BF16),
            jax.ShapeDtypeStruct((b, REC_CHUNK, d), F32),
            jax.ShapeDtypeStruct((b, REC_CHUNK, d), F32),
            jax.ShapeDtypeStruct((b, n_meta, d), BF16),
        ],
        compiler_params=_params(1),
        name="rec_in_meta",
    )(hm, nw, w_in, lb_f, lb_b)


_SUM_LEVEL_HALVES = (32, 16, 8, 4, 2)
_N_SUM_LEVELS = len(_SUM_LEVEL_HALVES)
_MASK_ADJACENT = _N_SUM_LEVELS
_MASK_DIAG = _N_SUM_LEVELS + 1
_ROWS_Q_INTER = _N_SUM_LEVELS * REC_CHUNK
_ROWS_K_STATE = (_N_SUM_LEVELS + 1) * REC_CHUNK
_N_SUM_ROWS = (_N_SUM_LEVELS + 2) * REC_CHUNK
SCAN_CHUNKS_PER_STEP = 4
SCAN_PIPELINE_LAG = 8
SUBLANE_TILE = 8


def _scan_constants(backward):
    c = REC_CHUNK
    u = np.arange(c)
    w = (u[None, :] >= u[:, None]) if backward else (u[None, :] <= u[:, None])
    masks = np.zeros((_N_SUM_LEVELS + 2, c, c), np.float32)
    for lvl, half in enumerate(_SUM_LEVEL_HALVES + (1,)):
        for t in range(c):
            start = (t // (2 * half)) * 2 * half
            if t - start >= half:
                masks[lvl, t, start:start + half] = 1.0
    masks[_MASK_DIAG] = np.eye(c, dtype=np.float32)
    if backward:
        masks = masks[:, ::-1, ::-1]
    w = np.concatenate([w, w], axis=1).astype(np.float32)
    return jnp.asarray(w, BF16), jnp.asarray(np.ascontiguousarray(masks), F32)


def _split_decays(cum, backward):
    c, d = cum.shape
    tile = SUBLANE_TILE
    sub = lax.broadcasted_iota(jnp.int32, (tile, d), 0)
    sign4 = jnp.where((sub < 4) != backward, -1.0, 1.0).astype(F32)
    sign2 = jnp.where(((sub % 4) < 2) != backward, -1.0, 1.0).astype(F32)
    row = lambda r: cum[r:r + 1]
    tiles = [cum[t:t + tile] for t in range(0, c, tile)]
    out = []
    for half in _SUM_LEVEL_HALVES:
        for i, x in enumerate(tiles):
            t0 = i * tile
            if half >= tile:
                start = (t0 // (2 * half)) * 2 * half
                r = start + half if backward else start + half - 1
                after = (t0 - start >= half) != backward
                out.append(x - row(r) if after else row(r) - x)
            elif half == 4:
                r = t0 + 4 if backward else t0 + 3
                out.append((x - row(r)) * sign4)
            else:
                r1, r2 = (t0 + 2, t0 + 6) if backward else (t0 + 1, t0 + 5)
                out.append((x - jnp.where(sub < 4, row(r1), row(r2))) * sign2)
    out.extend(tiles)
    last = 0 if backward else c - 1
    out.extend(row(last) - x for x in tiles)
    return jnp.concatenate(out, axis=0)


class _ChunkJob:
    def __init__(self, d, q_ref, v_ref, g_ref, o_ref, rows, slot):
        self.d, self.q_ref, self.v_ref, self.g_ref, self.o_ref = d, q_ref, v_ref, g_ref, o_ref
        self.rows, self.slot = rows, slot


def _scan_kernel(qf_ref, vf_ref, gf_ref, qb_ref, vb_ref, gb_ref, qm_ref, vm_ref, gfm_ref, gbm_ref,
                 wf_ref, wb_ref, mf_ref, mb_ref, of_ref, ob_ref, ofm_ref, obm_ref,
                 st_ref, e_ref):
    c = REC_CHUNK
    n_heads = qf_ref.shape[-1] // HEAD_DIM
    n_chunks = qf_ref.shape[1] // c
    step = pl.program_id(1)
    w_refs = (wf_ref, wb_ref)
    m_refs = (mf_ref, mb_ref)
    whole_rows = (c - 1, 0)

    @pl.when(step == 0)
    def _():
        st_ref[...] = jnp.zeros(st_ref.shape, F32)

    def decay_stage(job):
        g2 = job.g_ref[0, job.rows, :]
        g_hi = g2.astype(BF16)
        g_lo = (g2 - g_hi.astype(F32)).astype(BF16)
        cum = _dot(w_refs[job.d][...], jnp.concatenate([g_hi, g_lo], axis=0))
        e_ref[job.slot, job.d] = jnp.exp2(_split_decays(cum, backward=job.d == 1))

    def pair_stage(job, hd):
        d, m_ref = job.d, m_refs[job.d]
        sl = slice(hd * HEAD_DIM, (hd + 1) * HEAD_DIM)
        e = lambda r0: e_ref[job.slot, d, r0:r0 + c, sl]
        q = job.q_ref[0, job.rows, sl].astype(F32)
        f = jnp.exp2(job.g_ref[0, job.rows, sl])
        k = 1.0 - f
        row_id = lax.broadcasted_iota(jnp.int32, (c, HEAD_DIM), 0)
        near = _dot_nt(jnp.concatenate([q, q * f], axis=0).astype(BF16), k.astype(BF16))
        pair = m_ref[_MASK_DIAG] * near[:c] + m_ref[_MASK_ADJACENT] * near[c:]
        for lvl, half in enumerate(_SUM_LEVEL_HALVES):
            is_query = ((row_id // half) % 2 == 1) != (d == 1)
            both = (jnp.where(is_query, q, k) * e(lvl * c)).astype(BF16)
            pair = pair + m_ref[lvl] * _dot_nt(both, both)
        q_in = (q * e(_ROWS_Q_INTER)).astype(BF16)
        k_out = (k * e(_ROWS_K_STATE)).astype(BF16)
        return pair.astype(BF16), q_in, k_out

    def output_stage(job, hd, pair, q_in, k_out):
        d = job.d
        sl = slice(hd * HEAD_DIM, (hd + 1) * HEAD_DIM)
        v = job.v_ref[0, job.rows, sl]
        state = st_ref[d, hd]
        o = _dot(pair, v) + _dot(q_in, state.astype(BF16))
        job.o_ref[0, job.rows, sl] = o.astype(BF16)
        row = _ROWS_Q_INTER + whole_rows[d]
        t0 = (row // SUBLANE_TILE) * SUBLANE_TILE
        decay_col = e_ref[job.slot, d, t0:t0 + SUBLANE_TILE, sl].T[:, row - t0:row - t0 + 1]
        st_ref[d, hd] = state * decay_col + _dot_tn(k_out, v)

    def run(jobs):
        pending = []
        for group in jobs:
            for job in group:
                decay_stage(job)
            for hd in range(n_heads):
                for job in group:
                    pending.append((job, hd) + pair_stage(job, hd))
                    if len(pending) > SCAN_PIPELINE_LAG:
                        output_stage(*pending.pop(0))
        for item in pending:
            output_stage(*item)

    whole = slice(0, c)

    @pl.when(step == 0)
    def _():
        run([[_ChunkJob(0, qm_ref, vm_ref, gfm_ref, ofm_ref, whole, 0)]])

    groups = []
    for i in range(n_chunks):
        fwd_rows = slice(i * c, (i + 1) * c)
        bwd_rows = slice((n_chunks - 1 - i) * c, (n_chunks - i) * c)
        groups.append([_ChunkJob(0, qf_ref, vf_ref, gf_ref, of_ref, fwd_rows, i),
                       _ChunkJob(1, qb_ref, vb_ref, gb_ref, ob_ref, bwd_rows, i)])
    run(groups)

    @pl.when(step == pl.num_programs(1) - 1)
    def _():
        run([[_ChunkJob(1, qm_ref, vm_ref, gbm_ref, obm_ref, whole, 0)]])


def _scan(q, v, gf, gb, qm, vm, gfm, gbm):
    b, s, d = q.shape
    n_heads = d // HEAD_DIM
    chunks_per_step = min(SCAN_CHUNKS_PER_STEP, s // REC_CHUNK)
    rows = chunks_per_step * REC_CHUNK
    n_steps = s // rows
    wf, mf = _scan_constants(False)
    wb, mb = _scan_constants(True)
    fwd = pl.BlockSpec((1, rows, d), lambda bi, j: (bi, j, 0))
    bwd = pl.BlockSpec((1, rows, d), lambda bi, j: (bi, n_steps - 1 - j, 0))
    meta = pl.BlockSpec((1, REC_CHUNK, d), lambda bi, j: (bi, 0, 0))
    return pl.pallas_call(
        _scan_kernel,
        grid=(b, n_steps),
        in_specs=[fwd, fwd, fwd, bwd, bwd, bwd, meta, meta, meta, meta,
                  _resident(wf.shape), _resident(wb.shape),
                  _resident(mf.shape), _resident(mb.shape)],
        out_specs=[fwd, bwd, meta, meta],
        out_shape=[jax.ShapeDtypeStruct((b, s, d), BF16)] * 2
        + [jax.ShapeDtypeStruct((b, REC_CHUNK, d), BF16)] * 2,
        scratch_shapes=[
            pltpu.VMEM((2, n_heads, HEAD_DIM, HEAD_DIM), F32),
            pltpu.VMEM((chunks_per_step, 2, _N_SUM_ROWS, d), F32),
        ],
        compiler_params=_params(2),
        name="gla_scan",
    )(q, v, gf, q, v, gb, qm, vm, gfm, gbm, wf, wb, mf, mb)


def _rope_tables(length):
    inv_freq = ROPE_THETA ** (-jnp.arange(0, HEAD_DIM, 2, dtype=F32) / HEAD_DIM)
    ang = jnp.arange(length, dtype=F32)[:, None] * inv_freq[None, :]
    cos, sin = jnp.cos(ang), jnp.sin(ang)
    return jnp.concatenate([cos, cos], axis=-1), jnp.concatenate([-sin, sin], axis=-1)


def kernel(x, meta_tokens, mix_norm_w, mlp_norm_w, attn_w_in, attn_w_out, attn_q_norm_w,
           attn_k_norm_w, attn_sink, rec_w_in, rec_w_out, rec_lb_fwd, rec_lb_bwd,
           rec_out_norm_w, mlp_w_up, mlp_w_down):
    b, s, d = x.shape
    depth = mix_norm_w.shape[0]
    n_mixers = 2
    tm = min(TOKEN_TILE, s)
    n_meta_rows = b * N_META

    h = x
    hm = jnp.broadcast_to(meta_tokens.astype(x.dtype)[None], (b, N_META, d))
    cc, ss = _rope_tables(N_META + s)
    cc_meta, ss_meta = jnp.tile(cc[:N_META], (b, 1)), jnp.tile(ss[:N_META], (b, 1))
    cc_real, ss_real = cc[N_META:], ss[N_META:]
    row = lambda w: w.reshape(1, -1)
    flat_meta = lambda t: t.reshape(1, n_meta_rows, t.shape[-1])
    per_batch = lambda t: t.reshape(b, N_META, t.shape[-1])

    for layer in range(depth):
        j = layer // n_mixers
        nw = row(mix_norm_w[layer])
        mlp = (row(mlp_norm_w[layer]), mlp_w_up[layer].astype(BF16),
               mlp_w_down[layer].astype(BF16))
        if layer % n_mixers == 0:
            w_in = attn_w_in[j].astype(BF16)
            w_out = attn_w_out[j].astype(BF16)
            gains = (row(attn_q_norm_w[j]), row(attn_k_norm_w[j]))
            q, kv = _attn_in(h, nw, w_in, *gains, cc_real, ss_real, tm)
            qm, kvm = map(per_batch, _attn_in(
                flat_meta(hm), nw, w_in, *gains, cc_meta, ss_meta, n_meta_rows))
            sink = attn_sink[j].astype(F32)
            o = _attn(q, kv, kvm, sink)
            om = _attn_meta(qm, kvm, kv, sink)
            h = _post(h, o, w_out, *mlp, tm)
            hm = per_batch(_post(flat_meta(hm), flat_meta(om), w_out, *mlp, n_meta_rows))
        else:
            w_in = rec_w_in[j].astype(BF16)
            w_out = rec_w_out[j].astype(BF16)
            gain = row(rec_out_norm_w[j])
            q, v, gf, gb, gate = _rec_in(h, nw, w_in, rec_lb_fwd, rec_lb_bwd, layer,
                                         min(REC_IN_TOKEN_TILE, s))
            qm, vm, gfm, gbm, gate_m = _rec_in_meta(hm, nw, w_in, rec_lb_fwd, rec_lb_bwd, layer)
            o_f, o_b, o_fm, o_bm = _scan(q, v, gf, gb, qm, vm, gfm, gbm)
            h = _post(h, (o_f, o_b, gate), w_out, *mlp, tm, rec_extra=gain)
            meta_rows = lambda t: flat_meta(t[:, REC_CHUNK - N_META:])
            hm = per_batch(_post(flat_meta(hm), (meta_rows(o_fm), meta_rows(o_bm),
                                                 flat_meta(gate_m)),
                                 w_out, *mlp, n_meta_rows, rec_extra=gain))
    return h
```

```python
import functools

import numpy as np
import jax
import jax.numpy as jnp
from jax import lax
from jax.experimental import pallas as pl
from jax.experimental.pallas import tpu as pltpu

N_META = 16
HEAD_DIM = 128
N_KV_HEADS = 2
GROUP = 4
WINDOW = 128
ATTN_BLOCK = 128
ROPE_THETA = 10000.0
REC_CHUNK = 64
EPS = 1e-6

F32 = jnp.float32
BF16 = jnp.bfloat16

V7X_VMEM_BYTES = 64 * 1024 * 1024
V7X_VMEM_LIMIT_BYTES = V7X_VMEM_BYTES * 7 // 8
TOKEN_TILE = 512
FF_CHUNK = 1024
SUB_TILE = 256
ATTN_BLOCKS_PER_STEP = 4


def _params(n_axes):
    return pltpu.CompilerParams(
        dimension_semantics=("arbitrary",) * n_axes,
        vmem_limit_bytes=V7X_VMEM_LIMIT_BYTES,
    )


def _resident(shape):
    zeros = (0,) * len(shape)
    return pl.BlockSpec(shape, lambda *_: zeros, pipeline_mode=pl.Buffered(1))


def _rms(x, w):
    ms = jnp.mean(x * x, axis=-1, keepdims=True)
    return x * lax.rsqrt(ms + EPS) * w


def _dot(a, b):
    return jnp.dot(a, b, preferred_element_type=F32)


def _dot_nt(a, b):
    return lax.dot_general(a, b, (((1,), (1,)), ((), ())), preferred_element_type=F32)


def _dot_tn(a, b):
    return lax.dot_general(a, b, (((0,), (0,)), ((), ())), preferred_element_type=F32)


def _attn_in_kernel(h_ref, nw_ref, w_ref, qg_ref, kg_ref, cc_ref, ss_ref, q_ref, kv_ref):
    n_q = q_ref.shape[-1] // HEAD_DIM
    n_kv = kv_ref.shape[-1] // (2 * HEAD_DIM)
    tm = h_ref.shape[1]
    sub = min(tm, SUB_TILE)
    half = HEAD_DIM // 2
    q_gain = qg_ref[...] * (HEAD_DIM ** -0.5)
    k_gain = kg_ref[...]

    for r in range(tm // sub):
        rows = slice(r * sub, (r + 1) * sub)
        hn = _rms(h_ref[0, rows, :], nw_ref[...]).astype(BF16)
        qkv = _dot(hn, w_ref[...])
        cc = cc_ref[rows, :]
        ss = ss_ref[rows, :]
        tabs_q = (q_gain * cc, pltpu.roll(q_gain, half, 1) * ss)
        tabs_k = (k_gain * cc, pltpu.roll(k_gain, half, 1) * ss)

        def norm_rope(x, tabs):
            y = x * lax.rsqrt(jnp.mean(x * x, axis=-1, keepdims=True) + EPS)
            return y * tabs[0] + pltpu.roll(y, half, 1) * tabs[1]

        for j in range(n_q):
            sl = slice(j * HEAD_DIM, (j + 1) * HEAD_DIM)
            q_ref[0, rows, sl] = norm_rope(qkv[:, sl], tabs_q).astype(BF16)
        for j in range(n_kv):
            sl = slice(j * HEAD_DIM, (j + 1) * HEAD_DIM)
            src = slice((n_q + j) * HEAD_DIM, (n_q + j + 1) * HEAD_DIM)
            kv_ref[0, rows, sl] = norm_rope(qkv[:, src], tabs_k).astype(BF16)
        kv_ref[0, rows, n_kv * HEAD_DIM:] = qkv[:, (n_q + n_kv) * HEAD_DIM:].astype(BF16)


def _attn_in(h, nw, w_in, q_gain, k_gain, cc, ss, tm):
    bx, sx, d = h.shape
    n_in = w_in.shape[1]
    dq = d
    dkv = (n_in - dq) // 2
    grid = (bx, sx // tm)
    tok = lambda width: pl.BlockSpec((1, tm, width), lambda b, i: (b, i, 0))
    return pl.pallas_call(
        _attn_in_kernel,
        grid=grid,
        in_specs=[
            tok(d),
            _resident((1, d)),
            _resident((d, n_in)),
            _resident((1, HEAD_DIM)),
            _resident((1, HEAD_DIM)),
            pl.BlockSpec((tm, HEAD_DIM), lambda b, i: (i, 0)),
            pl.BlockSpec((tm, HEAD_DIM), lambda b, i: (i, 0)),
        ],
        out_specs=[tok(dq), tok(2 * dkv)],
        out_shape=[
            jax.ShapeDtypeStruct((bx, sx, dq), BF16),
            jax.ShapeDtypeStruct((bx, sx, 2 * dkv), BF16),
        ],
        compiler_params=_params(2),
        name="attn_in",
    )(h, nw, w_in, q_gain, k_gain, cc, ss)


def _softmax_pv(scores, values, sink_col):
    m = sink_col
    for s in scores:
        m = jnp.maximum(m, jnp.max(s, axis=-1, keepdims=True))
    den = jnp.exp(sink_col - m)
    acc = None
    for s, v in zip(scores, values):
        p = jnp.exp(s - m)
        den = den + jnp.sum(p, axis=-1, keepdims=True)
        pv = _dot(p.astype(BF16), v)
        acc = pv if acc is None else acc + pv
    return acc / den


def _attn_kernel(sink_ref, q_ref, kvp_ref, kvc_ref, kvn_ref, kvm_ref, o_ref, *, n_steps):
    step = pl.program_id(1)
    blk = ATTN_BLOCK
    n_blk = q_ref.shape[1] // blk
    v_off = N_KV_HEADS * HEAD_DIM
    a = lax.broadcasted_iota(jnp.int32, (GROUP * blk, 3 * blk), 0) % blk
    c = lax.broadcasted_iota(jnp.int32, (GROUP * blk, 3 * blk), 1)
    in_prev = (c < blk) & (c >= a)
    in_own = (c >= blk) & (c < 2 * blk)
    in_next = (c >= 2 * blk) & (c - 2 * blk <= a)
    interior = in_prev | in_own | in_next
    neg_inf = jnp.float32(-jnp.inf)
    kv_blocks = ([kvp_ref.at[0]] + [kvc_ref.at[0, pl.ds(i * blk, blk), :] for i in range(n_blk)]
                 + [kvn_ref.at[0]])
    for i in range(n_blk):
        rows = slice(i * blk, (i + 1) * blk)
        kv_refs = kv_blocks[i:i + 3]
        mask = interior
        if i == 0:
            mask = (in_prev & (step > 0)) | in_own | in_next
        if i == n_blk - 1:
            mask = mask & (in_prev | in_own | (step < n_steps - 1))
        for hk in range(N_KV_HEADS):
            ks = slice(hk * HEAD_DIM, (hk + 1) * HEAD_DIM)
            vs = slice(v_off + hk * HEAD_DIM, v_off + (hk + 1) * HEAD_DIM)
            keys = jnp.concatenate([r[:, ks] for r in kv_refs], axis=0)
            vals = jnp.concatenate([r[:, vs] for r in kv_refs], axis=0)
            heads = [hk * GROUP + g for g in range(GROUP)]
            q4 = jnp.concatenate(
                [q_ref[0, rows, h * HEAD_DIM:(h + 1) * HEAD_DIM] for h in heads], axis=0)
            s_band = jnp.where(mask, _dot_nt(q4, keys), neg_inf)
            s_meta = _dot_nt(q4, kvm_ref[0, :, ks])
            sink_col = jnp.concatenate(
                [jnp.full((blk, 1), sink_ref[h], F32) for h in heads], axis=0)
            o4 = _softmax_pv([s_band, s_meta], [vals, kvm_ref[0, :, vs]], sink_col)
            for g, h in enumerate(heads):
                o_ref[0, rows, h * HEAD_DIM:(h + 1) * HEAD_DIM] = (
                    o4[g * blk:(g + 1) * blk].astype(BF16))


def _attn(q, kv, kvm, sink):
    b, s, dq = q.shape
    dkv2 = kv.shape[-1]
    blk = ATTN_BLOCK
    n_blk = min(ATTN_BLOCKS_PER_STEP, s // blk)
    n_steps = s // (n_blk * blk)
    last_blk = s // blk - 1
    prev = pl.BlockSpec((1, blk, dkv2), lambda bi, n: (bi, jnp.maximum(n_blk * n - 1, 0), 0))
    own = pl.BlockSpec((1, n_blk * blk, dkv2), lambda bi, n: (bi, n, 0))
    nxt = pl.BlockSpec((1, blk, dkv2),
                       lambda bi, n: (bi, jnp.minimum(n_blk * (n + 1), last_blk), 0))
    meta = pl.BlockSpec((1, N_META, dkv2), lambda bi, n: (bi, 0, 0))
    return pl.pallas_call(
        functools.partial(_attn_kernel, n_steps=n_steps),
        grid=(b, n_steps),
        in_specs=[
            pl.BlockSpec(memory_space=pltpu.SMEM),
            pl.BlockSpec((1, n_blk * blk, dq), lambda bi, n: (bi, n, 0)),
            prev, own, nxt, meta,
        ],
        out_specs=pl.BlockSpec((1, n_blk * blk, dq), lambda bi, n: (bi, n, 0)),
        out_shape=jax.ShapeDtypeStruct((b, s, dq), BF16),
        compiler_params=_params(2),
        name="band_attn",
    )(sink, q, kv, kv, kv, kvm)


def _attn_meta_kernel(sink_ref, qm_ref, kvm_ref, kvr_ref, o_ref):
    rows = GROUP * N_META
    v_off = N_KV_HEADS * HEAD_DIM
    i = lax.broadcasted_iota(jnp.int32, (rows, ATTN_BLOCK), 0) % N_META
    j = lax.broadcasted_iota(jnp.int32, (rows, ATTN_BLOCK), 1)
    allowed = (N_META + j) - i <= WINDOW
    neg_inf = jnp.float32(-jnp.inf)
    for hk in range(N_KV_HEADS):
        ks = slice(hk * HEAD_DIM, (hk + 1) * HEAD_DIM)
        vs = slice(v_off + hk * HEAD_DIM, v_off + (hk + 1) * HEAD_DIM)
        heads = [hk * GROUP + g for g in range(GROUP)]
        q4 = jnp.concatenate(
            [qm_ref[0, :, h * HEAD_DIM:(h + 1) * HEAD_DIM] for h in heads], axis=0)
        s_meta = _dot_nt(q4, kvm_ref[0, :, ks])
        s_real = jnp.where(allowed, _dot_nt(q4, kvr_ref[0, :, ks]), neg_inf)
        sink_col = jnp.concatenate(
            [jnp.full((N_META, 1), sink_ref[h], F32) for h in heads], axis=0)
        o4 = _softmax_pv([s_meta, s_real], [kvm_ref[0, :, vs], kvr_ref[0, :, vs]], sink_col)
        for g, h in enumerate(heads):
            o_ref[0, :, h * HEAD_DIM:(h + 1) * HEAD_DIM] = (
                o4[g * N_META:(g + 1) * N_META].astype(BF16))


def _attn_meta(qm, kvm, kv, sink):
    b, _, dq = qm.shape
    dkv2 = kvm.shape[-1]
    meta = lambda width: pl.BlockSpec((1, N_META, width), lambda bi: (bi, 0, 0))
    first = pl.BlockSpec((1, ATTN_BLOCK, dkv2), lambda bi: (bi, 0, 0))
    return pl.pallas_call(
        _attn_meta_kernel,
        grid=(b,),
        in_specs=[pl.BlockSpec(memory_space=pltpu.SMEM), meta(dq), meta(dkv2), first],
        out_specs=meta(dq),
        out_shape=jax.ShapeDtypeStruct((b, N_META, dq), BF16),
        compiler_params=_params(1),
        name="meta_attn",
    )(sink, qm, kvm, kv)


def _post_kernel(*refs, rec, ff_chunk):
    if rec:
        h_ref, of_ref, ob_ref, gate_ref, og_ref, wo_ref, nw_ref, wu_ref, wd_ref, out_ref = refs
        o = of_ref[0].astype(F32) + ob_ref[0].astype(F32)
        n_heads = o.shape[-1] // HEAD_DIM
        o = jnp.concatenate(
            [_rms(o[:, j * HEAD_DIM:(j + 1) * HEAD_DIM], og_ref[...]) for j in range(n_heads)],
            axis=1)
        o = (o * gate_ref[0].astype(F32)).astype(BF16)
    else:
        h_ref, o_ref, wo_ref, nw_ref, wu_ref, wd_ref, out_ref = refs
        o = o_ref[0]
    h1 = h_ref[0] + _dot(o, wo_ref[...])
    hn = _rms(h1, nw_ref[...]).astype(BF16)
    d_ff = wu_ref.shape[1]
    acc = h1
    for c in range(d_ff // ff_chunk):
        sl = slice(c * ff_chunk, (c + 1) * ff_chunk)
        z = jnp.maximum(_dot(hn, wu_ref[:, sl]), 0.0)
        acc = acc + _dot((z * z).astype(BF16), wd_ref[sl, :])
    out_ref[0] = acc


def _post(h, mixer_out, w_out, nw, w_up, w_down, tm, rec_extra=None):
    bx, sx, d = h.shape
    d_ff = w_up.shape[1]
    tok = pl.BlockSpec((1, tm, d), lambda b, i: (b, i, 0))
    rec = rec_extra is not None
    if rec:
        ins = [h, *mixer_out, rec_extra, w_out, nw, w_up, w_down]
        specs = [tok, tok, tok, tok, _resident((1, HEAD_DIM))]
    else:
        ins = [h, mixer_out, w_out, nw, w_up, w_down]
        specs = [tok, tok]
    specs += [_resident((d, d)), _resident((1, d)), _resident((d, d_ff)), _resident((d_ff, d))]
    return pl.pallas_call(
        functools.partial(_post_kernel, rec=rec, ff_chunk=min(FF_CHUNK, d_ff)),
        grid=(bx, sx // tm),
        in_specs=specs,
        out_specs=tok,
        out_shape=jax.ShapeDtypeStruct((bx, sx, d), F32),
        compiler_params=_params(2),
        name="post_rec" if rec else "post_attn",
    )(*ins)


def _lower_bound(tab_ref, layer):
    t = tab_ref[...]
    e = jnp.exp(t - jnp.max(t, axis=0, keepdims=True))
    p = e / jnp.sum(e, axis=0, keepdims=True)
    return jnp.sum(p[1:layer + 1], axis=0, keepdims=True)


def _rec_in_rows(h, nw_ref, w_ref, lbf_ref, lbb_ref, layer):
    d = h.shape[-1]
    hn = _rms(h, nw_ref[...]).astype(BF16)
    proj = lambda j: _dot(hn, w_ref[:, j * d:(j + 1) * d])

    def log_forget(z, tab_ref):
        lb = _lower_bound(tab_ref, layer)
        return jnp.log2(lb + (1.0 - lb) * jax.nn.sigmoid(z))

    gf = log_forget(proj(2), lbf_ref)
    gb = log_forget(proj(3), lbb_ref)
    zq = proj(0)
    q = zq * jax.nn.sigmoid(zq)
    zg = proj(4)
    gate = zg * jax.nn.sigmoid(zg)
    v = proj(1)
    return q, v, gf, gb, gate


def _rec_in_kernel(h_ref, nw_ref, w_ref, lbf_ref, lbb_ref,
                   q_ref, v_ref, gf_ref, gb_ref, gate_ref, *, layer):
    tm = h_ref.shape[1]
    sub = min(tm, SUB_TILE)
    for r in range(tm // sub):
        rows = slice(r * sub, (r + 1) * sub)
        q, v, gf, gb, gate = _rec_in_rows(h_ref[0, rows, :], nw_ref, w_ref, lbf_ref, lbb_ref,
                                          layer)
        q_ref[0, rows, :] = q.astype(BF16)
        v_ref[0, rows, :] = v.astype(BF16)
        gf_ref[0, rows, :] = gf
        gb_ref[0, rows, :] = gb
        gate_ref[0, rows, :] = gate.astype(BF16)


def _rec_in(h, nw, w_in, lb_f, lb_b, layer, tm):
    b, s, d = h.shape
    depth = lb_f.shape[0]
    tok = pl.BlockSpec((1, tm, d), lambda bi, i: (bi, i, 0))
    return pl.pallas_call(
        functools.partial(_rec_in_kernel, layer=layer),
        grid=(b, s // tm),
        in_specs=[tok, _resident((1, d)), _resident((d, w_in.shape[1])),
                  _resident((depth, d)), _resident((depth, d))],
        out_specs=[tok, tok, tok, tok, tok],
        out_shape=[
            jax.ShapeDtypeStruct((b, s, d), BF16),
            jax.ShapeDtypeStruct((b, s, d), BF16),
            jax.ShapeDtypeStruct((b, s, d), F32),
            jax.ShapeDtypeStruct((b, s, d), F32),
            jax.ShapeDtypeStruct((b, s, d), BF16),
        ],
        compiler_params=_params(2),
        name="rec_in",
    )(h, nw, w_in, lb_f, lb_b)


def _rec_in_meta_kernel(hm_ref, nw_ref, w_ref, lbf_ref, lbb_ref,
                        q_ref, v_ref, gf_ref, gb_ref, gate_ref, *, layer):
    b, n_meta, d = hm_ref.shape
    h = hm_ref[...].reshape(b * n_meta, d)
    q, v, gf, gb, gate = _rec_in_rows(h, nw_ref, w_ref, lbf_ref, lbb_ref, layer)
    pad = REC_CHUNK - n_meta
    for ref, val in ((q_ref, q), (v_ref, v), (gf_ref, gf), (gb_ref, gb)):
        ref[:, :pad, :] = jnp.zeros((b, pad, d), ref.dtype)
        ref[:, pad:, :] = val.reshape(b, n_meta, d).astype(ref.dtype)
    gate_ref[...] = gate.reshape(b, n_meta, d).astype(BF16)


def _rec_in_meta(hm, nw, w_in, lb_f, lb_b, layer):
    b, n_meta, d = hm.shape
    depth = lb_f.shape[0]
    chunk = pl.BlockSpec((b, REC_CHUNK, d), lambda i: (0, 0, 0))
    whole = pl.BlockSpec((b, n_meta, d), lambda i: (0, 0, 0))
    return pl.pallas_call(
        functools.partial(_rec_in_meta_kernel, layer=layer),
        grid=(1,),
        in_specs=[whole, _resident((1, d)), _resident((d, w_in.shape[1])),
                  _resident((depth, d)), _resident((depth, d))],
        out_specs=[chunk, chunk, chunk, chunk, whole],
        out_shape=[
            jax.ShapeDtypeStruct((b, REC_CHUNK, d), BF16),
            jax.ShapeDtypeStruct((b, REC_CHUNK, d), BF16),
            jax.ShapeDtypeStruct((b, REC_CHUNK, d), F32),
            jax.ShapeDtypeStruct((b, REC_CHUNK, d), F32),
            jax.ShapeDtypeStruct((b, n_meta, d), BF16),
        ],
        compiler_params=_params(1),
        name="rec_in_meta",
    )(hm, nw, w_in, lb_f, lb_b)


_SUM_LEVEL_HALVES = (32, 16, 8, 4, 2)
_N_SUM_LEVELS = len(_SUM_LEVEL_HALVES)
_MASK_ADJACENT = _N_SUM_LEVELS
_MASK_DIAG = _N_SUM_LEVELS + 1
_ROWS_Q_INTER = _N_SUM_LEVELS * REC_CHUNK
_ROWS_K_STATE = (_N_SUM_LEVELS + 1) * REC_CHUNK
_N_SUM_ROWS = (_N_SUM_LEVELS + 2) * REC_CHUNK
SCAN_CHUNKS_PER_STEP = 4
SCAN_PIPELINE_LAG = 8
SUBLANE_TILE = 8


def _scan_constants(backward):
    c = REC_CHUNK
    u = np.arange(c)
    w = (u[None, :] >= u[:, None]) if backward else (u[None, :] <= u[:, None])
    masks = np.zeros((_N_SUM_LEVELS + 2, c, c), np.float32)
    for lvl, half in enumerate(_SUM_LEVEL_HALVES + (1,)):
        for t in range(c):
            start = (t // (2 * half)) * 2 * half
            if t - start >= half:
                masks[lvl, t, start:start + half] = 1.0
    masks[_MASK_DIAG] = np.eye(c, dtype=np.float32)
    if backward:
        masks = masks[:, ::-1, ::-1]
    w = np.concatenate([w, w], axis=1).astype(np.float32)
    return jnp.asarray(w, BF16), jnp.asarray(np.ascontiguousarray(masks), F32)


def _split_decays(cum, backward):
    c, d = cum.shape
    tile = SUBLANE_TILE
    sub = lax.broadcasted_iota(jnp.int32, (tile, d), 0)
    sign4 = jnp.where((sub < 4) != backward, -1.0, 1.0).astype(F32)
    sign2 = jnp.where(((sub % 4) < 2) != backward, -1.0, 1.0).astype(F32)
    row = lambda r: cum[r:r + 1]
    tiles = [cum[t:t + tile] for t in range(0, c, tile)]
    out = []
    for half in _SUM_LEVEL_HALVES:
        for i, x in enumerate(tiles):
            t0 = i * tile
            if half >= tile:
                start = (t0 // (2 * half)) * 2 * half
                r = start + half if backward else start + half - 1
                after = (t0 - start >= half) != backward
                out.append(x - row(r) if after else row(r) - x)
            elif half == 4:
                r = t0 + 4 if backward else t0 + 3
                out.append((x - row(r)) * sign4)
            else:
                r1, r2 = (t0 + 2, t0 + 6) if backward else (t0 + 1, t0 + 5)
                out.append((x - jnp.where(sub < 4, row(r1), row(r2))) * sign2)
    out.extend(tiles)
    last = 0 if backward else c - 1
    out.extend(row(last) - x for x in tiles)
    return jnp.concatenate(out, axis=0)


class _ChunkJob:
    def __init__(self, d, q_ref, v_ref, g_ref, o_ref, rows, slot):
        self.d, self.q_ref, self.v_ref, self.g_ref, self.o_ref = d, q_ref, v_ref, g_ref, o_ref
        self.rows, self.slot = rows, slot


def _scan_kernel(qf_ref, vf_ref, gf_ref, qb_ref, vb_ref, gb_ref, qm_ref, vm_ref, gfm_ref, gbm_ref,
                 wf_ref, wb_ref, mf_ref, mb_ref, of_ref, ob_ref, ofm_ref, obm_ref,
                 st_ref, e_ref):
    c = REC_CHUNK
    n_heads = qf_ref.shape[-1] // HEAD_DIM
    n_chunks = qf_ref.shape[1] // c
    step = pl.program_id(1)
    w_refs = (wf_ref, wb_ref)
    m_refs = (mf_ref, mb_ref)
    whole_rows = (c - 1, 0)

    @pl.when(step == 0)
    def _():
        st_ref[...] = jnp.zeros(st_ref.shape, F32)

    def decay_stage(job):
        g2 = job.g_ref[0, job.rows, :]
        g_hi = g2.astype(BF16)
        g_lo = (g2 - g_hi.astype(F32)).astype(BF16)
        cum = _dot(w_refs[job.d][...], jnp.concatenate([g_hi, g_lo], axis=0))
        e_ref[job.slot, job.d] = jnp.exp2(_split_decays(cum, backward=job.d == 1))

    def pair_stage(job, hd):
        d, m_ref = job.d, m_refs[job.d]
        sl = slice(hd * HEAD_DIM, (hd + 1) * HEAD_DIM)
        e = lambda r0: e_ref[job.slot, d, r0:r0 + c, sl]
        q = job.q_ref[0, job.rows, sl].astype(F32)
        f = jnp.exp2(job.g_ref[0, job.rows, sl])
        k = 1.0 - f
        row_id = lax.broadcasted_iota(jnp.int32, (c, HEAD_DIM), 0)
        near = _dot_nt(jnp.concatenate([q, q * f], axis=0).astype(BF16), k.astype(BF16))
        pair = m_ref[_MASK_DIAG] * near[:c] + m_ref[_MASK_ADJACENT] * near[c:]
        for lvl, half in enumerate(_SUM_LEVEL_HALVES):
            is_query = ((row_id // half) % 2 == 1) != (d == 1)
            both = (jnp.where(is_query, q, k) * e(lvl * c)).astype(BF16)
            pair = pair + m_ref[lvl] * _dot_nt(both, both)
        q_in = (q * e(_ROWS_Q_INTER)).astype(BF16)
        k_out = (k * e(_ROWS_K_STATE)).astype(BF16)
        return pair.astype(BF16), q_in, k_out

    def output_stage(job, hd, pair, q_in, k_out):
        d = job.d
        sl = slice(hd * HEAD_DIM, (hd + 1) * HEAD_DIM)
        v = job.v_ref[0, job.rows, sl]
        state = st_ref[d, hd]
        o = _dot(pair, v) + _dot(q_in, state.astype(BF16))
        job.o_ref[0, job.rows, sl] = o.astype(BF16)
        row = _ROWS_Q_INTER + whole_rows[d]
        t0 = (row // SUBLANE_TILE) * SUBLANE_TILE
        decay_col = e_ref[job.slot, d, t0:t0 + SUBLANE_TILE, sl].T[:, row - t0:row - t0 + 1]
        st_ref[d, hd] = state * decay_col + _dot_tn(k_out, v)

    def run(jobs):
        pending = []
        for group in jobs:
            for job in group:
                decay_stage(job)
            for hd in range(n_heads):
                for job in group:
                    pending.append((job, hd) + pair_stage(job, hd))
                    if len(pending) > SCAN_PIPELINE_LAG:
                        output_stage(*pending.pop(0))
        for item in pending:
            output_stage(*item)

    whole = slice(0, c)

    @pl.when(step == 0)
    def _():
        run([[_ChunkJob(0, qm_ref, vm_ref, gfm_ref, ofm_ref, whole, 0)]])

    groups = []
    for i in range(n_chunks):
        fwd_rows = slice(i * c, (i + 1) * c)
        bwd_rows = slice((n_chunks - 1 - i) * c, (n_chunks - i) * c)
        groups.append([_ChunkJob(0, qf_ref, vf_ref, gf_ref, of_ref, fwd_rows, i),
                       _ChunkJob(1, qb_ref, vb_ref, gb_ref, ob_ref, bwd_rows, i)])
    run(groups)

    @pl.when(step == pl.num_programs(1) - 1)
    def _():
        run([[_ChunkJob(1, qm_ref, vm_ref, gbm_ref, obm_ref, whole, 0)]])


def _scan(q, v, gf, gb, qm, vm, gfm, gbm):
    b, s, d = q.shape
    n_heads = d // HEAD_DIM
    chunks_per_step = min(SCAN_CHUNKS_PER_STEP, s // REC_CHUNK)
    rows = chunks_per_step * REC_CHUNK
    n_steps = s // rows
    wf, mf = _scan_constants(False)
    wb, mb = _scan_constants(True)
    fwd = pl.BlockSpec((1, rows, d), lambda bi, j: (bi, j, 0))
    bwd = pl.BlockSpec((1, rows, d), lambda bi, j: (bi, n_steps - 1 - j, 0))
    meta = pl.BlockSpec((1, REC_CHUNK, d), lambda bi, j: (bi, 0, 0))
    return pl.pallas_call(
        _scan_kernel,
        grid=(b, n_steps),
        in_specs=[fwd, fwd, fwd, bwd, bwd, bwd, meta, meta, meta, meta,
                  _resident(wf.shape), _resident(wb.shape),
                  _resident(mf.shape), _resident(mb.shape)],
        out_specs=[fwd, bwd, meta, meta],
        out_shape=[jax.ShapeDtypeStruct((b, s, d), BF16)] * 2
        + [jax.ShapeDtypeStruct((b, REC_CHUNK, d), BF16)] * 2,
        scratch_shapes=[
            pltpu.VMEM((2, n_heads, HEAD_DIM, HEAD_DIM), F32),
            pltpu.VMEM((chunks_per_step, 2, _N_SUM_ROWS, d), F32),
        ],
        compiler_params=_params(2),
        name="gla_scan",
    )(q, v, gf, q, v, gb, qm, vm, gfm, gbm, wf, wb, mf, mb)


def _rope_tables(length):
    inv_freq = ROPE_THETA ** (-jnp.arange(0, HEAD_DIM, 2, dtype=F32) / HEAD_DIM)
    ang = jnp.arange(length, dtype=F32)[:, None] * inv_freq[None, :]
    cos, sin = jnp.cos(ang), jnp.sin(ang)
    return jnp.concatenate([cos, cos], axis=-1), jnp.concatenate([-sin, sin], axis=-1)


def kernel(x, meta_tokens, mix_norm_w, mlp_norm_w, attn_w_in, attn_w_out, attn_q_norm_w,
           attn_k_norm_w, attn_sink, rec_w_in, rec_w_out, rec_lb_fwd, rec_lb_bwd,
           rec_out_norm_w, mlp_w_up, mlp_w_down):
    b, s, d = x.shape
    depth = mix_norm_w.shape[0]
    n_mixers = 2
    tm = min(TOKEN_TILE, s)
    n_meta_rows = b * N_META
    assert d % HEAD_DIM == 0 and d // HEAD_DIM == N_KV_HEADS * GROUP, x.shape
    assert s % tm == 0 and s % ATTN_BLOCK == 0 and s % REC_CHUNK == 0, x.shape
    assert meta_tokens.shape == (N_META, d), meta_tokens.shape

    h = x
    hm = jnp.broadcast_to(meta_tokens.astype(x.dtype)[None], (b, N_META, d))
    cc, ss = _rope_tables(N_META + s)
    cc_meta, ss_meta = jnp.tile(cc[:N_META], (b, 1)), jnp.tile(ss[:N_META], (b, 1))
    cc_real, ss_real = cc[N_META:], ss[N_META:]
    row = lambda w: w.reshape(1, -1)
    flat_meta = lambda t: t.reshape(1, n_meta_rows, t.shape[-1])
    per_batch = lambda t: t.reshape(b, N_META, t.shape[-1])

    for layer in range(depth):
        j = layer // n_mixers
        nw = row(mix_norm_w[layer])
        mlp = (row(mlp_norm_w[layer]), mlp_w_up[layer].astype(BF16),
               mlp_w_down[layer].astype(BF16))
        if layer % n_mixers == 0:
            w_in = attn_w_in[j].astype(BF16)
            w_out = attn_w_out[j].astype(BF16)
            gains = (row(attn_q_norm_w[j]), row(attn_k_norm_w[j]))
            q, kv = _attn_in(h, nw, w_in, *gains, cc_real, ss_real, tm)
            qm, kvm = map(per_batch, _attn_in(
                flat_meta(hm), nw, w_in, *gains, cc_meta, ss_meta, n_meta_rows))
            sink = attn_sink[j].astype(F32)
            o = _attn(q, kv, kvm, sink)
            om = _attn_meta(qm, kvm, kv, sink)
            h = _post(h, o, w_out, *mlp, tm)
            hm = per_batch(_post(flat_meta(hm), flat_meta(om), w_out, *mlp, n_meta_rows))
        else:
            w_in = rec_w_in[j].astype(BF16)
            w_out = rec_w_out[j].astype(BF16)
            gain = row(rec_out_norm_w[j])
            q, v, gf, gb, gate = _rec_in(h, nw, w_in, rec_lb_fwd, rec_lb_bwd, layer, tm)
            qm, vm, gfm, gbm, gate_m = _rec_in_meta(hm, nw, w_in, rec_lb_fwd, rec_lb_bwd, layer)
            o_f, o_b, o_fm, o_bm = _scan(q, v, gf, gb, qm, vm, gfm, gbm)
            h = _post(h, (o_f, o_b, gate), w_out, *mlp, tm, rec_extra=gain)
            meta_rows = lambda t: flat_meta(t[:, REC_CHUNK - N_META:])
            hm = per_batch(_post(flat_meta(hm), (meta_rows(o_fm), meta_rows(o_bm),
                                                 flat_meta(gate_m)),
                                 w_out, *mlp, n_meta_rows, rec_extra=gain))
    return h
```

```python
import functools

import numpy as np
import jax
import jax.numpy as jnp
from jax import lax
from jax.experimental import pallas as pl
from jax.experimental.pallas import tpu as pltpu

N_META = 16
HEAD_DIM = 128
N_KV_HEADS = 2
GROUP = 4
WINDOW = 128
ATTN_BLOCK = 128
ROPE_THETA = 10000.0
REC_CHUNK = 64
EPS = 1e-6

F32 = jnp.float32
BF16 = jnp.bfloat16

V7X_VMEM_BYTES = 64 * 1024 * 1024
V7X_VMEM_LIMIT_BYTES = V7X_VMEM_BYTES * 7 // 8
TOKEN_TILE = 512
FF_CHUNK = 1024
SUB_TILE = 256
ATTN_BLOCKS_PER_STEP = 4


def _params(n_axes):
    return pltpu.CompilerParams(
        dimension_semantics=("arbitrary",) * n_axes,
        vmem_limit_bytes=V7X_VMEM_LIMIT_BYTES,
    )


def _resident(shape):
    zeros = (0,) * len(shape)
    return pl.BlockSpec(shape, lambda *_: zeros, pipeline_mode=pl.Buffered(1))


def _rms(x, w):
    ms = jnp.mean(x * x, axis=-1, keepdims=True)
    return x * lax.rsqrt(ms + EPS) * w


def _dot(a, b):
    return jnp.dot(a, b, preferred_element_type=F32)


def _dot_nt(a, b):
    return lax.dot_general(a, b, (((1,), (1,)), ((), ())), preferred_element_type=F32)


def _dot_tn(a, b):
    return lax.dot_general(a, b, (((0,), (0,)), ((), ())), preferred_element_type=F32)


def _attn_in_kernel(h_ref, nw_ref, w_ref, qg_ref, kg_ref, cc_ref, ss_ref, q_ref, kv_ref):
    n_q = q_ref.shape[-1] // HEAD_DIM
    n_kv = kv_ref.shape[-1] // (2 * HEAD_DIM)
    tm = h_ref.shape[1]
    sub = min(tm, SUB_TILE)
    half = HEAD_DIM // 2
    q_gain = qg_ref[...] * (HEAD_DIM ** -0.5)
    k_gain = kg_ref[...]

    for r in range(tm // sub):
        rows = slice(r * sub, (r + 1) * sub)
        hn = _rms(h_ref[0, rows, :], nw_ref[...]).astype(BF16)
        qkv = _dot(hn, w_ref[...])
        cc = cc_ref[rows, :]
        ss = ss_ref[rows, :]
        tabs_q = (q_gain * cc, pltpu.roll(q_gain, half, 1) * ss)
        tabs_k = (k_gain * cc, pltpu.roll(k_gain, half, 1) * ss)

        def norm_rope(x, tabs):
            y = x * lax.rsqrt(jnp.mean(x * x, axis=-1, keepdims=True) + EPS)
            return y * tabs[0] + pltpu.roll(y, half, 1) * tabs[1]

        for j in range(n_q):
            sl = slice(j * HEAD_DIM, (j + 1) * HEAD_DIM)
            q_ref[0, rows, sl] = norm_rope(qkv[:, sl], tabs_q).astype(BF16)
        for j in range(n_kv):
            sl = slice(j * HEAD_DIM, (j + 1) * HEAD_DIM)
            src = slice((n_q + j) * HEAD_DIM, (n_q + j + 1) * HEAD_DIM)
            kv_ref[0, rows, sl] = norm_rope(qkv[:, src], tabs_k).astype(BF16)
        kv_ref[0, rows, n_kv * HEAD_DIM:] = qkv[:, (n_q + n_kv) * HEAD_DIM:].astype(BF16)


def _attn_in(h, nw, w_in, q_gain, k_gain, cc, ss, tm):
    bx, sx, d = h.shape
    n_in = w_in.shape[1]
    dq = d
    dkv = (n_in - dq) // 2
    grid = (bx, sx // tm)
    tok = lambda width: pl.BlockSpec((1, tm, width), lambda b, i: (b, i, 0))
    return pl.pallas_call(
        _attn_in_kernel,
        grid=grid,
        in_specs=[
            tok(d),
            _resident((1, d)),
            _resident((d, n_in)),
            _resident((1, HEAD_DIM)),
            _resident((1, HEAD_DIM)),
            pl.BlockSpec((tm, HEAD_DIM), lambda b, i: (i, 0)),
            pl.BlockSpec((tm, HEAD_DIM), lambda b, i: (i, 0)),
        ],
        out_specs=[tok(dq), tok(2 * dkv)],
        out_shape=[
            jax.ShapeDtypeStruct((bx, sx, dq), BF16),
            jax.ShapeDtypeStruct((bx, sx, 2 * dkv), BF16),
        ],
        compiler_params=_params(2),
        name="attn_in",
    )(h, nw, w_in, q_gain, k_gain, cc, ss)


def _softmax_pv(scores, values, sink_col):
    m = sink_col
    for s in scores:
        m = jnp.maximum(m, jnp.max(s, axis=-1, keepdims=True))
    den = jnp.exp(sink_col - m)
    acc = None
    for s, v in zip(scores, values):
        p = jnp.exp(s - m)
        den = den + jnp.sum(p, axis=-1, keepdims=True)
        pv = _dot(p.astype(BF16), v)
        acc = pv if acc is None else acc + pv
    return acc / den


def _attn_kernel(sink_ref, q_ref, kvp_ref, kvc_ref, kvn_ref, kvm_ref, o_ref, *, n_steps):
    step = pl.program_id(1)
    blk = ATTN_BLOCK
    n_blk = q_ref.shape[1] // blk
    v_off = N_KV_HEADS * HEAD_DIM
    a = lax.broadcasted_iota(jnp.int32, (GROUP * blk, 3 * blk), 0) % blk
    c = lax.broadcasted_iota(jnp.int32, (GROUP * blk, 3 * blk), 1)
    in_prev = (c < blk) & (c >= a)
    in_own = (c >= blk) & (c < 2 * blk)
    in_next = (c >= 2 * blk) & (c - 2 * blk <= a)
    interior = in_prev | in_own | in_next
    neg_inf = jnp.float32(-jnp.inf)
    meta_lane = lax.broadcasted_iota(jnp.int32, (GROUP * blk, blk), 1) < N_META
    kv_blocks = ([kvp_ref.at[0]] + [kvc_ref.at[0, pl.ds(i * blk, blk), :] for i in range(n_blk)]
                 + [kvn_ref.at[0]])
    for i in range(n_blk):
        rows = slice(i * blk, (i + 1) * blk)
        kv_refs = kv_blocks[i:i + 3]
        mask = interior
        if i == 0:
            mask = (in_prev & (step > 0)) | in_own | in_next
        if i == n_blk - 1:
            mask = mask & (in_prev | in_own | (step < n_steps - 1))
        for hk in range(N_KV_HEADS):
            ks = slice(hk * HEAD_DIM, (hk + 1) * HEAD_DIM)
            vs = slice(v_off + hk * HEAD_DIM, v_off + (hk + 1) * HEAD_DIM)
            keys = jnp.concatenate([r[:, ks] for r in kv_refs], axis=0)
            vals = jnp.concatenate([r[:, vs] for r in kv_refs], axis=0)
            heads = [hk * GROUP + g for g in range(GROUP)]
            q4 = jnp.concatenate(
                [q_ref[0, rows, h * HEAD_DIM:(h + 1) * HEAD_DIM] for h in heads], axis=0)
            s_band = jnp.where(mask, _dot_nt(q4, keys), neg_inf)
            pad = jnp.zeros((blk - N_META, HEAD_DIM), BF16)
            km = jnp.concatenate([kvm_ref[0, :, ks], pad], axis=0)
            vm = jnp.concatenate([kvm_ref[0, :, vs], pad], axis=0)
            s_meta = jnp.where(meta_lane, _dot_nt(q4, km), neg_inf)
            sink_col = jnp.concatenate(
                [jnp.full((blk, 1), sink_ref[h], F32) for h in heads], axis=0)
            tiles = [s_band[:, t * blk:(t + 1) * blk] for t in range(3)] + [s_meta]
            m = jnp.maximum(
                sink_col, jnp.max(functools.reduce(jnp.maximum, tiles), axis=-1, keepdims=True))
            ps = [jnp.exp(t - m) for t in tiles]
            den = jnp.exp(sink_col - m) + jnp.sum(
                functools.reduce(jnp.add, ps), axis=-1, keepdims=True)
            o4 = (_dot(jnp.concatenate(ps[:3], axis=1).astype(BF16), vals)
                  + _dot(ps[3].astype(BF16), vm)) / den
            for g, h in enumerate(heads):
                o_ref[0, rows, h * HEAD_DIM:(h + 1) * HEAD_DIM] = (
                    o4[g * blk:(g + 1) * blk].astype(BF16))


def _attn(q, kv, kvm, sink):
    b, s, dq = q.shape
    dkv2 = kv.shape[-1]
    blk = ATTN_BLOCK
    n_blk = min(ATTN_BLOCKS_PER_STEP, s // blk)
    n_steps = s // (n_blk * blk)
    last_blk = s // blk - 1
    prev = pl.BlockSpec((1, blk, dkv2), lambda bi, n: (bi, jnp.maximum(n_blk * n - 1, 0), 0))
    own = pl.BlockSpec((1, n_blk * blk, dkv2), lambda bi, n: (bi, n, 0))
    nxt = pl.BlockSpec((1, blk, dkv2),
                       lambda bi, n: (bi, jnp.minimum(n_blk * (n + 1), last_blk), 0))
    meta = pl.BlockSpec((1, N_META, dkv2), lambda bi, n: (bi, 0, 0))
    return pl.pallas_call(
        functools.partial(_attn_kernel, n_steps=n_steps),
        grid=(b, n_steps),
        in_specs=[
            pl.BlockSpec(memory_space=pltpu.SMEM),
            pl.BlockSpec((1, n_blk * blk, dq), lambda bi, n: (bi, n, 0)),
            prev, own, nxt, meta,
        ],
        out_specs=pl.BlockSpec((1, n_blk * blk, dq), lambda bi, n: (bi, n, 0)),
        out_shape=jax.ShapeDtypeStruct((b, s, dq), BF16),
        compiler_params=_params(2),
        name="band_attn",
    )(sink, q, kv, kv, kv, kvm)


def _attn_meta_kernel(sink_ref, qm_ref, kvm_ref, kvr_ref, o_ref):
    rows = GROUP * N_META
    v_off = N_KV_HEADS * HEAD_DIM
    i = lax.broadcasted_iota(jnp.int32, (rows, ATTN_BLOCK), 0) % N_META
    j = lax.broadcasted_iota(jnp.int32, (rows, ATTN_BLOCK), 1)
    allowed = (N_META + j) - i <= WINDOW
    neg_inf = jnp.float32(-jnp.inf)
    for hk in range(N_KV_HEADS):
        ks = slice(hk * HEAD_DIM, (hk + 1) * HEAD_DIM)
        vs = slice(v_off + hk * HEAD_DIM, v_off + (hk + 1) * HEAD_DIM)
        heads = [hk * GROUP + g for g in range(GROUP)]
        q4 = jnp.concatenate(
            [qm_ref[0, :, h * HEAD_DIM:(h + 1) * HEAD_DIM] for h in heads], axis=0)
        s_meta = _dot_nt(q4, kvm_ref[0, :, ks])
        s_real = jnp.where(allowed, _dot_nt(q4, kvr_ref[0, :, ks]), neg_inf)
        sink_col = jnp.concatenate(
            [jnp.full((N_META, 1), sink_ref[h], F32) for h in heads], axis=0)
        o4 = _softmax_pv([s_meta, s_real], [kvm_ref[0, :, vs], kvr_ref[0, :, vs]], sink_col)
        for g, h in enumerate(heads):
            o_ref[0, :, h * HEAD_DIM:(h + 1) * HEAD_DIM] = (
                o4[g * N_META:(g + 1) * N_META].astype(BF16))


def _attn_meta(qm, kvm, kv, sink):
    b, _, dq = qm.shape
    dkv2 = kvm.shape[-1]
    meta = lambda width: pl.BlockSpec((1, N_META, width), lambda bi: (bi, 0, 0))
    first = pl.BlockSpec((1, ATTN_BLOCK, dkv2), lambda bi: (bi, 0, 0))
    return pl.pallas_call(
        _attn_meta_kernel,
        grid=(b,),
        in_specs=[pl.BlockSpec(memory_space=pltpu.SMEM), meta(dq), meta(dkv2), first],
        out_specs=meta(dq),
        out_shape=jax.ShapeDtypeStruct((b, N_META, dq), BF16),
        compiler_params=_params(1),
        name="meta_attn",
    )(sink, qm, kvm, kv)


def _post_kernel(*refs, rec, ff_chunk):
    if rec:
        h_ref, of_ref, ob_ref, gate_ref, og_ref, wo_ref, nw_ref, wu_ref, wd_ref, out_ref = refs
        o = of_ref[0].astype(F32) + ob_ref[0].astype(F32)
        n_heads = o.shape[-1] // HEAD_DIM
        o = jnp.concatenate(
            [_rms(o[:, j * HEAD_DIM:(j + 1) * HEAD_DIM], og_ref[...]) for j in range(n_heads)],
            axis=1)
        o = (o * gate_ref[0].astype(F32)).astype(BF16)
    else:
        h_ref, o_ref, wo_ref, nw_ref, wu_ref, wd_ref, out_ref = refs
        o = o_ref[0]
    h1 = h_ref[0] + _dot(o, wo_ref[...])
    hn = _rms(h1, nw_ref[...]).astype(BF16)
    d_ff = wu_ref.shape[1]
    acc = h1
    for c in range(d_ff // ff_chunk):
        sl = slice(c * ff_chunk, (c + 1) * ff_chunk)
        z = jnp.maximum(_dot(hn, wu_ref[:, sl]), 0.0)
        acc = acc + _dot((z * z).astype(BF16), wd_ref[sl, :])
    out_ref[0] = acc


def _post(h, mixer_out, w_out, nw, w_up, w_down, tm, rec_extra=None):
    bx, sx, d = h.shape
    d_ff = w_up.shape[1]
    tok = pl.BlockSpec((1, tm, d), lambda b, i: (b, i, 0))
    rec = rec_extra is not None
    if rec:
        ins = [h, *mixer_out, rec_extra, w_out, nw, w_up, w_down]
        specs = [tok, tok, tok, tok, _resident((1, HEAD_DIM))]
    else:
        ins = [h, mixer_out, w_out, nw, w_up, w_down]
        specs = [tok, tok]
    specs += [_resident((d, d)), _resident((1, d)), _resident((d, d_ff)), _resident((d_ff, d))]
    return pl.pallas_call(
        functools.partial(_post_kernel, rec=rec, ff_chunk=min(FF_CHUNK, d_ff)),
        grid=(bx, sx // tm),
        in_specs=specs,
        out_specs=tok,
        out_shape=jax.ShapeDtypeStruct((bx, sx, d), F32),
        compiler_params=_params(2),
        name="post_rec" if rec else "post_attn",
    )(*ins)


def _lower_bound(tab_ref, layer):
    t = tab_ref[...]
    e = jnp.exp(t - jnp.max(t, axis=0, keepdims=True))
    p = e / jnp.sum(e, axis=0, keepdims=True)
    return jnp.sum(p[1:layer + 1], axis=0, keepdims=True)


def _rec_in_rows(h, nw_ref, w_ref, lbf_ref, lbb_ref, layer):
    d = h.shape[-1]
    hn = _rms(h, nw_ref[...]).astype(BF16)
    proj = lambda j: _dot(hn, w_ref[:, j * d:(j + 1) * d])

    def log_forget(z, tab_ref):
        lb = _lower_bound(tab_ref, layer)
        return jnp.log2(lb + (1.0 - lb) * jax.nn.sigmoid(z))

    gf = log_forget(proj(2), lbf_ref)
    gb = log_forget(proj(3), lbb_ref)
    zq = proj(0)
    q = zq * jax.nn.sigmoid(zq)
    zg = proj(4)
    gate = zg * jax.nn.sigmoid(zg)
    v = proj(1)
    return q, v, gf, gb, gate


def _rec_in_kernel(h_ref, nw_ref, w_ref, lbf_ref, lbb_ref,
                   q_ref, v_ref, gf_ref, gb_ref, gate_ref, *, layer):
    tm = h_ref.shape[1]
    sub = min(tm, SUB_TILE)
    for r in range(tm // sub):
        rows = slice(r * sub, (r + 1) * sub)
        q, v, gf, gb, gate = _rec_in_rows(h_ref[0, rows, :], nw_ref, w_ref, lbf_ref, lbb_ref,
                                          layer)
        q_ref[0, rows, :] = q.astype(BF16)
        v_ref[0, rows, :] = v.astype(BF16)
        gf_ref[0, rows, :] = gf
        gb_ref[0, rows, :] = gb
        gate_ref[0, rows, :] = gate.astype(BF16)


def _rec_in(h, nw, w_in, lb_f, lb_b, layer, tm):
    b, s, d = h.shape
    depth = lb_f.shape[0]
    tok = pl.BlockSpec((1, tm, d), lambda bi, i: (bi, i, 0))
    return pl.pallas_call(
        functools.partial(_rec_in_kernel, layer=layer),
        grid=(b, s // tm),
        in_specs=[tok, _resident((1, d)), _resident((d, w_in.shape[1])),
                  _resident((depth, d)), _resident((depth, d))],
        out_specs=[tok, tok, tok, tok, tok],
        out_shape=[
            jax.ShapeDtypeStruct((b, s, d), BF16),
            jax.ShapeDtypeStruct((b, s, d), BF16),
            jax.ShapeDtypeStruct((b, s, d), F32),
            jax.ShapeDtypeStruct((b, s, d), F32),
            jax.ShapeDtypeStruct((b, s, d), BF16),
        ],
        compiler_params=_params(2),
        name="rec_in",
    )(h, nw, w_in, lb_f, lb_b)


def _rec_in_meta_kernel(hm_ref, nw_ref, w_ref, lbf_ref, lbb_ref,
                        q_ref, v_ref, gf_ref, gb_ref, gate_ref, *, layer):
    b, n_meta, d = hm_ref.shape
    h = hm_ref[...].reshape(b * n_meta, d)
    q, v, gf, gb, gate = _rec_in_rows(h, nw_ref, w_ref, lbf_ref, lbb_ref, layer)
    pad = REC_CHUNK - n_meta
    for ref, val in ((q_ref, q), (v_ref, v), (gf_ref, gf), (gb_ref, gb)):
        ref[:, :pad, :] = jnp.zeros((b, pad, d), ref.dtype)
        ref[:, pad:, :] = val.reshape(b, n_meta, d).astype(ref.dtype)
    gate_ref[...] = gate.reshape(b, n_meta, d).astype(BF16)


def _rec_in_meta(hm, nw, w_in, lb_f, lb_b, layer):
    b, n_meta, d = hm.shape
    depth = lb_f.shape[0]
    chunk = pl.BlockSpec((b, REC_CHUNK, d), lambda i: (0, 0, 0))
    whole = pl.BlockSpec((b, n_meta, d), lambda i: (0, 0, 0))
    return pl.pallas_call(
        functools.partial(_rec_in_meta_kernel, layer=layer),
        grid=(1,),
        in_specs=[whole, _resident((1, d)), _resident((d, w_in.shape[1])),
                  _resident((depth, d)), _resident((depth, d))],
        out_specs=[chunk, chunk, chunk, chunk, whole],
        out_shape=[
            jax.ShapeDtypeStruct((b, REC_CHUNK, d), BF16),
            jax.ShapeDtypeStruct((b, REC_CHUNK, d), BF16),
            jax.ShapeDtypeStruct((b, REC_CHUNK, d), F32),
            jax.ShapeDtypeStruct((b, REC_CHUNK, d), F32),
            jax.ShapeDtypeStruct((b, n_meta, d), BF16),
        ],
        compiler_params=_params(1),
        name="rec_in_meta",
    )(hm, nw, w_in, lb_f, lb_b)


_SUM_LEVEL_HALVES = (32, 16, 8, 4, 2)
_N_SUM_LEVELS = len(_SUM_LEVEL_HALVES)
_MASK_ADJACENT = _N_SUM_LEVELS
_MASK_DIAG = _N_SUM_LEVELS + 1
_ROWS_Q_INTER = _N_SUM_LEVELS * REC_CHUNK
_ROWS_K_STATE = (_N_SUM_LEVELS + 1) * REC_CHUNK
_N_SUM_ROWS = (_N_SUM_LEVELS + 2) * REC_CHUNK
SCAN_CHUNKS_PER_STEP = 4
SCAN_PIPELINE_LAG = 8
SUBLANE_TILE = 8


def _scan_constants(backward):
    c = REC_CHUNK
    u = np.arange(c)
    w = (u[None, :] >= u[:, None]) if backward else (u[None, :] <= u[:, None])
    masks = np.zeros((_N_SUM_LEVELS + 2, c, c), np.float32)
    for lvl, half in enumerate(_SUM_LEVEL_HALVES + (1,)):
        for t in range(c):
            start = (t // (2 * half)) * 2 * half
            if t - start >= half:
                masks[lvl, t, start:start + half] = 1.0
    masks[_MASK_DIAG] = np.eye(c, dtype=np.float32)
    if backward:
        masks = masks[:, ::-1, ::-1]
    w = np.concatenate([w, w], axis=1).astype(np.float32)
    return jnp.asarray(w, BF16), jnp.asarray(np.ascontiguousarray(masks), F32)


def _split_decays(cum, backward):
    c, d = cum.shape
    tile = SUBLANE_TILE
    sub = lax.broadcasted_iota(jnp.int32, (tile, d), 0)
    sign4 = jnp.where((sub < 4) != backward, -1.0, 1.0).astype(F32)
    sign2 = jnp.where(((sub % 4) < 2) != backward, -1.0, 1.0).astype(F32)
    row = lambda r: cum[r:r + 1]
    tiles = [cum[t:t + tile] for t in range(0, c, tile)]
    out = []
    for half in _SUM_LEVEL_HALVES:
        for i, x in enumerate(tiles):
            t0 = i * tile
            if half >= tile:
                start = (t0 // (2 * half)) * 2 * half
                r = start + half if backward else start + half - 1
                after = (t0 - start >= half) != backward
                out.append(x - row(r) if after else row(r) - x)
            elif half == 4:
                r = t0 + 4 if backward else t0 + 3
                out.append((x - row(r)) * sign4)
            else:
                r1, r2 = (t0 + 2, t0 + 6) if backward else (t0 + 1, t0 + 5)
                out.append((x - jnp.where(sub < 4, row(r1), row(r2))) * sign2)
    out.extend(tiles)
    last = 0 if backward else c - 1
    out.extend(row(last) - x for x in tiles)
    return jnp.concatenate(out, axis=0)


class _ChunkJob:
    def __init__(self, d, q_ref, v_ref, g_ref, o_ref, rows, slot):
        self.d, self.q_ref, self.v_ref, self.g_ref, self.o_ref = d, q_ref, v_ref, g_ref, o_ref
        self.rows, self.slot = rows, slot


def _scan_kernel(qf_ref, vf_ref, gf_ref, qb_ref, vb_ref, gb_ref, qm_ref, vm_ref, gfm_ref, gbm_ref,
                 wf_ref, wb_ref, mf_ref, mb_ref, of_ref, ob_ref, ofm_ref, obm_ref,
                 st_ref, e_ref):
    c = REC_CHUNK
    n_heads = qf_ref.shape[-1] // HEAD_DIM
    n_chunks = qf_ref.shape[1] // c
    step = pl.program_id(1)
    w_refs = (wf_ref, wb_ref)
    m_refs = (mf_ref, mb_ref)
    whole_rows = (c - 1, 0)

    @pl.when(step == 0)
    def _():
        st_ref[...] = jnp.zeros(st_ref.shape, F32)

    def decay_stage(job):
        g2 = job.g_ref[0, job.rows, :]
        g_hi = g2.astype(BF16)
        g_lo = (g2 - g_hi.astype(F32)).astype(BF16)
        cum = _dot(w_refs[job.d][...], jnp.concatenate([g_hi, g_lo], axis=0))
        e_ref[job.slot, job.d] = jnp.exp2(_split_decays(cum, backward=job.d == 1))

    def pair_stage(job, hd):
        d, m_ref = job.d, m_refs[job.d]
        sl = slice(hd * HEAD_DIM, (hd + 1) * HEAD_DIM)
        e = lambda r0: e_ref[job.slot, d, r0:r0 + c, sl]
        q = job.q_ref[0, job.rows, sl].astype(F32)
        f = jnp.exp2(job.g_ref[0, job.rows, sl])
        k = 1.0 - f
        row_id = lax.broadcasted_iota(jnp.int32, (c, HEAD_DIM), 0)
        near = _dot_nt(jnp.concatenate([q, q * f], axis=0).astype(BF16), k.astype(BF16))
        pair = m_ref[_MASK_DIAG] * near[:c] + m_ref[_MASK_ADJACENT] * near[c:]
        for lvl, half in enumerate(_SUM_LEVEL_HALVES):
            is_query = ((row_id // half) % 2 == 1) != (d == 1)
            both = (jnp.where(is_query, q, k) * e(lvl * c)).astype(BF16)
            pair = pair + m_ref[lvl] * _dot_nt(both, both)
        q_in = (q * e(_ROWS_Q_INTER)).astype(BF16)
        k_out = (k * e(_ROWS_K_STATE)).astype(BF16)
        return pair.astype(BF16), q_in, k_out

    def output_stage(job, hd, pair, q_in, k_out):
        d = job.d
        sl = slice(hd * HEAD_DIM, (hd + 1) * HEAD_DIM)
        v = job.v_ref[0, job.rows, sl]
        state = st_ref[d, hd]
        o = _dot(pair, v) + _dot(q_in, state.astype(BF16))
        job.o_ref[0, job.rows, sl] = o.astype(BF16)
        row = _ROWS_Q_INTER + whole_rows[d]
        t0 = (row // SUBLANE_TILE) * SUBLANE_TILE
        decay_col = e_ref[job.slot, d, t0:t0 + SUBLANE_TILE, sl].T[:, row - t0:row - t0 + 1]
        st_ref[d, hd] = state * decay_col + _dot_tn(k_out, v)

    def run(jobs):
        pending = []
        for group in jobs:
            for job in group:
                decay_stage(job)
            for hd in range(n_heads):
                for job in group:
                    pending.append((job, hd) + pair_stage(job, hd))
                    if len(pending) > SCAN_PIPELINE_LAG:
                        output_stage(*pending.pop(0))
        for item in pending:
            output_stage(*item)

    whole = slice(0, c)

    @pl.when(step == 0)
    def _():
        run([[_ChunkJob(0, qm_ref, vm_ref, gfm_ref, ofm_ref, whole, 0)]])

    groups = []
    for i in range(n_chunks):
        fwd_rows = slice(i * c, (i + 1) * c)
        bwd_rows = slice((n_chunks - 1 - i) * c, (n_chunks - i) * c)
        groups.append([_ChunkJob(0, qf_ref, vf_ref, gf_ref, of_ref, fwd_rows, i),
                       _ChunkJob(1, qb_ref, vb_ref, gb_ref, ob_ref, bwd_rows, i)])
    run(groups)

    @pl.when(step == pl.num_programs(1) - 1)
    def _():
        run([[_ChunkJob(1, qm_ref, vm_ref, gbm_ref, obm_ref, whole, 0)]])


def _scan(q, v, gf, gb, qm, vm, gfm, gbm):
    b, s, d = q.shape
    n_heads = d // HEAD_DIM
    chunks_per_step = min(SCAN_CHUNKS_PER_STEP, s // REC_CHUNK)
    rows = chunks_per_step * REC_CHUNK
    n_steps = s // rows
    wf, mf = _scan_constants(False)
    wb, mb = _scan_constants(True)
    fwd = pl.BlockSpec((1, rows, d), lambda bi, j: (bi, j, 0))
    bwd = pl.BlockSpec((1, rows, d), lambda bi, j: (bi, n_steps - 1 - j, 0))
    meta = pl.BlockSpec((1, REC_CHUNK, d), lambda bi, j: (bi, 0, 0))
    return pl.pallas_call(
        _scan_kernel,
        grid=(b, n_steps),
        in_specs=[fwd, fwd, fwd, bwd, bwd, bwd, meta, meta, meta, meta,
                  _resident(wf.shape), _resident(wb.shape),
                  _resident(mf.shape), _resident(mb.shape)],
        out_specs=[fwd, bwd, meta, meta],
        out_shape=[jax.ShapeDtypeStruct((b, s, d), BF16)] * 2
        + [jax.ShapeDtypeStruct((b, REC_CHUNK, d), BF16)] * 2,
        scratch_shapes=[
            pltpu.VMEM((2, n_heads, HEAD_DIM, HEAD_DIM), F32),
            pltpu.VMEM((chunks_per_step, 2, _N_SUM_ROWS, d), F32),
        ],
        compiler_params=_params(2),
        name="gla_scan",
    )(q, v, gf, q, v, gb, qm, vm, gfm, gbm, wf, wb, mf, mb)


def _rope_tables(length):
    inv_freq = ROPE_THETA ** (-jnp.arange(0, HEAD_DIM, 2, dtype=F32) / HEAD_DIM)
    ang = jnp.arange(length, dtype=F32)[:, None] * inv_freq[None, :]
    cos, sin = jnp.cos(ang), jnp.sin(ang)
    return jnp.concatenate([cos, cos], axis=-1), jnp.concatenate([-sin, sin], axis=-1)


def kernel(x, meta_tokens, mix_norm_w, mlp_norm_w, attn_w_in, attn_w_out, attn_q_norm_w,
           attn_k_norm_w, attn_sink, rec_w_in, rec_w_out, rec_lb_fwd, rec_lb_bwd,
           rec_out_norm_w, mlp_w_up, mlp_w_down):
    b, s, d = x.shape
    depth = mix_norm_w.shape[0]
    n_mixers = 2
    tm = min(TOKEN_TILE, s)
    n_meta_rows = b * N_META
    assert d % HEAD_DIM == 0 and d // HEAD_DIM == N_KV_HEADS * GROUP, x.shape
    assert s % tm == 0 and s % ATTN_BLOCK == 0 and s % REC_CHUNK == 0, x.shape
    assert meta_tokens.shape == (N_META, d), meta_tokens.shape

    h = x
    hm = jnp.broadcast_to(meta_tokens.astype(x.dtype)[None], (b, N_META, d))
    cc, ss = _rope_tables(N_META + s)
    cc_meta, ss_meta = jnp.tile(cc[:N_META], (b, 1)), jnp.tile(ss[:N_META], (b, 1))
    cc_real, ss_real = cc[N_META:], ss[N_META:]
    row = lambda w: w.reshape(1, -1)
    flat_meta = lambda t: t.reshape(1, n_meta_rows, t.shape[-1])
    per_batch = lambda t: t.reshape(b, N_META, t.shape[-1])

    for layer in range(depth):
        j = layer // n_mixers
        nw = row(mix_norm_w[layer])
        mlp = (row(mlp_norm_w[layer]), mlp_w_up[layer].astype(BF16),
               mlp_w_down[layer].astype(BF16))
        if layer % n_mixers == 0:
            w_in = attn_w_in[j].astype(BF16)
            w_out = attn_w_out[j].astype(BF16)
            gains = (row(attn_q_norm_w[j]), row(attn_k_norm_w[j]))
            q, kv = _attn_in(h, nw, w_in, *gains, cc_real, ss_real, tm)
            qm, kvm = map(per_batch, _attn_in(
                flat_meta(hm), nw, w_in, *gains, cc_meta, ss_meta, n_meta_rows))
            sink = attn_sink[j].astype(F32)
            o = _attn(q, kv, kvm, sink)
            om = _attn_meta(qm, kvm, kv, sink)
            h = _post(h, o, w_out, *mlp, tm)
            hm = per_batch(_post(flat_meta(hm), flat_meta(om), w_out, *mlp, n_meta_rows))
        else:
            w_in = rec_w_in[j].astype(BF16)
            w_out = rec_w_out[j].astype(BF16)
            gain = row(rec_out_norm_w[j])
            q, v, gf, gb, gate = _rec_in(h, nw, w_in, rec_lb_fwd, rec_lb_bwd, layer, tm)
            qm, vm, gfm, gbm, gate_m = _rec_in_meta(hm, nw, w_in, rec_lb_fwd, rec_lb_bwd, layer)
            o_f, o_b, o_fm, o_bm = _scan(q, v, gf, gb, qm, vm, gfm, gbm)
            h = _post(h, (o_f, o_b, gate), w_out, *mlp, tm, rec_extra=gain)
            meta_rows = lambda t: flat_meta(t[:, REC_CHUNK - N_META:])
            hm = per_batch(_post(flat_meta(hm), (meta_rows(o_fm), meta_rows(o_bm),
                                                 flat_meta(gate_m)),
                                 w_out, *mlp, n_meta_rows, rec_extra=gain))
    return h
```

```python
import functools

import numpy as np
import jax
import jax.numpy as jnp
from jax import lax
from jax.experimental import pallas as pl
from jax.experimental.pallas import tpu as pltpu

N_META = 16
HEAD_DIM = 128
N_KV_HEADS = 2
GROUP = 4
WINDOW = 128
ATTN_BLOCK = 128
ROPE_THETA = 10000.0
REC_CHUNK = 64
EPS = 1e-6
LOG2E = 1.4426950408889634

F32 = jnp.float32
BF16 = jnp.bfloat16

V7X_VMEM_BYTES = 64 * 1024 * 1024
V7X_VMEM_LIMIT_BYTES = V7X_VMEM_BYTES * 7 // 8
TOKEN_TILE = 512
FF_CHUNK = 1024
SUB_TILE = 256
ATTN_BLOCKS_PER_STEP = 4


def _params(n_axes):
    return pltpu.CompilerParams(
        dimension_semantics=("arbitrary",) * n_axes,
        vmem_limit_bytes=V7X_VMEM_LIMIT_BYTES,
    )


def _resident(shape):
    zeros = (0,) * len(shape)
    return pl.BlockSpec(shape, lambda *_: zeros, pipeline_mode=pl.Buffered(1))


def _rms(x, w):
    ms = jnp.mean(x * x, axis=-1, keepdims=True)
    return x * lax.rsqrt(ms + EPS) * w


def _dot(a, b):
    return jnp.dot(a, b, preferred_element_type=F32)


def _dot_nt(a, b):
    return lax.dot_general(a, b, (((1,), (1,)), ((), ())), preferred_element_type=F32)


def _dot_tn(a, b):
    return lax.dot_general(a, b, (((0,), (0,)), ((), ())), preferred_element_type=F32)


def _attn_in_kernel(h_ref, nw_ref, w_ref, qg_ref, kg_ref, cc_ref, ss_ref, q_ref, kv_ref):
    n_q = q_ref.shape[-1] // HEAD_DIM
    n_kv = kv_ref.shape[-1] // (2 * HEAD_DIM)
    tm = h_ref.shape[1]
    sub = min(tm, SUB_TILE)
    half = HEAD_DIM // 2
    q_gain = qg_ref[...] * (HEAD_DIM ** -0.5 * LOG2E)
    k_gain = kg_ref[...]

    for r in range(tm // sub):
        rows = slice(r * sub, (r + 1) * sub)
        hn = _rms(h_ref[0, rows, :], nw_ref[...]).astype(BF16)
        qkv = _dot(hn, w_ref[...])
        cc = cc_ref[rows, :]
        ss = ss_ref[rows, :]
        tabs_q = (q_gain * cc, pltpu.roll(q_gain, half, 1) * ss)
        tabs_k = (k_gain * cc, pltpu.roll(k_gain, half, 1) * ss)

        def norm_rope(x, tabs):
            y = x * lax.rsqrt(jnp.mean(x * x, axis=-1, keepdims=True) + EPS)
            return y * tabs[0] + pltpu.roll(y, half, 1) * tabs[1]

        for j in range(n_q):
            sl = slice(j * HEAD_DIM, (j + 1) * HEAD_DIM)
            q_ref[0, rows, sl] = norm_rope(qkv[:, sl], tabs_q).astype(BF16)
        for j in range(n_kv):
            sl = slice(j * HEAD_DIM, (j + 1) * HEAD_DIM)
            src = slice((n_q + j) * HEAD_DIM, (n_q + j + 1) * HEAD_DIM)
            kv_ref[0, rows, sl] = norm_rope(qkv[:, src], tabs_k).astype(BF16)
        kv_ref[0, rows, n_kv * HEAD_DIM:] = qkv[:, (n_q + n_kv) * HEAD_DIM:].astype(BF16)


def _attn_in(h, nw, w_in, q_gain, k_gain, cc, ss, tm):
    bx, sx, d = h.shape
    n_in = w_in.shape[1]
    dq = d
    dkv = (n_in - dq) // 2
    grid = (bx, sx // tm)
    tok = lambda width: pl.BlockSpec((1, tm, width), lambda b, i: (b, i, 0))
    return pl.pallas_call(
        _attn_in_kernel,
        grid=grid,
        in_specs=[
            tok(d),
            _resident((1, d)),
            _resident((d, n_in)),
            _resident((1, HEAD_DIM)),
            _resident((1, HEAD_DIM)),
            pl.BlockSpec((tm, HEAD_DIM), lambda b, i: (i, 0)),
            pl.BlockSpec((tm, HEAD_DIM), lambda b, i: (i, 0)),
        ],
        out_specs=[tok(dq), tok(2 * dkv)],
        out_shape=[
            jax.ShapeDtypeStruct((bx, sx, dq), BF16),
            jax.ShapeDtypeStruct((bx, sx, 2 * dkv), BF16),
        ],
        compiler_params=_params(2),
        name="attn_in",
    )(h, nw, w_in, q_gain, k_gain, cc, ss)


def _softmax_pv(scores, values, sink_col):
    m = sink_col
    for s in scores:
        m = jnp.maximum(m, jnp.max(s, axis=-1, keepdims=True))
    den = jnp.exp2(sink_col - m)
    acc = None
    for s, v in zip(scores, values):
        p = jnp.exp2(s - m)
        den = den + jnp.sum(p, axis=-1, keepdims=True)
        pv = _dot(p.astype(BF16), v)
        acc = pv if acc is None else acc + pv
    return acc / den


def _attn_kernel(sink_ref, q_ref, kvp_ref, kvc_ref, kvn_ref, kvm_ref, o_ref, *, n_steps):
    step = pl.program_id(1)
    blk = ATTN_BLOCK
    n_blk = q_ref.shape[1] // blk
    v_off = N_KV_HEADS * HEAD_DIM
    a = lax.broadcasted_iota(jnp.int32, (GROUP * blk, 3 * blk), 0) % blk
    c = lax.broadcasted_iota(jnp.int32, (GROUP * blk, 3 * blk), 1)
    in_prev = (c < blk) & (c >= a)
    in_own = (c >= blk) & (c < 2 * blk)
    in_next = (c >= 2 * blk) & (c - 2 * blk <= a)
    interior = in_prev | in_own | in_next
    neg_inf = jnp.float32(-jnp.inf)
    lane = lax.broadcasted_iota(jnp.int32, (GROUP * blk, blk), 1)
    meta_lane = lane < N_META
    meta_fill = []
    for hk in range(N_KV_HEADS):
        sink_tile = jnp.concatenate(
            [jnp.full((blk, blk), sink_ref[hk * GROUP + g] * LOG2E, F32) for g in range(GROUP)],
            axis=0)
        meta_fill.append(jnp.where(lane == N_META, sink_tile, neg_inf))
    kv_blocks = ([kvp_ref.at[0]] + [kvc_ref.at[0, pl.ds(i * blk, blk), :] for i in range(n_blk)]
                 + [kvn_ref.at[0]])
    for i in range(n_blk):
        rows = slice(i * blk, (i + 1) * blk)
        kv_refs = kv_blocks[i:i + 3]
        mask = interior
        if i == 0:
            mask = (in_prev & (step > 0)) | in_own | in_next
        if i == n_blk - 1:
            mask = mask & (in_prev | in_own | (step < n_steps - 1))
        for hk in range(N_KV_HEADS):
            ks = slice(hk * HEAD_DIM, (hk + 1) * HEAD_DIM)
            vs = slice(v_off + hk * HEAD_DIM, v_off + (hk + 1) * HEAD_DIM)
            keys = jnp.concatenate([r[:, ks] for r in kv_refs], axis=0)
            vals = jnp.concatenate([r[:, vs] for r in kv_refs], axis=0)
            heads = [hk * GROUP + g for g in range(GROUP)]
            q4 = jnp.concatenate(
                [q_ref[0, rows, h * HEAD_DIM:(h + 1) * HEAD_DIM] for h in heads], axis=0)
            s_band = jnp.where(mask, _dot_nt(q4, keys), neg_inf)
            pad = jnp.zeros((blk - N_META, HEAD_DIM), BF16)
            km = jnp.concatenate([kvm_ref[0, :, ks], pad], axis=0)
            vm = jnp.concatenate([kvm_ref[0, :, vs], pad], axis=0)
            s_meta = jnp.where(meta_lane, _dot_nt(q4, km), meta_fill[hk])
            tiles = [s_band[:, t * blk:(t + 1) * blk] for t in range(3)] + [s_meta]
            m = jnp.max(functools.reduce(jnp.maximum, tiles), axis=-1, keepdims=True)
            p = jnp.concatenate([jnp.exp2(t - m) for t in tiles], axis=1).astype(BF16)
            o4 = (_dot(p, jnp.concatenate([vals, vm], axis=0))
                  / jnp.sum(p.astype(F32), axis=-1, keepdims=True))
            for g, h in enumerate(heads):
                o_ref[0, rows, h * HEAD_DIM:(h + 1) * HEAD_DIM] = (
                    o4[g * blk:(g + 1) * blk].astype(BF16))


def _attn(q, kv, kvm, sink):
    b, s, dq = q.shape
    dkv2 = kv.shape[-1]
    blk = ATTN_BLOCK
    n_blk = min(ATTN_BLOCKS_PER_STEP, s // blk)
    n_steps = s // (n_blk * blk)
    last_blk = s // blk - 1
    prev = pl.BlockSpec((1, blk, dkv2), lambda bi, n: (bi, jnp.maximum(n_blk * n - 1, 0), 0))
    own = pl.BlockSpec((1, n_blk * blk, dkv2), lambda bi, n: (bi, n, 0))
    nxt = pl.BlockSpec((1, blk, dkv2),
                       lambda bi, n: (bi, jnp.minimum(n_blk * (n + 1), last_blk), 0))
    meta = pl.BlockSpec((1, N_META, dkv2), lambda bi, n: (bi, 0, 0))
    return pl.pallas_call(
        functools.partial(_attn_kernel, n_steps=n_steps),
        grid=(b, n_steps),
        in_specs=[
            pl.BlockSpec(memory_space=pltpu.SMEM),
            pl.BlockSpec((1, n_blk * blk, dq), lambda bi, n: (bi, n, 0)),
            prev, own, nxt, meta,
        ],
        out_specs=pl.BlockSpec((1, n_blk * blk, dq), lambda bi, n: (bi, n, 0)),
        out_shape=jax.ShapeDtypeStruct((b, s, dq), BF16),
        compiler_params=_params(2),
        name="band_attn",
    )(sink, q, kv, kv, kv, kvm)


def _attn_meta_kernel(sink_ref, qm_ref, kvm_ref, kvr_ref, o_ref):
    rows = GROUP * N_META
    v_off = N_KV_HEADS * HEAD_DIM
    i = lax.broadcasted_iota(jnp.int32, (rows, ATTN_BLOCK), 0) % N_META
    j = lax.broadcasted_iota(jnp.int32, (rows, ATTN_BLOCK), 1)
    allowed = (N_META + j) - i <= WINDOW
    neg_inf = jnp.float32(-jnp.inf)
    for hk in range(N_KV_HEADS):
        ks = slice(hk * HEAD_DIM, (hk + 1) * HEAD_DIM)
        vs = slice(v_off + hk * HEAD_DIM, v_off + (hk + 1) * HEAD_DIM)
        heads = [hk * GROUP + g for g in range(GROUP)]
        q4 = jnp.concatenate(
            [qm_ref[0, :, h * HEAD_DIM:(h + 1) * HEAD_DIM] for h in heads], axis=0)
        s_meta = _dot_nt(q4, kvm_ref[0, :, ks])
        s_real = jnp.where(allowed, _dot_nt(q4, kvr_ref[0, :, ks]), neg_inf)
        sink_col = jnp.concatenate(
            [jnp.full((N_META, 1), sink_ref[h] * LOG2E, F32) for h in heads], axis=0)
        o4 = _softmax_pv([s_meta, s_real], [kvm_ref[0, :, vs], kvr_ref[0, :, vs]], sink_col)
        for g, h in enumerate(heads):
            o_ref[0, :, h * HEAD_DIM:(h + 1) * HEAD_DIM] = (
                o4[g * N_META:(g + 1) * N_META].astype(BF16))


def _attn_meta(qm, kvm, kv, sink):
    b, _, dq = qm.shape
    dkv2 = kvm.shape[-1]
    meta = lambda width: pl.BlockSpec((1, N_META, width), lambda bi: (bi, 0, 0))
    first = pl.BlockSpec((1, ATTN_BLOCK, dkv2), lambda bi: (bi, 0, 0))
    return pl.pallas_call(
        _attn_meta_kernel,
        grid=(b,),
        in_specs=[pl.BlockSpec(memory_space=pltpu.SMEM), meta(dq), meta(dkv2), first],
        out_specs=meta(dq),
        out_shape=jax.ShapeDtypeStruct((b, N_META, dq), BF16),
        compiler_params=_params(1),
        name="meta_attn",
    )(sink, qm, kvm, kv)


def _post_kernel(*refs, rec, ff_chunk):
    if rec:
        h_ref, of_ref, ob_ref, gate_ref, og_ref, wo_ref, nw_ref, wu_ref, wd_ref, out_ref = refs
        o = of_ref[0].astype(F32) + ob_ref[0].astype(F32)
        n_heads = o.shape[-1] // HEAD_DIM
        o = jnp.concatenate(
            [_rms(o[:, j * HEAD_DIM:(j + 1) * HEAD_DIM], og_ref[...]) for j in range(n_heads)],
            axis=1)
        o = (o * gate_ref[0].astype(F32)).astype(BF16)
    else:
        h_ref, o_ref, wo_ref, nw_ref, wu_ref, wd_ref, out_ref = refs
        o = o_ref[0]
    h1 = h_ref[0] + _dot(o, wo_ref[...])
    hn = _rms(h1, nw_ref[...]).astype(BF16)
    d_ff = wu_ref.shape[1]
    acc = h1
    for c in range(d_ff // ff_chunk):
        sl = slice(c * ff_chunk, (c + 1) * ff_chunk)
        z = jnp.maximum(_dot(hn, wu_ref[:, sl]), 0.0)
        acc = acc + _dot((z * z).astype(BF16), wd_ref[sl, :])
    out_ref[0] = acc


def _post(h, mixer_out, w_out, nw, w_up, w_down, tm, rec_extra=None):
    bx, sx, d = h.shape
    d_ff = w_up.shape[1]
    tok = pl.BlockSpec((1, tm, d), lambda b, i: (b, i, 0))
    rec = rec_extra is not None
    if rec:
        ins = [h, *mixer_out, rec_extra, w_out, nw, w_up, w_down]
        specs = [tok, tok, tok, tok, _resident((1, HEAD_DIM))]
    else:
        ins = [h, mixer_out, w_out, nw, w_up, w_down]
        specs = [tok, tok]
    specs += [_resident((d, d)), _resident((1, d)), _resident((d, d_ff)), _resident((d_ff, d))]
    return pl.pallas_call(
        functools.partial(_post_kernel, rec=rec, ff_chunk=min(FF_CHUNK, d_ff)),
        grid=(bx, sx // tm),
        in_specs=specs,
        out_specs=tok,
        out_shape=jax.ShapeDtypeStruct((bx, sx, d), F32),
        compiler_params=_params(2),
        name="post_rec" if rec else "post_attn",
    )(*ins)


def _lower_bound(tab_ref, layer):
    t = tab_ref[...]
    e = jnp.exp(t - jnp.max(t, axis=0, keepdims=True))
    p = e / jnp.sum(e, axis=0, keepdims=True)
    return jnp.sum(p[1:layer + 1], axis=0, keepdims=True)


def _rec_in_rows(h, nw_ref, w_ref, lbf_ref, lbb_ref, layer):
    d = h.shape[-1]
    hn = _rms(h, nw_ref[...]).astype(BF16)
    proj = lambda j: _dot(hn, w_ref[:, j * d:(j + 1) * d])

    def log_forget(z, tab_ref):
        lb = _lower_bound(tab_ref, layer)
        return jnp.log2(lb + (1.0 - lb) * jax.nn.sigmoid(z))

    gf = log_forget(proj(2), lbf_ref)
    gb = log_forget(proj(3), lbb_ref)
    zq = proj(0)
    q = zq * jax.nn.sigmoid(zq)
    zg = proj(4)
    gate = zg * jax.nn.sigmoid(zg)
    v = proj(1)
    return q, v, gf, gb, gate


def _rec_in_kernel(h_ref, nw_ref, w_ref, lbf_ref, lbb_ref,
                   q_ref, v_ref, gf_ref, gb_ref, gate_ref, *, layer):
    tm = h_ref.shape[1]
    sub = min(tm, SUB_TILE)
    for r in range(tm // sub):
        rows = slice(r * sub, (r + 1) * sub)
        q, v, gf, gb, gate = _rec_in_rows(h_ref[0, rows, :], nw_ref, w_ref, lbf_ref, lbb_ref,
                                          layer)
        q_ref[0, rows, :] = q.astype(BF16)
        v_ref[0, rows, :] = v.astype(BF16)
        gf_ref[0, rows, :] = gf
        gb_ref[0, rows, :] = gb
        gate_ref[0, rows, :] = gate.astype(BF16)


def _rec_in(h, nw, w_in, lb_f, lb_b, layer, tm):
    b, s, d = h.shape
    depth = lb_f.shape[0]
    tok = pl.BlockSpec((1, tm, d), lambda bi, i: (bi, i, 0))
    return pl.pallas_call(
        functools.partial(_rec_in_kernel, layer=layer),
        grid=(b, s // tm),
        in_specs=[tok, _resident((1, d)), _resident((d, w_in.shape[1])),
                  _resident((depth, d)), _resident((depth, d))],
        out_specs=[tok, tok, tok, tok, tok],
        out_shape=[
            jax.ShapeDtypeStruct((b, s, d), BF16),
            jax.ShapeDtypeStruct((b, s, d), BF16),
            jax.ShapeDtypeStruct((b, s, d), F32),
            jax.ShapeDtypeStruct((b, s, d), F32),
            jax.ShapeDtypeStruct((b, s, d), BF16),
        ],
        compiler_params=_params(2),
        name="rec_in",
    )(h, nw, w_in, lb_f, lb_b)


def _rec_in_meta_kernel(hm_ref, nw_ref, w_ref, lbf_ref, lbb_ref,
                        q_ref, v_ref, gf_ref, gb_ref, gate_ref, *, layer):
    b, n_meta, d = hm_ref.shape
    h = hm_ref[...].reshape(b * n_meta, d)
    q, v, gf, gb, gate = _rec_in_rows(h, nw_ref, w_ref, lbf_ref, lbb_ref, layer)
    pad = REC_CHUNK - n_meta
    for ref, val in ((q_ref, q), (v_ref, v), (gf_ref, gf), (gb_ref, gb)):
        ref[:, :pad, :] = jnp.zeros((b, pad, d), ref.dtype)
        ref[:, pad:, :] = val.reshape(b, n_meta, d).astype(ref.dtype)
    gate_ref[...] = gate.reshape(b, n_meta, d).astype(BF16)


def _rec_in_meta(hm, nw, w_in, lb_f, lb_b, layer):
    b, n_meta, d = hm.shape
    depth = lb_f.shape[0]
    chunk = pl.BlockSpec((b, REC_CHUNK, d), lambda i: (0, 0, 0))
    whole = pl.BlockSpec((b, n_meta, d), lambda i: (0, 0, 0))
    return pl.pallas_call(
        functools.partial(_rec_in_meta_kernel, layer=layer),
        grid=(1,),
        in_specs=[whole, _resident((1, d)), _resident((d, w_in.shape[1])),
                  _resident((depth, d)), _resident((depth, d))],
        out_specs=[chunk, chunk, chunk, chunk, whole],
        out_shape=[
            jax.ShapeDtypeStruct((b, REC_CHUNK, d), BF16),
            jax.ShapeDtypeStruct((b, REC_CHUNK, d), BF16),
            jax.ShapeDtypeStruct((b, REC_CHUNK, d), F32),
            jax.ShapeDtypeStruct((b, REC_CHUNK, d), F32),
            jax.ShapeDtypeStruct((b, n_meta, d), BF16),
        ],
        compiler_params=_params(1),
        name="rec_in_meta",
    )(hm, nw, w_in, lb_f, lb_b)


_SUM_LEVEL_HALVES = (32, 16, 8, 4, 2)
_N_SUM_LEVELS = len(_SUM_LEVEL_HALVES)
_MASK_ADJACENT = _N_SUM_LEVELS
_MASK_DIAG = _N_SUM_LEVELS + 1
_ROWS_Q_INTER = _N_SUM_LEVELS * REC_CHUNK
_ROWS_K_STATE = (_N_SUM_LEVELS + 1) * REC_CHUNK
_N_SUM_ROWS = (_N_SUM_LEVELS + 2) * REC_CHUNK
SCAN_CHUNKS_PER_STEP = 4
SCAN_PIPELINE_LAG = 8
SUBLANE_TILE = 8


def _scan_constants(backward):
    c = REC_CHUNK
    u = np.arange(c)
    w = (u[None, :] >= u[:, None]) if backward else (u[None, :] <= u[:, None])
    masks = np.zeros((_N_SUM_LEVELS + 2, c, c), np.float32)
    for lvl, half in enumerate(_SUM_LEVEL_HALVES + (1,)):
        for t in range(c):
            start = (t // (2 * half)) * 2 * half
            if t - start >= half:
                masks[lvl, t, start:start + half] = 1.0
    masks[_MASK_DIAG] = np.eye(c, dtype=np.float32)
    if backward:
        masks = masks[:, ::-1, ::-1]
    w = np.concatenate([w, w], axis=1).astype(np.float32)
    return jnp.asarray(w, BF16), jnp.asarray(np.ascontiguousarray(masks), F32)


def _split_decays(cum, backward):
    c, d = cum.shape
    tile = SUBLANE_TILE
    sub = lax.broadcasted_iota(jnp.int32, (tile, d), 0)
    sign4 = jnp.where((sub < 4) != backward, -1.0, 1.0).astype(F32)
    sign2 = jnp.where(((sub % 4) < 2) != backward, -1.0, 1.0).astype(F32)
    row = lambda r: cum[r:r + 1]
    tiles = [cum[t:t + tile] for t in range(0, c, tile)]
    out = []
    for half in _SUM_LEVEL_HALVES:
        for i, x in enumerate(tiles):
            t0 = i * tile
            if half >= tile:
                start = (t0 // (2 * half)) * 2 * half
                r = start + half if backward else start + half - 1
                after = (t0 - start >= half) != backward
                out.append(x - row(r) if after else row(r) - x)
            elif half == 4:
                r = t0 + 4 if backward else t0 + 3
                out.append((x - row(r)) * sign4)
            else:
                r1, r2 = (t0 + 2, t0 + 6) if backward else (t0 + 1, t0 + 5)
                out.append((x - jnp.where(sub < 4, row(r1), row(r2))) * sign2)
    out.extend(tiles)
    last = 0 if backward else c - 1
    out.extend(row(last) - x for x in tiles)
    return jnp.concatenate(out, axis=0)


class _ChunkJob:
    def __init__(self, d, q_ref, v_ref, g_ref, o_ref, rows, slot):
        self.d, self.q_ref, self.v_ref, self.g_ref, self.o_ref = d, q_ref, v_ref, g_ref, o_ref
        self.rows, self.slot = rows, slot


def _scan_kernel(qf_ref, vf_ref, gf_ref, qb_ref, vb_ref, gb_ref, qm_ref, vm_ref, gfm_ref, gbm_ref,
                 wf_ref, wb_ref, mf_ref, mb_ref, of_ref, ob_ref, ofm_ref, obm_ref,
                 st_ref, e_ref):
    c = REC_CHUNK
    n_heads = qf_ref.shape[-1] // HEAD_DIM
    n_chunks = qf_ref.shape[1] // c
    step = pl.program_id(1)
    w_refs = (wf_ref, wb_ref)
    m_refs = (mf_ref, mb_ref)
    whole_rows = (c - 1, 0)

    @pl.when(step == 0)
    def _():
        st_ref[...] = jnp.zeros(st_ref.shape, F32)

    def decay_stage(job):
        g2 = job.g_ref[0, job.rows, :]
        g_hi = g2.astype(BF16)
        g_lo = (g2 - g_hi.astype(F32)).astype(BF16)
        cum = _dot(w_refs[job.d][...], jnp.concatenate([g_hi, g_lo], axis=0))
        e_ref[job.slot, job.d] = jnp.exp2(_split_decays(cum, backward=job.d == 1))

    def pair_stage(job, hd):
        d, m_ref = job.d, m_refs[job.d]
        sl = slice(hd * HEAD_DIM, (hd + 1) * HEAD_DIM)
        e = lambda r0: e_ref[job.slot, d, r0:r0 + c, sl]
        q = job.q_ref[0, job.rows, sl].astype(F32)
        f = jnp.exp2(job.g_ref[0, job.rows, sl])
        k = 1.0 - f
        row_id = lax.broadcasted_iota(jnp.int32, (c, HEAD_DIM), 0)
        near = _dot_nt(jnp.concatenate([q, q * f], axis=0).astype(BF16), k.astype(BF16))
        pair = m_ref[_MASK_DIAG] * near[:c] + m_ref[_MASK_ADJACENT] * near[c:]
        for lvl, half in enumerate(_SUM_LEVEL_HALVES):
            is_query = ((row_id // half) % 2 == 1) != (d == 1)
            both = (jnp.where(is_query, q, k) * e(lvl * c)).astype(BF16)
            pair = pair + m_ref[lvl] * _dot_nt(both, both)
        q_in = (q * e(_ROWS_Q_INTER)).astype(BF16)
        k_out = (k * e(_ROWS_K_STATE)).astype(BF16)
        return pair.astype(BF16), q_in, k_out

    def output_stage(job, hd, pair, q_in, k_out):
        d = job.d
        sl = slice(hd * HEAD_DIM, (hd + 1) * HEAD_DIM)
        v = job.v_ref[0, job.rows, sl]
        state = st_ref[d, hd]
        o = _dot(pair, v) + _dot(q_in, state.astype(BF16))
        job.o_ref[0, job.rows, sl] = o.astype(BF16)
        row = _ROWS_Q_INTER + whole_rows[d]
        t0 = (row // SUBLANE_TILE) * SUBLANE_TILE
        decay_col = e_ref[job.slot, d, t0:t0 + SUBLANE_TILE, sl].T[:, row - t0:row - t0 + 1]
        st_ref[d, hd] = state * decay_col + _dot_tn(k_out, v)

    def run(jobs):
        pending = []
        for group in jobs:
            for job in group:
                decay_stage(job)
            for hd in range(n_heads):
                for job in group:
                    pending.append((job, hd) + pair_stage(job, hd))
                    if len(pending) > SCAN_PIPELINE_LAG:
                        output_stage(*pending.pop(0))
        for item in pending:
            output_stage(*item)

    whole = slice(0, c)

    @pl.when(step == 0)
    def _():
        run([[_ChunkJob(0, qm_ref, vm_ref, gfm_ref, ofm_ref, whole, 0)]])

    groups = []
    for i in range(n_chunks):
        fwd_rows = slice(i * c, (i + 1) * c)
        bwd_rows = slice((n_chunks - 1 - i) * c, (n_chunks - i) * c)
        groups.append([_ChunkJob(0, qf_ref, vf_ref, gf_ref, of_ref, fwd_rows, i),
                       _ChunkJob(1, qb_ref, vb_ref, gb_ref, ob_ref, bwd_rows, i)])
    run(groups)

    @pl.when(step == pl.num_programs(1) - 1)
    def _():
        run([[_ChunkJob(1, qm_ref, vm_ref, gbm_ref, obm_ref, whole, 0)]])


def _scan(q, v, gf, gb, qm, vm, gfm, gbm):
    b, s, d = q.shape
    n_heads = d // HEAD_DIM
    chunks_per_step = min(SCAN_CHUNKS_PER_STEP, s // REC_CHUNK)
    rows = chunks_per_step * REC_CHUNK
    n_steps = s // rows
    wf, mf = _scan_constants(False)
    wb, mb = _scan_constants(True)
    fwd = pl.BlockSpec((1, rows, d), lambda bi, j: (bi, j, 0))
    bwd = pl.BlockSpec((1, rows, d), lambda bi, j: (bi, n_steps - 1 - j, 0))
    meta = pl.BlockSpec((1, REC_CHUNK, d), lambda bi, j: (bi, 0, 0))
    return pl.pallas_call(
        _scan_kernel,
        grid=(b, n_steps),
        in_specs=[fwd, fwd, fwd, bwd, bwd, bwd, meta, meta, meta, meta,
                  _resident(wf.shape), _resident(wb.shape),
                  _resident(mf.shape), _resident(mb.shape)],
        out_specs=[fwd, bwd, meta, meta],
        out_shape=[jax.ShapeDtypeStruct((b, s, d), BF16)] * 2
        + [jax.ShapeDtypeStruct((b, REC_CHUNK, d), BF16)] * 2,
        scratch_shapes=[
            pltpu.VMEM((2, n_heads, HEAD_DIM, HEAD_DIM), F32),
            pltpu.VMEM((chunks_per_step, 2, _N_SUM_ROWS, d), F32),
        ],
        compiler_params=_params(2),
        name="gla_scan",
    )(q, v, gf, q, v, gb, qm, vm, gfm, gbm, wf, wb, mf, mb)


def _rope_tables(length):
    inv_freq = ROPE_THETA ** (-jnp.arange(0, HEAD_DIM, 2, dtype=F32) / HEAD_DIM)
    ang = jnp.arange(length, dtype=F32)[:, None] * inv_freq[None, :]
    cos, sin = jnp.cos(ang), jnp.sin(ang)
    return jnp.concatenate([cos, cos], axis=-1), jnp.concatenate([-sin, sin], axis=-1)


def kernel(x, meta_tokens, mix_norm_w, mlp_norm_w, attn_w_in, attn_w_out, attn_q_norm_w,
           attn_k_norm_w, attn_sink, rec_w_in, rec_w_out, rec_lb_fwd, rec_lb_bwd,
           rec_out_norm_w, mlp_w_up, mlp_w_down):
    b, s, d = x.shape
    depth = mix_norm_w.shape[0]
    n_mixers = 2
    tm = min(TOKEN_TILE, s)
    n_meta_rows = b * N_META
    assert d % HEAD_DIM == 0 and d // HEAD_DIM == N_KV_HEADS * GROUP, x.shape
    assert s % tm == 0 and s % ATTN_BLOCK == 0 and s % REC_CHUNK == 0, x.shape
    assert meta_tokens.shape == (N_META, d), meta_tokens.shape

    h = x
    hm = jnp.broadcast_to(meta_tokens.astype(x.dtype)[None], (b, N_META, d))
    cc, ss = _rope_tables(N_META + s)
    cc_meta, ss_meta = jnp.tile(cc[:N_META], (b, 1)), jnp.tile(ss[:N_META], (b, 1))
    cc_real, ss_real = cc[N_META:], ss[N_META:]
    row = lambda w: w.reshape(1, -1)
    flat_meta = lambda t: t.reshape(1, n_meta_rows, t.shape[-1])
    per_batch = lambda t: t.reshape(b, N_META, t.shape[-1])

    for layer in range(depth):
        j = layer // n_mixers
        nw = row(mix_norm_w[layer])
        mlp = (row(mlp_norm_w[layer]), mlp_w_up[layer].astype(BF16),
               mlp_w_down[layer].astype(BF16))
        if layer % n_mixers == 0:
            w_in = attn_w_in[j].astype(BF16)
            w_out = attn_w_out[j].astype(BF16)
            gains = (row(attn_q_norm_w[j]), row(attn_k_norm_w[j]))
            q, kv = _attn_in(h, nw, w_in, *gains, cc_real, ss_real, tm)
            qm, kvm = map(per_batch, _attn_in(
                flat_meta(hm), nw, w_in, *gains, cc_meta, ss_meta, n_meta_rows))
            sink = attn_sink[j].astype(F32)
            o = _attn(q, kv, kvm, sink)
            om = _attn_meta(qm, kvm, kv, sink)
            h = _post(h, o, w_out, *mlp, tm)
            hm = per_batch(_post(flat_meta(hm), flat_meta(om), w_out, *mlp, n_meta_rows))
        else:
            w_in = rec_w_in[j].astype(BF16)
            w_out = rec_w_out[j].astype(BF16)
            gain = row(rec_out_norm_w[j])
            q, v, gf, gb, gate = _rec_in(h, nw, w_in, rec_lb_fwd, rec_lb_bwd, layer, tm)
            qm, vm, gfm, gbm, gate_m = _rec_in_meta(hm, nw, w_in, rec_lb_fwd, rec_lb_bwd, layer)
            o_f, o_b, o_fm, o_bm = _scan(q, v, gf, gb, qm, vm, gfm, gbm)
            h = _post(h, (o_f, o_b, gate), w_out, *mlp, tm, rec_extra=gain)
            meta_rows = lambda t: flat_meta(t[:, REC_CHUNK - N_META:])
            hm = per_batch(_post(flat_meta(hm), (meta_rows(o_fm), meta_rows(o_bm),
                                                 flat_meta(gate_m)),
                                 w_out, *mlp, n_meta_rows, rec_extra=gain))
    return h
```

```python
import functools
import math

import numpy as np
import jax
import jax.numpy as jnp
from jax import lax
from jax.experimental import pallas as pl
from jax.experimental.pallas import tpu as pltpu

N_META = 16
HEAD_DIM = 128
N_KV_HEADS = 2
GROUP = 4
WINDOW = 128
ATTN_BLOCK = 128
ROPE_THETA = 10000.0
REC_CHUNK = 64
EPS = 1e-6
LOG2E = 1.4426950408889634

F32 = jnp.float32
BF16 = jnp.bfloat16

V7X_VMEM_BYTES = 64 * 1024 * 1024
V7X_VMEM_LIMIT_BYTES = V7X_VMEM_BYTES * 7 // 8
TOKEN_TILE = 512
FF_CHUNK = 1024
SUB_TILE = 256
REC_IN_TOKEN_TILE = 1024
REC_IN_SUB_TILE = 128
ATTN_BLOCKS_PER_STEP = 4


def _params(n_axes):
    return pltpu.CompilerParams(
        dimension_semantics=("arbitrary",) * n_axes,
        vmem_limit_bytes=V7X_VMEM_LIMIT_BYTES,
    )


def _resident(shape):
    zeros = (0,) * len(shape)
    return pl.BlockSpec(shape, lambda *_: zeros, pipeline_mode=pl.Buffered(1))


def _rms(x, w):
    ms = jnp.mean(x * x, axis=-1, keepdims=True)
    return x * lax.rsqrt(ms + EPS) * w


def _dot(a, b):
    return jnp.dot(a, b, preferred_element_type=F32)


def _dot_nt(a, b):
    return lax.dot_general(a, b, (((1,), (1,)), ((), ())), preferred_element_type=F32)


def _dot_tn(a, b):
    return lax.dot_general(a, b, (((0,), (0,)), ((), ())), preferred_element_type=F32)


def _attn_in_kernel(h_ref, nw_ref, w_ref, qg_ref, kg_ref, cc_ref, ss_ref, q_ref, kv_ref):
    n_q = q_ref.shape[-1] // HEAD_DIM
    n_kv = kv_ref.shape[-1] // (2 * HEAD_DIM)
    tm = h_ref.shape[1]
    sub = min(tm, SUB_TILE)
    half = HEAD_DIM // 2
    q_gain = qg_ref[...] * (HEAD_DIM ** -0.5 * LOG2E)
    k_gain = kg_ref[...]

    for r in range(tm // sub):
        rows = slice(r * sub, (r + 1) * sub)
        hn = _rms(h_ref[0, rows, :], nw_ref[...]).astype(BF16)
        qkv = _dot(hn, w_ref[...])
        cc = cc_ref[rows, :]
        ss = ss_ref[rows, :]
        tabs_q = (q_gain * cc, pltpu.roll(q_gain, half, 1) * ss)
        tabs_k = (k_gain * cc, pltpu.roll(k_gain, half, 1) * ss)

        def norm_rope(x, tabs):
            y = x * lax.rsqrt(jnp.mean(x * x, axis=-1, keepdims=True) + EPS)
            return y * tabs[0] + pltpu.roll(y, half, 1) * tabs[1]

        for j in range(n_q):
            sl = slice(j * HEAD_DIM, (j + 1) * HEAD_DIM)
            q_ref[0, rows, sl] = norm_rope(qkv[:, sl], tabs_q).astype(BF16)
        for j in range(n_kv):
            sl = slice(j * HEAD_DIM, (j + 1) * HEAD_DIM)
            src = slice((n_q + j) * HEAD_DIM, (n_q + j + 1) * HEAD_DIM)
            kv_ref[0, rows, sl] = norm_rope(qkv[:, src], tabs_k).astype(BF16)
        kv_ref[0, rows, n_kv * HEAD_DIM:] = qkv[:, (n_q + n_kv) * HEAD_DIM:].astype(BF16)


def _attn_in(h, nw, w_in, q_gain, k_gain, cc, ss, tm):
    bx, sx, d = h.shape
    n_in = w_in.shape[1]
    dq = d
    dkv = (n_in - dq) // 2
    grid = (bx, sx // tm)
    tok = lambda width: pl.BlockSpec((1, tm, width), lambda b, i: (b, i, 0))
    return pl.pallas_call(
        _attn_in_kernel,
        grid=grid,
        in_specs=[
            tok(d),
            _resident((1, d)),
            _resident((d, n_in)),
            _resident((1, HEAD_DIM)),
            _resident((1, HEAD_DIM)),
            pl.BlockSpec((tm, HEAD_DIM), lambda b, i: (i, 0)),
            pl.BlockSpec((tm, HEAD_DIM), lambda b, i: (i, 0)),
        ],
        out_specs=[tok(dq), tok(2 * dkv)],
        out_shape=[
            jax.ShapeDtypeStruct((bx, sx, dq), BF16),
            jax.ShapeDtypeStruct((bx, sx, 2 * dkv), BF16),
        ],
        compiler_params=_params(2),
        name="attn_in",
    )(h, nw, w_in, q_gain, k_gain, cc, ss)


def _softmax_pv(scores, values, sink_col):
    m = sink_col
    for s in scores:
        m = jnp.maximum(m, jnp.max(s, axis=-1, keepdims=True))
    den = jnp.exp2(sink_col - m)
    acc = None
    for s, v in zip(scores, values):
        p = jnp.exp2(s - m)
        den = den + jnp.sum(p, axis=-1, keepdims=True)
        pv = _dot(p.astype(BF16), v)
        acc = pv if acc is None else acc + pv
    return acc / den


def _attn_kernel(sink_ref, q_ref, kvp_ref, kvc_ref, kvn_ref, kvm_ref, o_ref, *, n_steps):
    step = pl.program_id(1)
    blk = ATTN_BLOCK
    n_blk = q_ref.shape[1] // blk
    v_off = N_KV_HEADS * HEAD_DIM
    a = lax.broadcasted_iota(jnp.int32, (GROUP * blk, 3 * blk), 0) % blk
    c = lax.broadcasted_iota(jnp.int32, (GROUP * blk, 3 * blk), 1)
    in_prev = (c < blk) & (c >= a)
    in_own = (c >= blk) & (c < 2 * blk)
    in_next = (c >= 2 * blk) & (c - 2 * blk <= a)
    interior = in_prev | in_own | in_next
    neg_inf = jnp.float32(-jnp.inf)
    lane = lax.broadcasted_iota(jnp.int32, (GROUP * blk, blk), 1)
    meta_lane = lane < N_META
    meta_fill = []
    for hk in range(N_KV_HEADS):
        sink_tile = jnp.concatenate(
            [jnp.full((blk, blk), sink_ref[hk * GROUP + g] * LOG2E, F32) for g in range(GROUP)],
            axis=0)
        meta_fill.append(jnp.where(lane == N_META, sink_tile, neg_inf))
    kv_blocks = ([kvp_ref.at[0]] + [kvc_ref.at[0, pl.ds(i * blk, blk), :] for i in range(n_blk)]
                 + [kvn_ref.at[0]])
    pad = jnp.zeros((blk - N_META, HEAD_DIM), BF16)

    def scores(i, hk):
        rows = slice(i * blk, (i + 1) * blk)
        mask = interior
        if i == 0:
            mask = (in_prev & (step > 0)) | in_own | in_next
        if i == n_blk - 1:
            mask = mask & (in_prev | in_own | (step < n_steps - 1))
        ks = slice(hk * HEAD_DIM, (hk + 1) * HEAD_DIM)
        keys = jnp.concatenate([r[:, ks] for r in kv_blocks[i:i + 3]], axis=0)
        q4 = jnp.concatenate(
            [q_ref[0, rows, h * HEAD_DIM:(h + 1) * HEAD_DIM]
             for h in range(hk * GROUP, (hk + 1) * GROUP)], axis=0)
        s_band = jnp.where(mask, _dot_nt(q4, keys), neg_inf)
        km = jnp.concatenate([kvm_ref[0, :, ks], pad], axis=0)
        s_meta = jnp.where(meta_lane, _dot_nt(q4, km), meta_fill[hk])
        return [s_band[:, t * blk:(t + 1) * blk] for t in range(3)] + [s_meta]

    def finish(i, hk, tiles):
        rows = slice(i * blk, (i + 1) * blk)
        vs = slice(v_off + hk * HEAD_DIM, v_off + (hk + 1) * HEAD_DIM)
        vals = jnp.concatenate([r[:, vs] for r in kv_blocks[i:i + 3]] + [kvm_ref[0, :, vs], pad],
                               axis=0)
        m = jnp.max(functools.reduce(jnp.maximum, tiles), axis=-1, keepdims=True)
        p = jnp.concatenate([jnp.exp2(t - m) for t in tiles], axis=1).astype(BF16)
        o4 = _dot(p, vals) / jnp.sum(p.astype(F32), axis=-1, keepdims=True)
        for g in range(GROUP):
            h = hk * GROUP + g
            o_ref[0, rows, h * HEAD_DIM:(h + 1) * HEAD_DIM] = o4[g * blk:(g + 1) * blk].astype(BF16)

    previous = None
    for i in range(n_blk):
        for hk in range(N_KV_HEADS):
            current = (i, hk, scores(i, hk))
            if previous is not None:
                finish(*previous)
            previous = current
    finish(*previous)


def _attn(q, kv, kvm, sink):
    b, s, dq = q.shape
    dkv2 = kv.shape[-1]
    blk = ATTN_BLOCK
    n_blk = min(ATTN_BLOCKS_PER_STEP, s // blk)
    n_steps = s // (n_blk * blk)
    last_blk = s // blk - 1
    prev = pl.BlockSpec((1, blk, dkv2), lambda bi, n: (bi, jnp.maximum(n_blk * n - 1, 0), 0))
    own = pl.BlockSpec((1, n_blk * blk, dkv2), lambda bi, n: (bi, n, 0))
    nxt = pl.BlockSpec((1, blk, dkv2),
                       lambda bi, n: (bi, jnp.minimum(n_blk * (n + 1), last_blk), 0))
    meta = pl.BlockSpec((1, N_META, dkv2), lambda bi, n: (bi, 0, 0))
    return pl.pallas_call(
        functools.partial(_attn_kernel, n_steps=n_steps),
        grid=(b, n_steps),
        in_specs=[
            pl.BlockSpec(memory_space=pltpu.SMEM),
            pl.BlockSpec((1, n_blk * blk, dq), lambda bi, n: (bi, n, 0)),
            prev, own, nxt, meta,
        ],
        out_specs=pl.BlockSpec((1, n_blk * blk, dq), lambda bi, n: (bi, n, 0)),
        out_shape=jax.ShapeDtypeStruct((b, s, dq), BF16),
        compiler_params=_params(2),
        name="band_attn",
    )(sink, q, kv, kv, kv, kvm)


def _attn_meta_kernel(sink_ref, qm_ref, kvm_ref, kvr_ref, o_ref):
    rows = GROUP * N_META
    v_off = N_KV_HEADS * HEAD_DIM
    i = lax.broadcasted_iota(jnp.int32, (rows, ATTN_BLOCK), 0) % N_META
    j = lax.broadcasted_iota(jnp.int32, (rows, ATTN_BLOCK), 1)
    allowed = (N_META + j) - i <= WINDOW
    neg_inf = jnp.float32(-jnp.inf)
    for hk in range(N_KV_HEADS):
        ks = slice(hk * HEAD_DIM, (hk + 1) * HEAD_DIM)
        vs = slice(v_off + hk * HEAD_DIM, v_off + (hk + 1) * HEAD_DIM)
        heads = [hk * GROUP + g for g in range(GROUP)]
        q4 = jnp.concatenate(
            [qm_ref[0, :, h * HEAD_DIM:(h + 1) * HEAD_DIM] for h in heads], axis=0)
        s_meta = _dot_nt(q4, kvm_ref[0, :, ks])
        s_real = jnp.where(allowed, _dot_nt(q4, kvr_ref[0, :, ks]), neg_inf)
        sink_col = jnp.concatenate(
            [jnp.full((N_META, 1), sink_ref[h] * LOG2E, F32) for h in heads], axis=0)
        o4 = _softmax_pv([s_meta, s_real], [kvm_ref[0, :, vs], kvr_ref[0, :, vs]], sink_col)
        for g, h in enumerate(heads):
            o_ref[0, :, h * HEAD_DIM:(h + 1) * HEAD_DIM] = (
                o4[g * N_META:(g + 1) * N_META].astype(BF16))


def _attn_meta(qm, kvm, kv, sink):
    b, _, dq = qm.shape
    dkv2 = kvm.shape[-1]
    meta = lambda width: pl.BlockSpec((1, N_META, width), lambda bi: (bi, 0, 0))
    first = pl.BlockSpec((1, ATTN_BLOCK, dkv2), lambda bi: (bi, 0, 0))
    return pl.pallas_call(
        _attn_meta_kernel,
        grid=(b,),
        in_specs=[pl.BlockSpec(memory_space=pltpu.SMEM), meta(dq), meta(dkv2), first],
        out_specs=meta(dq),
        out_shape=jax.ShapeDtypeStruct((b, N_META, dq), BF16),
        compiler_params=_params(1),
        name="meta_attn",
    )(sink, qm, kvm, kv)


def _post_kernel(*refs, rec, ff_chunk):
    if rec:
        h_ref, of_ref, ob_ref, gate_ref, og_ref, wo_ref, nw_ref, wu_ref, wd_ref, out_ref = refs
        o = of_ref[0].astype(F32) + ob_ref[0].astype(F32)
        n_heads = o.shape[-1] // HEAD_DIM
        o = jnp.concatenate(
            [_rms(o[:, j * HEAD_DIM:(j + 1) * HEAD_DIM], og_ref[...]) for j in range(n_heads)],
            axis=1)
        o = (o * gate_ref[0].astype(F32)).astype(BF16)
    else:
        h_ref, o_ref, wo_ref, nw_ref, wu_ref, wd_ref, out_ref = refs
        o = o_ref[0]
    h1 = h_ref[0] + _dot(o, wo_ref[...])
    hn = _rms(h1, nw_ref[...]).astype(BF16)
    d_ff = wu_ref.shape[1]
    acc = h1
    for c in range(d_ff // ff_chunk):
        sl = slice(c * ff_chunk, (c + 1) * ff_chunk)
        z = jnp.maximum(_dot(hn, wu_ref[:, sl]), 0.0)
        acc = acc + _dot((z * z).astype(BF16), wd_ref[sl, :])
    out_ref[0] = acc


def _post(h, mixer_out, w_out, nw, w_up, w_down, tm, rec_extra=None):
    bx, sx, d = h.shape
    d_ff = w_up.shape[1]
    tok = pl.BlockSpec((1, tm, d), lambda b, i: (b, i, 0))
    rec = rec_extra is not None
    if rec:
        ins = [h, *mixer_out, rec_extra, w_out, nw, w_up, w_down]
        specs = [tok, tok, tok, tok, _resident((1, HEAD_DIM))]
    else:
        ins = [h, mixer_out, w_out, nw, w_up, w_down]
        specs = [tok, tok]
    specs += [_resident((d, d)), _resident((1, d)), _resident((d, d_ff)), _resident((d_ff, d))]
    return pl.pallas_call(
        functools.partial(_post_kernel, rec=rec, ff_chunk=min(FF_CHUNK, d_ff)),
        grid=(bx, sx // tm),
        in_specs=specs,
        out_specs=tok,
        out_shape=jax.ShapeDtypeStruct((bx, sx, d), F32),
        compiler_params=_params(2),
        name="post_rec" if rec else "post_attn",
    )(*ins)


def _lower_bound(tab_ref, layer):
    t = tab_ref[...]
    e = jnp.exp(t - jnp.max(t, axis=0, keepdims=True))
    p = e / jnp.sum(e, axis=0, keepdims=True)
    return jnp.sum(p[1:layer + 1], axis=0, keepdims=True)


def _rec_in_rows(h, nw_ref, w_ref, lbf_ref, lbb_ref, layer):
    d = h.shape[-1]
    hn = _rms(h, nw_ref[...]).astype(BF16)
    proj = lambda j: _dot(hn, w_ref[:, j * d:(j + 1) * d])

    def log_forget(z, tab_ref):
        lb = _lower_bound(tab_ref, layer)
        return jnp.log2(lb + (1.0 - lb) * jax.nn.sigmoid(z))

    gf = log_forget(proj(2), lbf_ref)
    gb = log_forget(proj(3), lbb_ref)
    zq = proj(0)
    q = zq * jax.nn.sigmoid(zq)
    zg = proj(4)
    gate = zg * jax.nn.sigmoid(zg)
    v = proj(1)
    return q, v, gf, gb, gate


def _rec_in_kernel(h_ref, nw_ref, w_ref, lbf_ref, lbb_ref,
                   q_ref, v_ref, gf_ref, gb_ref, gate_ref, *, layer):
    tm = h_ref.shape[1]
    sub = min(tm, REC_IN_SUB_TILE)
    for r in range(tm // sub):
        rows = slice(r * sub, (r + 1) * sub)
        q, v, gf, gb, gate = _rec_in_rows(h_ref[0, rows, :], nw_ref, w_ref, lbf_ref, lbb_ref,
                                          layer)
        q_ref[0, rows, :] = q.astype(BF16)
        v_ref[0, rows, :] = v.astype(BF16)
        gf_ref[0, rows, :] = gf
        gb_ref[0, rows, :] = gb
        gate_ref[0, rows, :] = gate.astype(BF16)


def _rec_in(h, nw, w_in, lb_f, lb_b, layer, tm):
    b, s, d = h.shape
    depth = lb_f.shape[0]
    tok = pl.BlockSpec((1, tm, d), lambda bi, i: (bi, i, 0))
    return pl.pallas_call(
        functools.partial(_rec_in_kernel, layer=layer),
        grid=(b, s // tm),
        in_specs=[tok, _resident((1, d)), _resident((d, w_in.shape[1])),
                  _resident((depth, d)), _resident((depth, d))],
        out_specs=[tok, tok, tok, tok, tok],
        out_shape=[
            jax.ShapeDtypeStruct((b, s, d), BF16),
            jax.ShapeDtypeStruct((b, s, d), BF16),
            jax.ShapeDtypeStruct((b, s, d), F32),
            jax.ShapeDtypeStruct((b, s, d), F32),
            jax.ShapeDtypeStruct((b, s, d), BF16),
        ],
        compiler_params=_params(2),
        name="rec_in",
    )(h, nw, w_in, lb_f, lb_b)


def _rec_in_meta_kernel(hm_ref, nw_ref, w_ref, lbf_ref, lbb_ref,
                        q_ref, v_ref, gf_ref, gb_ref, gate_ref, *, layer):
    b, n_meta, d = hm_ref.shape
    h = hm_ref[...].reshape(b * n_meta, d)
    q, v, gf, gb, gate = _rec_in_rows(h, nw_ref, w_ref, lbf_ref, lbb_ref, layer)
    pad = REC_CHUNK - n_meta
    for ref, val in ((q_ref, q), (v_ref, v), (gf_ref, gf), (gb_ref, gb)):
        ref[:, :pad, :] = jnp.zeros((b, pad, d), ref.dtype)
        ref[:, pad:, :] = val.reshape(b, n_meta, d).astype(ref.dtype)
    gate_ref[...] = gate.reshape(b, n_meta, d).astype(BF16)


def _rec_in_meta(hm, nw, w_in, lb_f, lb_b, layer):
    b, n_meta, d = hm.shape
    depth = lb_f.shape[0]
    chunk = pl.BlockSpec((b, REC_CHUNK, d), lambda i: (0, 0, 0))
    whole = pl.BlockSpec((b, n_meta, d), lambda i: (0, 0, 0))
    return pl.pallas_call(
        functools.partial(_rec_in_meta_kernel, layer=layer),
        grid=(1,),
        in_specs=[whole, _resident((1, d)), _resident((d, w_in.shape[1])),
                  _resident((depth, d)), _resident((depth, d))],
        out_specs=[chunk, chunk, chunk, chunk, whole],
        out_shape=[
            jax.ShapeDtypeStruct((b, REC_CHUNK, d), BF16),
            jax.ShapeDtypeStruct((b, REC_CHUNK, d), BF16),
            jax.ShapeDtypeStruct((b, REC_CHUNK, d), F32),
            jax.ShapeDtypeStruct((b, REC_CHUNK, d), F32),
            jax.ShapeDtypeStruct((b, n_meta, d), BF16),
        ],
        compiler_params=_params(1),
        name="rec_in_meta",
    )(hm, nw, w_in, lb_f, lb_b)


_SUM_LEVEL_HALVES = (32, 16, 8, 4, 2)
_N_SUM_LEVELS = len(_SUM_LEVEL_HALVES)
_MASK_ADJACENT = _N_SUM_LEVELS
_MASK_DIAG = _N_SUM_LEVELS + 1
_ROWS_Q_INTER = _N_SUM_LEVELS * REC_CHUNK
_ROWS_K_STATE = (_N_SUM_LEVELS + 1) * REC_CHUNK
_N_SUM_ROWS = (_N_SUM_LEVELS + 2) * REC_CHUNK
SCAN_CHUNKS_PER_STEP = 4
SCAN_PIPELINE_LAG = 8
SUBLANE_TILE = 8


def _scan_constants(backward):
    c = REC_CHUNK
    u = np.arange(c)
    w = (u[None, :] >= u[:, None]) if backward else (u[None, :] <= u[:, None])
    masks = np.zeros((_N_SUM_LEVELS + 2, c, c), np.float32)
    for lvl, half in enumerate(_SUM_LEVEL_HALVES + (1,)):
        for t in range(c):
            start = (t // (2 * half)) * 2 * half
            if t - start >= half:
                masks[lvl, t, start:start + half] = 1.0
    masks[_MASK_DIAG] = np.eye(c, dtype=np.float32)
    if backward:
        masks = masks[:, ::-1, ::-1]
    w = np.concatenate([w, w], axis=1).astype(np.float32)
    return jnp.asarray(w, BF16), jnp.asarray(np.ascontiguousarray(masks), F32)


def _split_decays(cum, backward):
    c, d = cum.shape
    tile = SUBLANE_TILE
    sub = lax.broadcasted_iota(jnp.int32, (tile, d), 0)
    sign4 = jnp.where((sub < 4) != backward, -1.0, 1.0).astype(F32)
    sign2 = jnp.where(((sub % 4) < 2) != backward, -1.0, 1.0).astype(F32)
    row = lambda r: cum[r:r + 1]
    tiles = [cum[t:t + tile] for t in range(0, c, tile)]
    out = []
    for half in _SUM_LEVEL_HALVES:
        for i, x in enumerate(tiles):
            t0 = i * tile
            if half >= tile:
                start = (t0 // (2 * half)) * 2 * half
                r = start + half if backward else start + half - 1
                after = (t0 - start >= half) != backward
                out.append(x - row(r) if after else row(r) - x)
            elif half == 4:
                r = t0 + 4 if backward else t0 + 3
                out.append((x - row(r)) * sign4)
            else:
                r1, r2 = (t0 + 2, t0 + 6) if backward else (t0 + 1, t0 + 5)
                out.append((x - jnp.where(sub < 4, row(r1), row(r2))) * sign2)
    out.extend(tiles)
    last = 0 if backward else c - 1
    out.extend(row(last) - x for x in tiles)
    return jnp.concatenate(out, axis=0)


class _ChunkJob:
    def __init__(self, d, q_ref, v_ref, g_ref, o_ref, rows, slot):
        self.d, self.q_ref, self.v_ref, self.g_ref, self.o_ref = d, q_ref, v_ref, g_ref, o_ref
        self.rows, self.slot = rows, slot


def _scan_kernel(qf_ref, vf_ref, gf_ref, qb_ref, vb_ref, gb_ref, qm_ref, vm_ref, gfm_ref, gbm_ref,
                 wf_ref, wb_ref, mf_ref, mb_ref, of_ref, ob_ref, ofm_ref, obm_ref,
                 st_ref, e_ref):
    c = REC_CHUNK
    n_heads = qf_ref.shape[-1] // HEAD_DIM
    n_chunks = qf_ref.shape[1] // c
    step = pl.program_id(1)
    w_refs = (wf_ref, wb_ref)
    m_refs = (mf_ref, mb_ref)
    whole_rows = (c - 1, 0)

    @pl.when(step == 0)
    def _():
        st_ref[...] = jnp.zeros(st_ref.shape, F32)

    def decay_stage(job):
        g2 = job.g_ref[0, job.rows, :]
        g_hi = g2.astype(BF16)
        g_lo = (g2 - g_hi.astype(F32)).astype(BF16)
        cum = _dot(w_refs[job.d][...], jnp.concatenate([g_hi, g_lo], axis=0))
        e_ref[job.slot, job.d] = jnp.exp2(_split_decays(cum, backward=job.d == 1))

    def pair_stage(job, hd):
        d, m_ref = job.d, m_refs[job.d]
        sl = slice(hd * HEAD_DIM, (hd + 1) * HEAD_DIM)
        e = lambda r0: e_ref[job.slot, d, r0:r0 + c, sl]
        q = job.q_ref[0, job.rows, sl].astype(F32)
        f = jnp.exp2(job.g_ref[0, job.rows, sl])
        k = 1.0 - f
        row_id = lax.broadcasted_iota(jnp.int32, (c, HEAD_DIM), 0)
        near = _dot_nt(jnp.concatenate([q, q * f], axis=0).astype(BF16), k.astype(BF16))
        pair = m_ref[_MASK_DIAG] * near[:c] + m_ref[_MASK_ADJACENT] * near[c:]
        for lvl, half in enumerate(_SUM_LEVEL_HALVES):
            is_query = ((row_id // half) % 2 == 1) != (d == 1)
            both = (jnp.where(is_query, q, k) * e(lvl * c)).astype(BF16)
            pair = pair + m_ref[lvl] * _dot_nt(both, both)
        q_in = (q * e(_ROWS_Q_INTER)).astype(BF16)
        k_out = (k * e(_ROWS_K_STATE)).astype(BF16)
        return pair.astype(BF16), q_in, k_out

    def output_stage(job, hd, pair, q_in, k_out):
        d = job.d
        sl = slice(hd * HEAD_DIM, (hd + 1) * HEAD_DIM)
        v = job.v_ref[0, job.rows, sl]
        state = st_ref[d, hd]
        o = _dot(pair, v) + _dot(q_in, state.astype(BF16))
        job.o_ref[0, job.rows, sl] = o.astype(BF16)
        row = _ROWS_Q_INTER + whole_rows[d]
        t0 = (row // SUBLANE_TILE) * SUBLANE_TILE
        decay_col = e_ref[job.slot, d, t0:t0 + SUBLANE_TILE, sl].T[:, row - t0:row - t0 + 1]
        st_ref[d, hd] = state * decay_col + _dot_tn(k_out, v)

    def run(jobs):
        pending = []
        for group in jobs:
            for job in group:
                decay_stage(job)
            for hd in range(n_heads):
                for job in group:
                    pending.append((job, hd) + pair_stage(job, hd))
                    if len(pending) > SCAN_PIPELINE_LAG:
                        output_stage(*pending.pop(0))
        for item in pending:
            output_stage(*item)

    whole = slice(0, c)

    @pl.when(step == 0)
    def _():
        run([[_ChunkJob(0, qm_ref, vm_ref, gfm_ref, ofm_ref, whole, 0)]])

    groups = []
    for i in range(n_chunks):
        fwd_rows = slice(i * c, (i + 1) * c)
        bwd_rows = slice((n_chunks - 1 - i) * c, (n_chunks - i) * c)
        groups.append([_ChunkJob(0, qf_ref, vf_ref, gf_ref, of_ref, fwd_rows, i),
                       _ChunkJob(1, qb_ref, vb_ref, gb_ref, ob_ref, bwd_rows, i)])
    run(groups)

    @pl.when(step == pl.num_programs(1) - 1)
    def _():
        run([[_ChunkJob(1, qm_ref, vm_ref, gbm_ref, obm_ref, whole, 0)]])


def _scan(q, v, gf, gb, qm, vm, gfm, gbm):
    b, s, d = q.shape
    n_heads = d // HEAD_DIM
    chunks_per_step = min(SCAN_CHUNKS_PER_STEP, s // REC_CHUNK)
    rows = chunks_per_step * REC_CHUNK
    n_steps = s // rows
    wf, mf = _scan_constants(False)
    wb, mb = _scan_constants(True)
    fwd = pl.BlockSpec((1, rows, d), lambda bi, j: (bi, j, 0))
    bwd = pl.BlockSpec((1, rows, d), lambda bi, j: (bi, n_steps - 1 - j, 0))
    meta = pl.BlockSpec((1, REC_CHUNK, d), lambda bi, j: (bi, 0, 0))
    return pl.pallas_call(
        _scan_kernel,
        grid=(b, n_steps),
        in_specs=[fwd, fwd, fwd, bwd, bwd, bwd, meta, meta, meta, meta,
                  _resident(wf.shape), _resident(wb.shape),
                  _resident(mf.shape), _resident(mb.shape)],
        out_specs=[fwd, bwd, meta, meta],
        out_shape=[jax.ShapeDtypeStruct((b, s, d), BF16)] * 2
        + [jax.ShapeDtypeStruct((b, REC_CHUNK, d), BF16)] * 2,
        scratch_shapes=[
            pltpu.VMEM((2, n_heads, HEAD_DIM, HEAD_DIM), F32),
            pltpu.VMEM((chunks_per_step, 2, _N_SUM_ROWS, d), F32),
        ],
        compiler_params=_params(2),
        name="gla_scan",
    )(q, v, gf, q, v, gb, qm, vm, gfm, gbm, wf, wb, mf, mb)


def _rope_tables(length):
    inv_freq = ROPE_THETA ** (-jnp.arange(0, HEAD_DIM, 2, dtype=F32) / HEAD_DIM)
    ang = jnp.arange(length, dtype=F32)[:, None] * inv_freq[None, :]
    cos, sin = jnp.cos(ang), jnp.sin(ang)
    return jnp.concatenate([cos, cos], axis=-1), jnp.concatenate([-sin, sin], axis=-1)


def kernel(x, meta_tokens, mix_norm_w, mlp_norm_w, attn_w_in, attn_w_out, attn_q_norm_w,
           attn_k_norm_w, attn_sink, rec_w_in, rec_w_out, rec_lb_fwd, rec_lb_bwd,
           rec_out_norm_w, mlp_w_up, mlp_w_down):
    b, s, d = x.shape
    depth = mix_norm_w.shape[0]
    n_mixers = 2
    tm = min(TOKEN_TILE, s)
    n_meta_rows = b * N_META
    assert d % HEAD_DIM == 0 and d // HEAD_DIM == N_KV_HEADS * GROUP, x.shape
    assert s % tm == 0 and s % ATTN_BLOCK == 0 and s % REC_CHUNK == 0, x.shape
    assert meta_tokens.shape == (N_META, d), meta_tokens.shape

    h = x
    hm = jnp.broadcast_to(meta_tokens.astype(x.dtype)[None], (b, N_META, d))
    cc, ss = _rope_tables(N_META + s)
    cc_meta, ss_meta = jnp.tile(cc[:N_META], (b, 1)), jnp.tile(ss[:N_META], (b, 1))
    cc_real, ss_real = cc[N_META:], ss[N_META:]
    row = lambda w: w.reshape(1, -1)
    flat_meta = lambda t: t.reshape(1, n_meta_rows, t.shape[-1])
    per_batch = lambda t: t.reshape(b, N_META, t.shape[-1])

    for layer in range(depth):
        j = layer // n_mixers
        nw = row(mix_norm_w[layer])
        mlp = (row(mlp_norm_w[layer]), mlp_w_up[layer].astype(BF16),
               mlp_w_down[layer].astype(BF16))
        if layer % n_mixers == 0:
            w_in = attn_w_in[j].astype(BF16)
            w_out = attn_w_out[j].astype(BF16)
            gains = (row(attn_q_norm_w[j]), row(attn_k_norm_w[j]))
            q, kv = _attn_in(h, nw, w_in, *gains, cc_real, ss_real, tm)
            qm, kvm = map(per_batch, _attn_in(
                flat_meta(hm), nw, w_in, *gains, cc_meta, ss_meta, n_meta_rows))
            sink = attn_sink[j].astype(F32)
            o = _attn(q, kv, kvm, sink)
            om = _attn_meta(qm, kvm, kv, sink)
            h = _post(h, o, w_out, *mlp, tm)
            hm = per_batch(_post(flat_meta(hm), flat_meta(om), w_out, *mlp, n_meta_rows))
        else:
            w_in = rec_w_in[j].astype(BF16)
            w_out = rec_w_out[j].astype(BF16)
            gain = row(rec_out_norm_w[j])
            q, v, gf, gb, gate = _rec_in(h, nw, w_in, rec_lb_fwd, rec_lb_bwd, layer,
                                         math.gcd(REC_IN_TOKEN_TILE, s))
            qm, vm, gfm, gbm, gate_m = _rec_in_meta(hm, nw, w_in, rec_lb_fwd, rec_lb_bwd, layer)
            o_f, o_b, o_fm, o_bm = _scan(q, v, gf, gb, qm, vm, gfm, gbm)
            h = _post(h, (o_f, o_b, gate), w_out, *mlp, tm, rec_extra=gain)
            meta_rows = lambda t: flat_meta(t[:, REC_CHUNK - N_META:])
            hm = per_batch(_post(flat_meta(hm), (meta_rows(o_fm), meta_rows(o_bm),
                                                 flat_meta(gate_m)),
                                 w_out, *mlp, n_meta_rows, rec_extra=gain))
    return h
```

```python
import functools
import math

import numpy as np
import jax
import jax.numpy as jnp
from jax import lax
from jax.experimental import pallas as pl
from jax.experimental.pallas import tpu as pltpu

N_META = 16
HEAD_DIM = 128
N_KV_HEADS = 2
GROUP = 4
WINDOW = 128
ATTN_BLOCK = 128
ROPE_THETA = 10000.0
REC_CHUNK = 64
EPS = 1e-6
LOG2E = 1.4426950408889634

F32 = jnp.float32
BF16 = jnp.bfloat16

V7X_VMEM_BYTES = 64 * 1024 * 1024
V7X_VMEM_LIMIT_BYTES = V7X_VMEM_BYTES * 7 // 8
TOKEN_TILE = 512
FF_CHUNK = 1024
SUB_TILE = 256
REC_IN_TOKEN_TILE = 1024
REC_IN_SUB_TILE = 128
ATTN_BLOCKS_PER_STEP = 4


def _params(n_axes, n_inputs=0, weight_inputs=()):
    fusion = [i in weight_inputs for i in range(n_inputs)] if weight_inputs else None
    return pltpu.CompilerParams(
        dimension_semantics=("arbitrary",) * n_axes,
        vmem_limit_bytes=V7X_VMEM_LIMIT_BYTES,
        allow_input_fusion=fusion,
    )


def _resident(shape):
    zeros = (0,) * len(shape)
    return pl.BlockSpec(shape, lambda *_: zeros, pipeline_mode=pl.Buffered(1))


def _rms(x, w):
    ms = jnp.mean(x * x, axis=-1, keepdims=True)
    return x * lax.rsqrt(ms + EPS) * w


def _dot(a, b):
    return jnp.dot(a, b, preferred_element_type=F32)


def _dot_nt(a, b):
    return lax.dot_general(a, b, (((1,), (1,)), ((), ())), preferred_element_type=F32)


def _dot_tn(a, b):
    return lax.dot_general(a, b, (((0,), (0,)), ((), ())), preferred_element_type=F32)


def _attn_in_kernel(h_ref, nw_ref, w_ref, qg_ref, kg_ref, cc_ref, ss_ref, q_ref, kv_ref):
    n_q = q_ref.shape[-1] // HEAD_DIM
    n_kv = kv_ref.shape[-1] // (2 * HEAD_DIM)
    tm = h_ref.shape[1]
    sub = min(tm, SUB_TILE)
    half = HEAD_DIM // 2
    q_gain = qg_ref[...] * (HEAD_DIM ** -0.5 * LOG2E)
    k_gain = kg_ref[...]

    for r in range(tm // sub):
        rows = slice(r * sub, (r + 1) * sub)
        hn = _rms(h_ref[0, rows, :], nw_ref[...]).astype(BF16)
        qkv = _dot(hn, w_ref[...])
        cc = cc_ref[rows, :]
        ss = ss_ref[rows, :]
        tabs_q = (q_gain * cc, pltpu.roll(q_gain, half, 1) * ss)
        tabs_k = (k_gain * cc, pltpu.roll(k_gain, half, 1) * ss)

        def norm_rope(x, tabs):
            y = x * lax.rsqrt(jnp.mean(x * x, axis=-1, keepdims=True) + EPS)
            return y * tabs[0] + pltpu.roll(y, half, 1) * tabs[1]

        for j in range(n_q):
            sl = slice(j * HEAD_DIM, (j + 1) * HEAD_DIM)
            q_ref[0, rows, sl] = norm_rope(qkv[:, sl], tabs_q).astype(BF16)
        for j in range(n_kv):
            sl = slice(j * HEAD_DIM, (j + 1) * HEAD_DIM)
            src = slice((n_q + j) * HEAD_DIM, (n_q + j + 1) * HEAD_DIM)
            kv_ref[0, rows, sl] = norm_rope(qkv[:, src], tabs_k).astype(BF16)
        kv_ref[0, rows, n_kv * HEAD_DIM:] = qkv[:, (n_q + n_kv) * HEAD_DIM:].astype(BF16)


def _attn_in(h, nw, w_in, q_gain, k_gain, cc, ss, tm):
    bx, sx, d = h.shape
    n_in = w_in.shape[1]
    dq = d
    dkv = (n_in - dq) // 2
    grid = (bx, sx // tm)
    tok = lambda width: pl.BlockSpec((1, tm, width), lambda b, i: (b, i, 0))
    return pl.pallas_call(
        _attn_in_kernel,
        grid=grid,
        in_specs=[
            tok(d),
            _resident((1, d)),
            _resident((d, n_in)),
            _resident((1, HEAD_DIM)),
            _resident((1, HEAD_DIM)),
            pl.BlockSpec((tm, HEAD_DIM), lambda b, i: (i, 0)),
            pl.BlockSpec((tm, HEAD_DIM), lambda b, i: (i, 0)),
        ],
        out_specs=[tok(dq), tok(2 * dkv)],
        out_shape=[
            jax.ShapeDtypeStruct((bx, sx, dq), BF16),
            jax.ShapeDtypeStruct((bx, sx, 2 * dkv), BF16),
        ],
        compiler_params=_params(2, n_inputs=7, weight_inputs=(2,)),
        name="attn_in",
    )(h, nw, w_in, q_gain, k_gain, cc, ss)


def _softmax_pv(scores, values, sink_col):
    m = sink_col
    for s in scores:
        m = jnp.maximum(m, jnp.max(s, axis=-1, keepdims=True))
    den = jnp.exp2(sink_col - m)
    acc = None
    for s, v in zip(scores, values):
        p = jnp.exp2(s - m)
        den = den + jnp.sum(p, axis=-1, keepdims=True)
        pv = _dot(p.astype(BF16), v)
        acc = pv if acc is None else acc + pv
    return acc / den


def _attn_kernel(sink_ref, q_ref, kvp_ref, kvc_ref, kvn_ref, kvm_ref, o_ref, *, n_steps):
    step = pl.program_id(1)
    blk = ATTN_BLOCK
    n_blk = q_ref.shape[1] // blk
    v_off = N_KV_HEADS * HEAD_DIM
    a = lax.broadcasted_iota(jnp.int32, (GROUP * blk, 3 * blk), 0) % blk
    c = lax.broadcasted_iota(jnp.int32, (GROUP * blk, 3 * blk), 1)
    in_prev = (c < blk) & (c >= a)
    in_own = (c >= blk) & (c < 2 * blk)
    in_next = (c >= 2 * blk) & (c - 2 * blk <= a)
    interior = in_prev | in_own | in_next
    neg_inf = jnp.float32(-jnp.inf)
    lane = lax.broadcasted_iota(jnp.int32, (GROUP * blk, blk), 1)
    meta_lane = lane < N_META
    meta_fill = []
    for hk in range(N_KV_HEADS):
        sink_tile = jnp.concatenate(
            [jnp.full((blk, blk), sink_ref[hk * GROUP + g] * LOG2E, F32) for g in range(GROUP)],
            axis=0)
        meta_fill.append(jnp.where(lane == N_META, sink_tile, neg_inf))
    kv_blocks = ([kvp_ref.at[0]] + [kvc_ref.at[0, pl.ds(i * blk, blk), :] for i in range(n_blk)]
                 + [kvn_ref.at[0]])
    pad = jnp.zeros((blk - N_META, HEAD_DIM), BF16)

    def scores(i, hk):
        rows = slice(i * blk, (i + 1) * blk)
        mask = interior
        if i == 0:
            mask = (in_prev & (step > 0)) | in_own | in_next
        if i == n_blk - 1:
            mask = mask & (in_prev | in_own | (step < n_steps - 1))
        ks = slice(hk * HEAD_DIM, (hk + 1) * HEAD_DIM)
        keys = jnp.concatenate([r[:, ks] for r in kv_blocks[i:i + 3]], axis=0)
        q4 = jnp.concatenate(
            [q_ref[0, rows, h * HEAD_DIM:(h + 1) * HEAD_DIM]
             for h in range(hk * GROUP, (hk + 1) * GROUP)], axis=0)
        s_band = jnp.where(mask, _dot_nt(q4, keys), neg_inf)
        km = jnp.concatenate([kvm_ref[0, :, ks], pad], axis=0)
        s_meta = jnp.where(meta_lane, _dot_nt(q4, km), meta_fill[hk])
        return [s_band[:, t * blk:(t + 1) * blk] for t in range(3)] + [s_meta]

    def finish(i, hk, tiles):
        rows = slice(i * blk, (i + 1) * blk)
        vs = slice(v_off + hk * HEAD_DIM, v_off + (hk + 1) * HEAD_DIM)
        vals = jnp.concatenate([r[:, vs] for r in kv_blocks[i:i + 3]] + [kvm_ref[0, :, vs], pad],
                               axis=0)
        m = jnp.max(functools.reduce(jnp.maximum, tiles), axis=-1, keepdims=True)
        p = jnp.concatenate([jnp.exp2(t - m) for t in tiles], axis=1).astype(BF16)
        o4 = _dot(p, vals) / jnp.sum(p.astype(F32), axis=-1, keepdims=True)
        for g in range(GROUP):
            h = hk * GROUP + g
            o_ref[0, rows, h * HEAD_DIM:(h + 1) * HEAD_DIM] = o4[g * blk:(g + 1) * blk].astype(BF16)

    previous = None
    for i in range(n_blk):
        for hk in range(N_KV_HEADS):
            current = (i, hk, scores(i, hk))
            if previous is not None:
                finish(*previous)
            previous = current
    finish(*previous)


def _attn(q, kv, kvm, sink):
    b, s, dq = q.shape
    dkv2 = kv.shape[-1]
    blk = ATTN_BLOCK
    n_blk = min(ATTN_BLOCKS_PER_STEP, s // blk)
    n_steps = s // (n_blk * blk)
    last_blk = s // blk - 1
    prev = pl.BlockSpec((1, blk, dkv2), lambda bi, n: (bi, jnp.maximum(n_blk * n - 1, 0), 0))
    own = pl.BlockSpec((1, n_blk * blk, dkv2), lambda bi, n: (bi, n, 0))
    nxt = pl.BlockSpec((1, blk, dkv2),
                       lambda bi, n: (bi, jnp.minimum(n_blk * (n + 1), last_blk), 0))
    meta = pl.BlockSpec((1, N_META, dkv2), lambda bi, n: (bi, 0, 0))
    return pl.pallas_call(
        functools.partial(_attn_kernel, n_steps=n_steps),
        grid=(b, n_steps),
        in_specs=[
            pl.BlockSpec(memory_space=pltpu.SMEM),
            pl.BlockSpec((1, n_blk * blk, dq), lambda bi, n: (bi, n, 0)),
            prev, own, nxt, meta,
        ],
        out_specs=pl.BlockSpec((1, n_blk * blk, dq), lambda bi, n: (bi, n, 0)),
        out_shape=jax.ShapeDtypeStruct((b, s, dq), BF16),
        compiler_params=_params(2),
        name="band_attn",
    )(sink, q, kv, kv, kv, kvm)


def _attn_meta_kernel(sink_ref, qm_ref, kvm_ref, kvr_ref, o_ref):
    rows = GROUP * N_META
    v_off = N_KV_HEADS * HEAD_DIM
    i = lax.broadcasted_iota(jnp.int32, (rows, ATTN_BLOCK), 0) % N_META
    j = lax.broadcasted_iota(jnp.int32, (rows, ATTN_BLOCK), 1)
    allowed = (N_META + j) - i <= WINDOW
    neg_inf = jnp.float32(-jnp.inf)
    for hk in range(N_KV_HEADS):
        ks = slice(hk * HEAD_DIM, (hk + 1) * HEAD_DIM)
        vs = slice(v_off + hk * HEAD_DIM, v_off + (hk + 1) * HEAD_DIM)
        heads = [hk * GROUP + g for g in range(GROUP)]
        q4 = jnp.concatenate(
            [qm_ref[0, :, h * HEAD_DIM:(h + 1) * HEAD_DIM] for h in heads], axis=0)
        s_meta = _dot_nt(q4, kvm_ref[0, :, ks])
        s_real = jnp.where(allowed, _dot_nt(q4, kvr_ref[0, :, ks]), neg_inf)
        sink_col = jnp.concatenate(
            [jnp.full((N_META, 1), sink_ref[h] * LOG2E, F32) for h in heads], axis=0)
        o4 = _softmax_pv([s_meta, s_real], [kvm_ref[0, :, vs], kvr_ref[0, :, vs]], sink_col)
        for g, h in enumerate(heads):
            o_ref[0, :, h * HEAD_DIM:(h + 1) * HEAD_DIM] = (
                o4[g * N_META:(g + 1) * N_META].astype(BF16))


def _attn_meta(qm, kvm, kv, sink):
    b, _, dq = qm.shape
    dkv2 = kvm.shape[-1]
    meta = lambda width: pl.BlockSpec((1, N_META, width), lambda bi: (bi, 0, 0))
    first = pl.BlockSpec((1, ATTN_BLOCK, dkv2), lambda bi: (bi, 0, 0))
    return pl.pallas_call(
        _attn_meta_kernel,
        grid=(b,),
        in_specs=[pl.BlockSpec(memory_space=pltpu.SMEM), meta(dq), meta(dkv2), first],
        out_specs=meta(dq),
        out_shape=jax.ShapeDtypeStruct((b, N_META, dq), BF16),
        compiler_params=_params(1),
        name="meta_attn",
    )(sink, qm, kvm, kv)


def _post_kernel(*refs, rec, ff_chunk):
    if rec:
        h_ref, of_ref, ob_ref, gate_ref, og_ref, wo_ref, nw_ref, wu_ref, wd_ref, out_ref = refs
        o = of_ref[0].astype(F32) + ob_ref[0].astype(F32)
        n_heads = o.shape[-1] // HEAD_DIM
        o = jnp.concatenate(
            [_rms(o[:, j * HEAD_DIM:(j + 1) * HEAD_DIM], og_ref[...]) for j in range(n_heads)],
            axis=1)
        o = (o * gate_ref[0].astype(F32)).astype(BF16)
    else:
        h_ref, o_ref, wo_ref, nw_ref, wu_ref, wd_ref, out_ref = refs
        o = o_ref[0]
    h1 = h_ref[0] + _dot(o, wo_ref[...])
    hn = _rms(h1, nw_ref[...]).astype(BF16)
    d_ff = wu_ref.shape[1]
    acc = h1
    for c in range(d_ff // ff_chunk):
        sl = slice(c * ff_chunk, (c + 1) * ff_chunk)
        z = jnp.maximum(_dot(hn, wu_ref[:, sl]), 0.0)
        acc = acc + _dot((z * z).astype(BF16), wd_ref[sl, :])
    out_ref[0] = acc


def _post(h, mixer_out, w_out, nw, w_up, w_down, tm, rec_extra=None):
    bx, sx, d = h.shape
    d_ff = w_up.shape[1]
    tok = pl.BlockSpec((1, tm, d), lambda b, i: (b, i, 0))
    rec = rec_extra is not None
    if rec:
        ins = [h, *mixer_out, rec_extra, w_out, nw, w_up, w_down]
        specs = [tok, tok, tok, tok, _resident((1, HEAD_DIM))]
    else:
        ins = [h, mixer_out, w_out, nw, w_up, w_down]
        specs = [tok, tok]
    specs += [_resident((d, d)), _resident((1, d)), _resident((d, d_ff)), _resident((d_ff, d))]
    return pl.pallas_call(
        functools.partial(_post_kernel, rec=rec, ff_chunk=min(FF_CHUNK, d_ff)),
        grid=(bx, sx // tm),
        in_specs=specs,
        out_specs=tok,
        out_shape=jax.ShapeDtypeStruct((bx, sx, d), F32),
        compiler_params=_params(2, n_inputs=len(ins), weight_inputs=(
            len(ins) - 4, len(ins) - 2, len(ins) - 1)),
        name="post_rec" if rec else "post_attn",
    )(*ins)


def _lower_bound(tab_ref, layer):
    t = tab_ref[...]
    e = jnp.exp(t - jnp.max(t, axis=0, keepdims=True))
    p = e / jnp.sum(e, axis=0, keepdims=True)
    return jnp.sum(p[1:layer + 1], axis=0, keepdims=True)


def _rec_in_rows(h, nw_ref, w_ref, lbf_ref, lbb_ref, layer):
    d = h.shape[-1]
    hn = _rms(h, nw_ref[...]).astype(BF16)
    proj = lambda j: _dot(hn, w_ref[:, j * d:(j + 1) * d])

    def log_forget(z, tab_ref):
        lb = _lower_bound(tab_ref, layer)
        return jnp.log2(lb + (1.0 - lb) * jax.nn.sigmoid(z))

    gf = log_forget(proj(2), lbf_ref)
    gb = log_forget(proj(3), lbb_ref)
    zq = proj(0)
    q = zq * jax.nn.sigmoid(zq)
    zg = proj(4)
    gate = zg * jax.nn.sigmoid(zg)
    v = proj(1)
    return q, v, gf, gb, gate


def _rec_in_kernel(h_ref, nw_ref, w_ref, lbf_ref, lbb_ref,
                   q_ref, v_ref, gf_ref, gb_ref, gate_ref, *, layer):
    tm = h_ref.shape[1]
    sub = min(tm, REC_IN_SUB_TILE)
    for r in range(tm // sub):
        rows = slice(r * sub, (r + 1) * sub)
        q, v, gf, gb, gate = _rec_in_rows(h_ref[0, rows, :], nw_ref, w_ref, lbf_ref, lbb_ref,
                                          layer)
        q_ref[0, rows, :] = q.astype(BF16)
        v_ref[0, rows, :] = v.astype(BF16)
        gf_ref[0, rows, :] = gf
        gb_ref[0, rows, :] = gb
        gate_ref[0, rows, :] = gate.astype(BF16)


def _rec_in(h, nw, w_in, lb_f, lb_b, layer, tm):
    b, s, d = h.shape
    depth = lb_f.shape[0]
    tok = pl.BlockSpec((1, tm, d), lambda bi, i: (bi, i, 0))
    return pl.pallas_call(
        functools.partial(_rec_in_kernel, layer=layer),
        grid=(b, s // tm),
        in_specs=[tok, _resident((1, d)), _resident((d, w_in.shape[1])),
                  _resident((depth, d)), _resident((depth, d))],
        out_specs=[tok, tok, tok, tok, tok],
        out_shape=[
            jax.ShapeDtypeStruct((b, s, d), BF16),
            jax.ShapeDtypeStruct((b, s, d), BF16),
            jax.ShapeDtypeStruct((b, s, d), F32),
            jax.ShapeDtypeStruct((b, s, d), F32),
            jax.ShapeDtypeStruct((b, s, d), BF16),
        ],
        compiler_params=_params(2, n_inputs=5, weight_inputs=(2,)),
        name="rec_in",
    )(h, nw, w_in, lb_f, lb_b)


def _rec_in_meta_kernel(hm_ref, nw_ref, w_ref, lbf_ref, lbb_ref,
                        q_ref, v_ref, gf_ref, gb_ref, gate_ref, *, layer):
    b, n_meta, d = hm_ref.shape
    h = hm_ref[...].reshape(b * n_meta, d)
    q, v, gf, gb, gate = _rec_in_rows(h, nw_ref, w_ref, lbf_ref, lbb_ref, layer)
    pad = REC_CHUNK - n_meta
    for ref, val in ((q_ref, q), (v_ref, v), (gf_ref, gf), (gb_ref, gb)):
        ref[:, :pad, :] = jnp.zeros((b, pad, d), ref.dtype)
        ref[:, pad:, :] = val.reshape(b, n_meta, d).astype(ref.dtype)
    gate_ref[...] = gate.reshape(b, n_meta, d).astype(BF16)


def _rec_in_meta(hm, nw, w_in, lb_f, lb_b, layer):
    b, n_meta, d = hm.shape
    depth = lb_f.shape[0]
    chunk = pl.BlockSpec((b, REC_CHUNK, d), lambda i: (0, 0, 0))
    whole = pl.BlockSpec((b, n_meta, d), lambda i: (0, 0, 0))
    return pl.pallas_call(
        functools.partial(_rec_in_meta_kernel, layer=layer),
        grid=(1,),
        in_specs=[whole, _resident((1, d)), _resident((d, w_in.shape[1])),
                  _resident((depth, d)), _resident((depth, d))],
        out_specs=[chunk, chunk, chunk, chunk, whole],
        out_shape=[
            jax.ShapeDtypeStruct((b, REC_CHUNK, d), BF16),
            jax.ShapeDtypeStruct((b, REC_CHUNK, d), BF16),
            jax.ShapeDtypeStruct((b, REC_CHUNK, d), F32),
            jax.ShapeDtypeStruct((b, REC_CHUNK, d), F32),
            jax.ShapeDtypeStruct((b, n_meta, d), BF16),
        ],
        compiler_params=_params(1, n_inputs=5, weight_inputs=(2,)),
        name="rec_in_meta",
    )(hm, nw, w_in, lb_f, lb_b)


_SUM_LEVEL_HALVES = (32, 16, 8, 4, 2)
_N_SUM_LEVELS = len(_SUM_LEVEL_HALVES)
_MASK_ADJACENT = _N_SUM_LEVELS
_MASK_DIAG = _N_SUM_LEVELS + 1
_ROWS_Q_INTER = _N_SUM_LEVELS * REC_CHUNK
_ROWS_K_STATE = (_N_SUM_LEVELS + 1) * REC_CHUNK
_N_SUM_ROWS = (_N_SUM_LEVELS + 2) * REC_CHUNK
SCAN_CHUNKS_PER_STEP = 4
SCAN_PIPELINE_LAG = 8
SUBLANE_TILE = 8


def _scan_constants(backward):
    c = REC_CHUNK
    u = np.arange(c)
    w = (u[None, :] >= u[:, None]) if backward else (u[None, :] <= u[:, None])
    masks = np.zeros((_N_SUM_LEVELS + 2, c, c), np.float32)
    for lvl, half in enumerate(_SUM_LEVEL_HALVES + (1,)):
        for t in range(c):
            start = (t // (2 * half)) * 2 * half
            if t - start >= half:
                masks[lvl, t, start:start + half] = 1.0
    masks[_MASK_DIAG] = np.eye(c, dtype=np.float32)
    if backward:
        masks = masks[:, ::-1, ::-1]
    w = np.concatenate([w, w], axis=1).astype(np.float32)
    return jnp.asarray(w, BF16), jnp.asarray(np.ascontiguousarray(masks), F32)


def _split_decays(cum, backward):
    c, d = cum.shape
    tile = SUBLANE_TILE
    sub = lax.broadcasted_iota(jnp.int32, (tile, d), 0)
    sign4 = jnp.where((sub < 4) != backward, -1.0, 1.0).astype(F32)
    sign2 = jnp.where(((sub % 4) < 2) != backward, -1.0, 1.0).astype(F32)
    row = lambda r: cum[r:r + 1]
    tiles = [cum[t:t + tile] for t in range(0, c, tile)]
    out = []
    for half in _SUM_LEVEL_HALVES:
        for i, x in enumerate(tiles):
            t0 = i * tile
            if half >= tile:
                start = (t0 // (2 * half)) * 2 * half
                r = start + half if backward else start + half - 1
                after = (t0 - start >= half) != backward
                out.append(x - row(r) if after else row(r) - x)
            elif half == 4:
                r = t0 + 4 if backward else t0 + 3
                out.append((x - row(r)) * sign4)
            else:
                r1, r2 = (t0 + 2, t0 + 6) if backward else (t0 + 1, t0 + 5)
                out.append((x - jnp.where(sub < 4, row(r1), row(r2))) * sign2)
    out.extend(tiles)
    last = 0 if backward else c - 1
    out.extend(row(last) - x for x in tiles)
    return jnp.concatenate(out, axis=0)


class _ChunkJob:
    def __init__(self, d, q_ref, v_ref, g_ref, o_ref, rows, slot):
        self.d, self.q_ref, self.v_ref, self.g_ref, self.o_ref = d, q_ref, v_ref, g_ref, o_ref
        self.rows, self.slot = rows, slot


def _scan_kernel(qf_ref, vf_ref, gf_ref, qb_ref, vb_ref, gb_ref, qm_ref, vm_ref, gfm_ref, gbm_ref,
                 wf_ref, wb_ref, mf_ref, mb_ref, of_ref, ob_ref, ofm_ref, obm_ref,
                 st_ref, e_ref):
    c = REC_CHUNK
    n_heads = qf_ref.shape[-1] // HEAD_DIM
    n_chunks = qf_ref.shape[1] // c
    step = pl.program_id(1)
    w_refs = (wf_ref, wb_ref)
    m_refs = (mf_ref, mb_ref)
    whole_rows = (c - 1, 0)

    @pl.when(step == 0)
    def _():
        st_ref[...] = jnp.zeros(st_ref.shape, F32)

    def decay_stage(job):
        g2 = job.g_ref[0, job.rows, :]
        g_hi = g2.astype(BF16)
        g_lo = (g2 - g_hi.astype(F32)).astype(BF16)
        cum = _dot(w_refs[job.d][...], jnp.concatenate([g_hi, g_lo], axis=0))
        e_ref[job.slot, job.d] = jnp.exp2(_split_decays(cum, backward=job.d == 1))

    def pair_stage(job, hd):
        d, m_ref = job.d, m_refs[job.d]
        sl = slice(hd * HEAD_DIM, (hd + 1) * HEAD_DIM)
        e = lambda r0: e_ref[job.slot, d, r0:r0 + c, sl]
        q = job.q_ref[0, job.rows, sl].astype(F32)
        f = jnp.exp2(job.g_ref[0, job.rows, sl])
        k = 1.0 - f
        row_id = lax.broadcasted_iota(jnp.int32, (c, HEAD_DIM), 0)
        near = _dot_nt(jnp.concatenate([q, q * f], axis=0).astype(BF16), k.astype(BF16))
        pair = m_ref[_MASK_DIAG] * near[:c] + m_ref[_MASK_ADJACENT] * near[c:]
        for lvl, half in enumerate(_SUM_LEVEL_HALVES):
            is_query = ((row_id // half) % 2 == 1) != (d == 1)
            both = (jnp.where(is_query, q, k) * e(lvl * c)).astype(BF16)
            pair = pair + m_ref[lvl] * _dot_nt(both, both)
        q_in = (q * e(_ROWS_Q_INTER)).astype(BF16)
        k_out = (k * e(_ROWS_K_STATE)).astype(BF16)
        return pair.astype(BF16), q_in, k_out

    def output_stage(job, hd, pair, q_in, k_out):
        d = job.d
        sl = slice(hd * HEAD_DIM, (hd + 1) * HEAD_DIM)
        v = job.v_ref[0, job.rows, sl]
        state = st_ref[d, hd]
        o = _dot(pair, v) + _dot(q_in, state.astype(BF16))
        job.o_ref[0, job.rows, sl] = o.astype(BF16)
        row = _ROWS_Q_INTER + whole_rows[d]
        t0 = (row // SUBLANE_TILE) * SUBLANE_TILE
        decay_col = e_ref[job.slot, d, t0:t0 + SUBLANE_TILE, sl].T[:, row - t0:row - t0 + 1]
        st_ref[d, hd] = state * decay_col + _dot_tn(k_out, v)

    def run(jobs):
        pending = []
        for group in jobs:
            for job in group:
                decay_stage(job)
            for hd in range(n_heads):
                for job in group:
                    pending.append((job, hd) + pair_stage(job, hd))
                    if len(pending) > SCAN_PIPELINE_LAG:
                        output_stage(*pending.pop(0))
        for item in pending:
            output_stage(*item)

    whole = slice(0, c)

    @pl.when(step == 0)
    def _():
        run([[_ChunkJob(0, qm_ref, vm_ref, gfm_ref, ofm_ref, whole, 0)]])

    groups = []
    for i in range(n_chunks):
        fwd_rows = slice(i * c, (i + 1) * c)
        bwd_rows = slice((n_chunks - 1 - i) * c, (n_chunks - i) * c)
        groups.append([_ChunkJob(0, qf_ref, vf_ref, gf_ref, of_ref, fwd_rows, i),
                       _ChunkJob(1, qb_ref, vb_ref, gb_ref, ob_ref, bwd_rows, i)])
    run(groups)

    @pl.when(step == pl.num_programs(1) - 1)
    def _():
        run([[_ChunkJob(1, qm_ref, vm_ref, gbm_ref, obm_ref, whole, 0)]])


def _scan(q, v, gf, gb, qm, vm, gfm, gbm):
    b, s, d = q.shape
    n_heads = d // HEAD_DIM
    chunks_per_step = min(SCAN_CHUNKS_PER_STEP, s // REC_CHUNK)
    rows = chunks_per_step * REC_CHUNK
    n_steps = s // rows
    wf, mf = _scan_constants(False)
    wb, mb = _scan_constants(True)
    fwd = pl.BlockSpec((1, rows, d), lambda bi, j: (bi, j, 0))
    bwd = pl.BlockSpec((1, rows, d), lambda bi, j: (bi, n_steps - 1 - j, 0))
    meta = pl.BlockSpec((1, REC_CHUNK, d), lambda bi, j: (bi, 0, 0))
    return pl.pallas_call(
        _scan_kernel,
        grid=(b, n_steps),
        in_specs=[fwd, fwd, fwd, bwd, bwd, bwd, meta, meta, meta, meta,
                  _resident(wf.shape), _resident(wb.shape),
                  _resident(mf.shape), _resident(mb.shape)],
        out_specs=[fwd, bwd, meta, meta],
        out_shape=[jax.ShapeDtypeStruct((b, s, d), BF16)] * 2
        + [jax.ShapeDtypeStruct((b, REC_CHUNK, d), BF16)] * 2,
        scratch_shapes=[
            pltpu.VMEM((2, n_heads, HEAD_DIM, HEAD_DIM), F32),
            pltpu.VMEM((chunks_per_step, 2, _N_SUM_ROWS, d), F32),
        ],
        compiler_params=_params(2),
        name="gla_scan",
    )(q, v, gf, q, v, gb, qm, vm, gfm, gbm, wf, wb, mf, mb)


def _rope_tables(length):
    inv_freq = ROPE_THETA ** (-jnp.arange(0, HEAD_DIM, 2, dtype=F32) / HEAD_DIM)
    ang = jnp.arange(length, dtype=F32)[:, None] * inv_freq[None, :]
    cos, sin = jnp.cos(ang), jnp.sin(ang)
    return jnp.concatenate([cos, cos], axis=-1), jnp.concatenate([-sin, sin], axis=-1)


def kernel(x, meta_tokens, mix_norm_w, mlp_norm_w, attn_w_in, attn_w_out, attn_q_norm_w,
           attn_k_norm_w, attn_sink, rec_w_in, rec_w_out, rec_lb_fwd, rec_lb_bwd,
           rec_out_norm_w, mlp_w_up, mlp_w_down):
    b, s, d = x.shape
    depth = mix_norm_w.shape[0]
    n_mixers = 2
    tm = min(TOKEN_TILE, s)
    n_meta_rows = b * N_META
    assert d % HEAD_DIM == 0 and d // HEAD_DIM == N_KV_HEADS * GROUP, x.shape
    assert s % tm == 0 and s % ATTN_BLOCK == 0 and s % REC_CHUNK == 0, x.shape
    assert meta_tokens.shape == (N_META, d), meta_tokens.shape

    h = x
    hm = jnp.broadcast_to(meta_tokens.astype(x.dtype)[None], (b, N_META, d))
    cc, ss = _rope_tables(N_META + s)
    cc_meta, ss_meta = jnp.tile(cc[:N_META], (b, 1)), jnp.tile(ss[:N_META], (b, 1))
    cc_real, ss_real = cc[N_META:], ss[N_META:]
    row = lambda w: w.reshape(1, -1)
    flat_meta = lambda t: t.reshape(1, n_meta_rows, t.shape[-1])
    per_batch = lambda t: t.reshape(b, N_META, t.shape[-1])

    for layer in range(depth):
        j = layer // n_mixers
        nw = row(mix_norm_w[layer])
        mlp = (row(mlp_norm_w[layer]), mlp_w_up[layer].astype(BF16),
               mlp_w_down[layer].astype(BF16))
        if layer % n_mixers == 0:
            w_in = attn_w_in[j].astype(BF16)
            w_out = attn_w_out[j].astype(BF16)
            gains = (row(attn_q_norm_w[j]), row(attn_k_norm_w[j]))
            q, kv = _attn_in(h, nw, w_in, *gains, cc_real, ss_real, tm)
            qm, kvm = map(per_batch, _attn_in(
                flat_meta(hm), nw, w_in, *gains, cc_meta, ss_meta, n_meta_rows))
            sink = attn_sink[j].astype(F32)
            o = _attn(q, kv, kvm, sink)
            om = _attn_meta(qm, kvm, kv, sink)
            h = _post(h, o, w_out, *mlp, tm)
            hm = per_batch(_post(flat_meta(hm), flat_meta(om), w_out, *mlp, n_meta_rows))
        else:
            w_in = rec_w_in[j].astype(BF16)
            w_out = rec_w_out[j].astype(BF16)
            gain = row(rec_out_norm_w[j])
            q, v, gf, gb, gate = _rec_in(h, nw, w_in, rec_lb_fwd, rec_lb_bwd, layer,
                                         math.gcd(REC_IN_TOKEN_TILE, s))
            qm, vm, gfm, gbm, gate_m = _rec_in_meta(hm, nw, w_in, rec_lb_fwd, rec_lb_bwd, layer)
            o_f, o_b, o_fm, o_bm = _scan(q, v, gf, gb, qm, vm, gfm, gbm)
            h = _post(h, (o_f, o_b, gate), w_out, *mlp, tm, rec_extra=gain)
            meta_rows = lambda t: flat_meta(t[:, REC_CHUNK - N_META:])
            hm = per_batch(_post(flat_meta(hm), (meta_rows(o_fm), meta_rows(o_bm),
                                                 flat_meta(gate_m)),
                                 w_out, *mlp, n_meta_rows, rec_extra=gain))
    return h
```

```python
import functools
import math

import numpy as np
import jax
import jax.numpy as jnp
from jax import lax
from jax.experimental import pallas as pl
from jax.experimental.pallas import tpu as pltpu

N_META = 16
HEAD_DIM = 128
N_KV_HEADS = 2
GROUP = 4
WINDOW = 128
ATTN_BLOCK = 128
ROPE_THETA = 10000.0
REC_CHUNK = 64
EPS = 1e-6
LOG2E = 1.4426950408889634

F32 = jnp.float32
BF16 = jnp.bfloat16

V7X_VMEM_BYTES = 64 * 1024 * 1024
V7X_VMEM_LIMIT_BYTES = V7X_VMEM_BYTES * 7 // 8
TOKEN_TILE = 512
FF_CHUNK = 1024
SUB_TILE = 256
REC_IN_TOKEN_TILE = 1024
REC_IN_SUB_TILE = 128
ATTN_BLOCKS_PER_STEP = 4


def _params(n_axes):
    return pltpu.CompilerParams(
        dimension_semantics=("arbitrary",) * n_axes,
        vmem_limit_bytes=V7X_VMEM_LIMIT_BYTES,
    )


def _resident(shape):
    zeros = (0,) * len(shape)
    return pl.BlockSpec(shape, lambda *_: zeros, pipeline_mode=pl.Buffered(1))


def _rms(x, w):
    ms = jnp.mean(x * x, axis=-1, keepdims=True)
    return x * lax.rsqrt(ms + EPS) * w


def _dot(a, b):
    return jnp.dot(a, b, preferred_element_type=F32)


def _dot_nt(a, b):
    return lax.dot_general(a, b, (((1,), (1,)), ((), ())), preferred_element_type=F32)


def _dot_tn(a, b):
    return lax.dot_general(a, b, (((0,), (0,)), ((), ())), preferred_element_type=F32)


def _attn_in_kernel(h_ref, nw_ref, w_ref, qg_ref, kg_ref, cc_ref, ss_ref, q_ref, kv_ref):
    n_q = q_ref.shape[-1] // HEAD_DIM
    n_kv = kv_ref.shape[-1] // (2 * HEAD_DIM)
    tm = h_ref.shape[1]
    sub = min(tm, SUB_TILE)
    half = HEAD_DIM // 2
    q_gain = qg_ref[...] * (HEAD_DIM ** -0.5 * LOG2E)
    k_gain = kg_ref[...]

    for r in range(tm // sub):
        rows = slice(r * sub, (r + 1) * sub)
        hn = _rms(h_ref[0, rows, :], nw_ref[...]).astype(BF16)
        qkv = _dot(hn, w_ref[...])
        cc = cc_ref[rows, :]
        ss = ss_ref[rows, :]
        tabs_q = (q_gain * cc, pltpu.roll(q_gain, half, 1) * ss)
        tabs_k = (k_gain * cc, pltpu.roll(k_gain, half, 1) * ss)

        def norm_rope(x, tabs):
            y = x * lax.rsqrt(jnp.mean(x * x, axis=-1, keepdims=True) + EPS)
            return y * tabs[0] + pltpu.roll(y, half, 1) * tabs[1]

        for j in range(n_q):
            sl = slice(j * HEAD_DIM, (j + 1) * HEAD_DIM)
            q_ref[0, rows, sl] = norm_rope(qkv[:, sl], tabs_q).astype(BF16)
        for j in range(n_kv):
            sl = slice(j * HEAD_DIM, (j + 1) * HEAD_DIM)
            src = slice((n_q + j) * HEAD_DIM, (n_q + j + 1) * HEAD_DIM)
            kv_ref[0, rows, sl] = norm_rope(qkv[:, src], tabs_k).astype(BF16)
        kv_ref[0, rows, n_kv * HEAD_DIM:] = qkv[:, (n_q + n_kv) * HEAD_DIM:].astype(BF16)


def _attn_in(h, nw, w_in, q_gain, k_gain, cc, ss, tm):
    bx, sx, d = h.shape
    n_in = w_in.shape[1]
    dq = d
    dkv = (n_in - dq) // 2
    grid = (bx, sx // tm)
    tok = lambda width: pl.BlockSpec((1, tm, width), lambda b, i: (b, i, 0))
    return pl.pallas_call(
        _attn_in_kernel,
        grid=grid,
        in_specs=[
            tok(d),
            _resident((1, d)),
            _resident((d, n_in)),
            _resident((1, HEAD_DIM)),
            _resident((1, HEAD_DIM)),
            pl.BlockSpec((tm, HEAD_DIM), lambda b, i: (i, 0)),
            pl.BlockSpec((tm, HEAD_DIM), lambda b, i: (i, 0)),
        ],
        out_specs=[tok(dq), tok(2 * dkv)],
        out_shape=[
            jax.ShapeDtypeStruct((bx, sx, dq), BF16),
            jax.ShapeDtypeStruct((bx, sx, 2 * dkv), BF16),
        ],
        compiler_params=_params(2),
        name="attn_in",
    )(h, nw, w_in, q_gain, k_gain, cc, ss)


def _softmax_pv(scores, values, sink_col):
    m = sink_col
    for s in scores:
        m = jnp.maximum(m, jnp.max(s, axis=-1, keepdims=True))
    den = jnp.exp2(sink_col - m)
    acc = None
    for s, v in zip(scores, values):
        p = jnp.exp2(s - m)
        den = den + jnp.sum(p, axis=-1, keepdims=True)
        pv = _dot(p.astype(BF16), v)
        acc = pv if acc is None else acc + pv
    return acc / den


def _attn_kernel(sink_ref, q_ref, kvp_ref, kvc_ref, kvn_ref, kvm_ref, o_ref, *, n_steps):
    step = pl.program_id(1)
    blk = ATTN_BLOCK
    n_blk = q_ref.shape[1] // blk
    v_off = N_KV_HEADS * HEAD_DIM
    a = lax.broadcasted_iota(jnp.int32, (GROUP * blk, 3 * blk), 0) % blk
    c = lax.broadcasted_iota(jnp.int32, (GROUP * blk, 3 * blk), 1)
    in_prev = (c < blk) & (c >= a)
    in_own = (c >= blk) & (c < 2 * blk)
    in_next = (c >= 2 * blk) & (c - 2 * blk <= a)
    interior = in_prev | in_own | in_next
    neg_inf = jnp.float32(-jnp.inf)
    lane = lax.broadcasted_iota(jnp.int32, (GROUP * blk, blk), 1)
    meta_lane = lane < N_META
    meta_fill = []
    for hk in range(N_KV_HEADS):
        sink_tile = jnp.concatenate(
            [jnp.full((blk, blk), sink_ref[hk * GROUP + g] * LOG2E, F32) for g in range(GROUP)],
            axis=0)
        meta_fill.append(jnp.where(lane == N_META, sink_tile, neg_inf))
    kv_blocks = ([kvp_ref.at[0]] + [kvc_ref.at[0, pl.ds(i * blk, blk), :] for i in range(n_blk)]
                 + [kvn_ref.at[0]])
    pad = jnp.zeros((blk - N_META, HEAD_DIM), BF16)

    def scores(i, hk):
        rows = slice(i * blk, (i + 1) * blk)
        mask = interior
        if i == 0:
            mask = (in_prev & (step > 0)) | in_own | in_next
        if i == n_blk - 1:
            mask = mask & (in_prev | in_own | (step < n_steps - 1))
        ks = slice(hk * HEAD_DIM, (hk + 1) * HEAD_DIM)
        keys = jnp.concatenate([r[:, ks] for r in kv_blocks[i:i + 3]], axis=0)
        q4 = jnp.concatenate(
            [q_ref[0, rows, h * HEAD_DIM:(h + 1) * HEAD_DIM]
             for h in range(hk * GROUP, (hk + 1) * GROUP)], axis=0)
        s_band = jnp.where(mask, _dot_nt(q4, keys), neg_inf)
        km = jnp.concatenate([kvm_ref[0, :, ks], pad], axis=0)
        s_meta = jnp.where(meta_lane, _dot_nt(q4, km), meta_fill[hk])
        return [s_band[:, t * blk:(t + 1) * blk] for t in range(3)] + [s_meta]

    def finish(i, hk, tiles):
        rows = slice(i * blk, (i + 1) * blk)
        vs = slice(v_off + hk * HEAD_DIM, v_off + (hk + 1) * HEAD_DIM)
        vals = jnp.concatenate([r[:, vs] for r in kv_blocks[i:i + 3]] + [kvm_ref[0, :, vs], pad],
                               axis=0)
        m = jnp.max(functools.reduce(jnp.maximum, tiles), axis=-1, keepdims=True)
        p = jnp.concatenate([jnp.exp2(t - m) for t in tiles], axis=1).astype(BF16)
        o4 = _dot(p, vals) / jnp.sum(p.astype(F32), axis=-1, keepdims=True)
        for g in range(GROUP):
            h = hk * GROUP + g
            o_ref[0, rows, h * HEAD_DIM:(h + 1) * HEAD_DIM] = o4[g * blk:(g + 1) * blk].astype(BF16)

    previous = None
    for i in range(n_blk):
        for hk in range(N_KV_HEADS):
            current = (i, hk, scores(i, hk))
            if previous is not None:
                finish(*previous)
            previous = current
    finish(*previous)


def _attn(q, kv, kvm, sink):
    b, s, dq = q.shape
    dkv2 = kv.shape[-1]
    blk = ATTN_BLOCK
    n_blk = min(ATTN_BLOCKS_PER_STEP, s // blk)
    n_steps = s // (n_blk * blk)
    last_blk = s // blk - 1
    prev = pl.BlockSpec((1, blk, dkv2), lambda bi, n: (bi, jnp.maximum(n_blk * n - 1, 0), 0))
    own = pl.BlockSpec((1, n_blk * blk, dkv2), lambda bi, n: (bi, n, 0))
    nxt = pl.BlockSpec((1, blk, dkv2),
                       lambda bi, n: (bi, jnp.minimum(n_blk * (n + 1), last_blk), 0))
    meta = pl.BlockSpec((1, N_META, dkv2), lambda bi, n: (bi, 0, 0))
    return pl.pallas_call(
        functools.partial(_attn_kernel, n_steps=n_steps),
        grid=(b, n_steps),
        in_specs=[
            pl.BlockSpec(memory_space=pltpu.SMEM),
            pl.BlockSpec((1, n_blk * blk, dq), lambda bi, n: (bi, n, 0)),
            prev, own, nxt, meta,
        ],
        out_specs=pl.BlockSpec((1, n_blk * blk, dq), lambda bi, n: (bi, n, 0)),
        out_shape=jax.ShapeDtypeStruct((b, s, dq), BF16),
        compiler_params=_params(2),
        name="band_attn",
    )(sink, q, kv, kv, kv, kvm)


def _attn_meta_kernel(sink_ref, qm_ref, kvm_ref, kvr_ref, o_ref):
    rows = GROUP * N_META
    v_off = N_KV_HEADS * HEAD_DIM
    i = lax.broadcasted_iota(jnp.int32, (rows, ATTN_BLOCK), 0) % N_META
    j = lax.broadcasted_iota(jnp.int32, (rows, ATTN_BLOCK), 1)
    allowed = (N_META + j) - i <= WINDOW
    neg_inf = jnp.float32(-jnp.inf)
    for hk in range(N_KV_HEADS):
        ks = slice(hk * HEAD_DIM, (hk + 1) * HEAD_DIM)
        vs = slice(v_off + hk * HEAD_DIM, v_off + (hk + 1) * HEAD_DIM)
        heads = [hk * GROUP + g for g in range(GROUP)]
        q4 = jnp.concatenate(
            [qm_ref[0, :, h * HEAD_DIM:(h + 1) * HEAD_DIM] for h in heads], axis=0)
        s_meta = _dot_nt(q4, kvm_ref[0, :, ks])
        s_real = jnp.where(allowed, _dot_nt(q4, kvr_ref[0, :, ks]), neg_inf)
        sink_col = jnp.concatenate(
            [jnp.full((N_META, 1), sink_ref[h] * LOG2E, F32) for h in heads], axis=0)
        o4 = _softmax_pv([s_meta, s_real], [kvm_ref[0, :, vs], kvr_ref[0, :, vs]], sink_col)
        for g, h in enumerate(heads):
            o_ref[0, :, h * HEAD_DIM:(h + 1) * HEAD_DIM] = (
                o4[g * N_META:(g + 1) * N_META].astype(BF16))


def _attn_meta(qm, kvm, kv, sink):
    b, _, dq = qm.shape
    dkv2 = kvm.shape[-1]
    meta = lambda width: pl.BlockSpec((1, N_META, width), lambda bi: (bi, 0, 0))
    first = pl.BlockSpec((1, ATTN_BLOCK, dkv2), lambda bi: (bi, 0, 0))
    return pl.pallas_call(
        _attn_meta_kernel,
        grid=(b,),
        in_specs=[pl.BlockSpec(memory_space=pltpu.SMEM), meta(dq), meta(dkv2), first],
        out_specs=meta(dq),
        out_shape=jax.ShapeDtypeStruct((b, N_META, dq), BF16),
        compiler_params=_params(1),
        name="meta_attn",
    )(sink, qm, kvm, kv)


def _post_kernel(*refs, rec, ff_chunk):
    if rec:
        h_ref, of_ref, ob_ref, gate_ref, og_ref, wo_ref, nw_ref, wu_ref, wd_ref, out_ref = refs
        o = of_ref[0].astype(F32) + ob_ref[0].astype(F32)
        n_heads = o.shape[-1] // HEAD_DIM
        o = jnp.concatenate(
            [_rms(o[:, j * HEAD_DIM:(j + 1) * HEAD_DIM], og_ref[...]) for j in range(n_heads)],
            axis=1)
        o = (o * gate_ref[0].astype(F32)).astype(BF16)
    else:
        h_ref, o_ref, wo_ref, nw_ref, wu_ref, wd_ref, out_ref = refs
        o = o_ref[0]
    h1 = h_ref[0] + _dot(o, wo_ref[...])
    hn = _rms(h1, nw_ref[...]).astype(BF16)
    d_ff = wu_ref.shape[1]
    acc = h1
    for c in range(d_ff // ff_chunk):
        sl = slice(c * ff_chunk, (c + 1) * ff_chunk)
        z = jnp.maximum(_dot(hn, wu_ref[:, sl]), 0.0)
        acc = acc + _dot((z * z).astype(BF16), wd_ref[sl, :])
    out_ref[0] = acc


def _post(h, mixer_out, w_out, nw, w_up, w_down, tm, rec_extra=None):
    bx, sx, d = h.shape
    d_ff = w_up.shape[1]
    tok = pl.BlockSpec((1, tm, d), lambda b, i: (b, i, 0))
    rec = rec_extra is not None
    if rec:
        ins = [h, *mixer_out, rec_extra, w_out, nw, w_up, w_down]
        specs = [tok, tok, tok, tok, _resident((1, HEAD_DIM))]
    else:
        ins = [h, mixer_out, w_out, nw, w_up, w_down]
        specs = [tok, tok]
    specs += [_resident((d, d)), _resident((1, d)), _resident((d, d_ff)), _resident((d_ff, d))]
    return pl.pallas_call(
        functools.partial(_post_kernel, rec=rec, ff_chunk=min(FF_CHUNK, d_ff)),
        grid=(bx, sx // tm),
        in_specs=specs,
        out_specs=tok,
        out_shape=jax.ShapeDtypeStruct((bx, sx, d), F32),
        compiler_params=_params(2),
        name="post_rec" if rec else "post_attn",
    )(*ins)


def _lower_bound(tab_ref, layer):
    t = tab_ref[...]
    e = jnp.exp(t - jnp.max(t, axis=0, keepdims=True))
    p = e / jnp.sum(e, axis=0, keepdims=True)
    return jnp.sum(p[1:layer + 1], axis=0, keepdims=True)


def _rec_in_rows(h, nw_ref, w_ref, lbf_ref, lbb_ref, layer):
    d = h.shape[-1]
    hn = _rms(h, nw_ref[...]).astype(BF16)
    proj = lambda j: _dot(hn, w_ref[:, j * d:(j + 1) * d])

    def log_forget(z, tab_ref):
        lb = _lower_bound(tab_ref, layer)
        return jnp.log2(lb + (1.0 - lb) * jax.nn.sigmoid(z))

    gf = log_forget(proj(2), lbf_ref)
    gb = log_forget(proj(3), lbb_ref)
    zq = proj(0)
    q = zq * jax.nn.sigmoid(zq)
    zg = proj(4)
    gate = zg * jax.nn.sigmoid(zg)
    v = proj(1)
    return q, v, gf, gb, gate


def _rec_in_kernel(h_ref, nw_ref, w_ref, lbf_ref, lbb_ref,
                   qv_ref, gf_ref, gb_ref, gate_ref, *, layer):
    tm, d = h_ref.shape[1:]
    sub = min(tm, REC_IN_SUB_TILE)
    for r in range(tm // sub):
        rows = slice(r * sub, (r + 1) * sub)
        q, v, gf, gb, gate = _rec_in_rows(h_ref[0, rows, :], nw_ref, w_ref, lbf_ref, lbb_ref,
                                          layer)
        qv_ref[0, rows, :d] = q.astype(BF16)
        qv_ref[0, rows, d:] = v.astype(BF16)
        gf_ref[0, rows, :] = gf
        gb_ref[0, rows, :] = gb
        gate_ref[0, rows, :] = gate.astype(BF16)


def _rec_in(h, nw, w_in, lb_f, lb_b, layer, tm):
    b, s, d = h.shape
    depth = lb_f.shape[0]
    tok = pl.BlockSpec((1, tm, d), lambda bi, i: (bi, i, 0))
    return pl.pallas_call(
        functools.partial(_rec_in_kernel, layer=layer),
        grid=(b, s // tm),
        in_specs=[tok, _resident((1, d)), _resident((d, w_in.shape[1])),
                  _resident((depth, d)), _resident((depth, d))],
        out_specs=[pl.BlockSpec((1, tm, 2 * d), lambda bi, i: (bi, i, 0)), tok, tok, tok],
        out_shape=[
            jax.ShapeDtypeStruct((b, s, 2 * d), BF16),
            jax.ShapeDtypeStruct((b, s, d), F32),
            jax.ShapeDtypeStruct((b, s, d), F32),
            jax.ShapeDtypeStruct((b, s, d), BF16),
        ],
        compiler_params=_params(2),
        name="rec_in",
    )(h, nw, w_in, lb_f, lb_b)


def _rec_in_meta_kernel(hm_ref, nw_ref, w_ref, lbf_ref, lbb_ref,
                        qv_ref, gf_ref, gb_ref, gate_ref, *, layer):
    b, n_meta, d = hm_ref.shape
    h = hm_ref[...].reshape(b * n_meta, d)
    q, v, gf, gb, gate = _rec_in_rows(h, nw_ref, w_ref, lbf_ref, lbb_ref, layer)
    pad = REC_CHUNK - n_meta
    for ref, val in ((qv_ref, jnp.concatenate([q, v], axis=1)), (gf_ref, gf), (gb_ref, gb)):
        width = val.shape[-1]
        ref[:, :pad, :] = jnp.zeros((b, pad, width), ref.dtype)
        ref[:, pad:, :] = val.reshape(b, n_meta, width).astype(ref.dtype)
    gate_ref[...] = gate.reshape(b, n_meta, d).astype(BF16)


def _rec_in_meta(hm, nw, w_in, lb_f, lb_b, layer):
    b, n_meta, d = hm.shape
    depth = lb_f.shape[0]
    chunk = pl.BlockSpec((b, REC_CHUNK, d), lambda i: (0, 0, 0))
    whole = pl.BlockSpec((b, n_meta, d), lambda i: (0, 0, 0))
    return pl.pallas_call(
        functools.partial(_rec_in_meta_kernel, layer=layer),
        grid=(1,),
        in_specs=[whole, _resident((1, d)), _resident((d, w_in.shape[1])),
                  _resident((depth, d)), _resident((depth, d))],
        out_specs=[pl.BlockSpec((b, REC_CHUNK, 2 * d), lambda i: (0, 0, 0)), chunk, chunk, whole],
        out_shape=[
            jax.ShapeDtypeStruct((b, REC_CHUNK, 2 * d), BF16),
            jax.ShapeDtypeStruct((b, REC_CHUNK, d), F32),
            jax.ShapeDtypeStruct((b, REC_CHUNK, d), F32),
            jax.ShapeDtypeStruct((b, n_meta, d), BF16),
        ],
        compiler_params=_params(1),
        name="rec_in_meta",
    )(hm, nw, w_in, lb_f, lb_b)


_SUM_LEVEL_HALVES = (32, 16, 8, 4, 2)
_N_SUM_LEVELS = len(_SUM_LEVEL_HALVES)
_MASK_ADJACENT = _N_SUM_LEVELS
_MASK_DIAG = _N_SUM_LEVELS + 1
_ROWS_Q_INTER = _N_SUM_LEVELS * REC_CHUNK
_ROWS_K_STATE = (_N_SUM_LEVELS + 1) * REC_CHUNK
_N_SUM_ROWS = (_N_SUM_LEVELS + 2) * REC_CHUNK
SCAN_CHUNKS_PER_STEP = 4
SCAN_PIPELINE_LAG = 8
SUBLANE_TILE = 8


def _scan_constants(backward):
    c = REC_CHUNK
    u = np.arange(c)
    w = (u[None, :] >= u[:, None]) if backward else (u[None, :] <= u[:, None])
    masks = np.zeros((_N_SUM_LEVELS + 2, c, c), np.float32)
    for lvl, half in enumerate(_SUM_LEVEL_HALVES + (1,)):
        for t in range(c):
            start = (t // (2 * half)) * 2 * half
            if t - start >= half:
                masks[lvl, t, start:start + half] = 1.0
    masks[_MASK_DIAG] = np.eye(c, dtype=np.float32)
    if backward:
        masks = masks[:, ::-1, ::-1]
    w = np.concatenate([w, w], axis=1).astype(np.float32)
    return jnp.asarray(w, BF16), jnp.asarray(np.ascontiguousarray(masks), F32)


def _split_decays(cum, backward):
    c, d = cum.shape
    tile = SUBLANE_TILE
    sub = lax.broadcasted_iota(jnp.int32, (tile, d), 0)
    sign4 = jnp.where((sub < 4) != backward, -1.0, 1.0).astype(F32)
    sign2 = jnp.where(((sub % 4) < 2) != backward, -1.0, 1.0).astype(F32)
    row = lambda r: cum[r:r + 1]
    tiles = [cum[t:t + tile] for t in range(0, c, tile)]
    out = []
    for half in _SUM_LEVEL_HALVES:
        for i, x in enumerate(tiles):
            t0 = i * tile
            if half >= tile:
                start = (t0 // (2 * half)) * 2 * half
                r = start + half if backward else start + half - 1
                after = (t0 - start >= half) != backward
                out.append(x - row(r) if after else row(r) - x)
            elif half == 4:
                r = t0 + 4 if backward else t0 + 3
                out.append((x - row(r)) * sign4)
            else:
                r1, r2 = (t0 + 2, t0 + 6) if backward else (t0 + 1, t0 + 5)
                out.append((x - jnp.where(sub < 4, row(r1), row(r2))) * sign2)
    out.extend(tiles)
    last = 0 if backward else c - 1
    out.extend(row(last) - x for x in tiles)
    return jnp.concatenate(out, axis=0)


class _ChunkJob:
    def __init__(self, d, qv_ref, g_ref, o_ref, rows, slot):
        self.d, self.qv_ref, self.g_ref, self.o_ref = d, qv_ref, g_ref, o_ref
        self.rows, self.slot = rows, slot


def _scan_kernel(qvf_ref, gf_ref, qvb_ref, gb_ref, qvm_ref, gfm_ref, gbm_ref,
                 wf_ref, wb_ref, mf_ref, mb_ref, of_ref, ob_ref, ofm_ref, obm_ref,
                 st_ref, e_ref):
    c = REC_CHUNK
    d_model = gf_ref.shape[-1]
    n_heads = d_model // HEAD_DIM
    n_chunks = gf_ref.shape[1] // c
    step = pl.program_id(1)
    w_refs = (wf_ref, wb_ref)
    m_refs = (mf_ref, mb_ref)
    whole_rows = (c - 1, 0)

    @pl.when(step == 0)
    def _():
        st_ref[...] = jnp.zeros(st_ref.shape, F32)

    def decay_stage(job):
        g2 = job.g_ref[0, job.rows, :]
        g_hi = g2.astype(BF16)
        g_lo = (g2 - g_hi.astype(F32)).astype(BF16)
        cum = _dot(w_refs[job.d][...], jnp.concatenate([g_hi, g_lo], axis=0))
        e_ref[job.slot, job.d] = jnp.exp2(_split_decays(cum, backward=job.d == 1))

    def pair_stage(job, hd):
        d, m_ref = job.d, m_refs[job.d]
        sl = slice(hd * HEAD_DIM, (hd + 1) * HEAD_DIM)
        e = lambda r0: e_ref[job.slot, d, r0:r0 + c, sl]
        q = job.qv_ref[0, job.rows, sl].astype(F32)
        f = jnp.exp2(job.g_ref[0, job.rows, sl])
        k = 1.0 - f
        row_id = lax.broadcasted_iota(jnp.int32, (c, HEAD_DIM), 0)
        near = _dot_nt(jnp.concatenate([q, q * f], axis=0).astype(BF16), k.astype(BF16))
        pair = m_ref[_MASK_DIAG] * near[:c] + m_ref[_MASK_ADJACENT] * near[c:]
        for lvl, half in enumerate(_SUM_LEVEL_HALVES):
            is_query = ((row_id // half) % 2 == 1) != (d == 1)
            both = (jnp.where(is_query, q, k) * e(lvl * c)).astype(BF16)
            pair = pair + m_ref[lvl] * _dot_nt(both, both)
        q_in = (q * e(_ROWS_Q_INTER)).astype(BF16)
        k_out = (k * e(_ROWS_K_STATE)).astype(BF16)
        return pair.astype(BF16), q_in, k_out

    def output_stage(job, hd, pair, q_in, k_out):
        d = job.d
        sl = slice(hd * HEAD_DIM, (hd + 1) * HEAD_DIM)
        v = job.qv_ref[0, job.rows, d_model + hd * HEAD_DIM:d_model + (hd + 1) * HEAD_DIM]
        state = st_ref[d, hd]
        o = _dot(pair, v) + _dot(q_in, state.astype(BF16))
        job.o_ref[0, job.rows, sl] = o.astype(BF16)
        row = _ROWS_Q_INTER + whole_rows[d]
        t0 = (row // SUBLANE_TILE) * SUBLANE_TILE
        decay_col = e_ref[job.slot, d, t0:t0 + SUBLANE_TILE, sl].T[:, row - t0:row - t0 + 1]
        st_ref[d, hd] = state * decay_col + _dot_tn(k_out, v)

    def run(jobs):
        pending = []
        for group in jobs:
            for job in group:
                decay_stage(job)
            for hd in range(n_heads):
                for job in group:
                    pending.append((job, hd) + pair_stage(job, hd))
                    if len(pending) > SCAN_PIPELINE_LAG:
                        output_stage(*pending.pop(0))
        for item in pending:
            output_stage(*item)

    whole = slice(0, c)

    @pl.when(step == 0)
    def _():
        run([[_ChunkJob(0, qvm_ref, gfm_ref, ofm_ref, whole, 0)]])

    groups = []
    for i in range(n_chunks):
        fwd_rows = slice(i * c, (i + 1) * c)
        bwd_rows = slice((n_chunks - 1 - i) * c, (n_chunks - i) * c)
        groups.append([_ChunkJob(0, qvf_ref, gf_ref, of_ref, fwd_rows, i),
                       _ChunkJob(1, qvb_ref, gb_ref, ob_ref, bwd_rows, i)])
    run(groups)

    @pl.when(step == pl.num_programs(1) - 1)
    def _():
        run([[_ChunkJob(1, qvm_ref, gbm_ref, obm_ref, whole, 0)]])


def _scan(qv, gf, gb, qvm, gfm, gbm):
    b, s, d = gf.shape
    n_heads = d // HEAD_DIM
    chunks_per_step = min(SCAN_CHUNKS_PER_STEP, s // REC_CHUNK)
    rows = chunks_per_step * REC_CHUNK
    n_steps = s // rows
    wf, mf = _scan_constants(False)
    wb, mb = _scan_constants(True)
    fwd = pl.BlockSpec((1, rows, d), lambda bi, j: (bi, j, 0))
    bwd = pl.BlockSpec((1, rows, d), lambda bi, j: (bi, n_steps - 1 - j, 0))
    meta = pl.BlockSpec((1, REC_CHUNK, d), lambda bi, j: (bi, 0, 0))
    wide = lambda spec: pl.BlockSpec(spec.block_shape[:2] + (2 * d,), spec.index_map)
    return pl.pallas_call(
        _scan_kernel,
        grid=(b, n_steps),
        in_specs=[wide(fwd), fwd, wide(bwd), bwd, wide(meta), meta, meta,
                  _resident(wf.shape), _resident(wb.shape),
                  _resident(mf.shape), _resident(mb.shape)],
        out_specs=[fwd, bwd, meta, meta],
        out_shape=[jax.ShapeDtypeStruct((b, s, d), BF16)] * 2
        + [jax.ShapeDtypeStruct((b, REC_CHUNK, d), BF16)] * 2,
        scratch_shapes=[
            pltpu.VMEM((2, n_heads, HEAD_DIM, HEAD_DIM), F32),
            pltpu.VMEM((chunks_per_step, 2, _N_SUM_ROWS, d), F32),
        ],
        compiler_params=_params(2),
        name="gla_scan",
    )(qv, gf, qv, gb, qvm, gfm, gbm, wf, wb, mf, mb)


def _rope_tables(length):
    inv_freq = ROPE_THETA ** (-jnp.arange(0, HEAD_DIM, 2, dtype=F32) / HEAD_DIM)
    ang = jnp.arange(length, dtype=F32)[:, None] * inv_freq[None, :]
    cos, sin = jnp.cos(ang), jnp.sin(ang)
    return jnp.concatenate([cos, cos], axis=-1), jnp.concatenate([-sin, sin], axis=-1)


def kernel(x, meta_tokens, mix_norm_w, mlp_norm_w, attn_w_in, attn_w_out, attn_q_norm_w,
           attn_k_norm_w, attn_sink, rec_w_in, rec_w_out, rec_lb_fwd, rec_lb_bwd,
           rec_out_norm_w, mlp_w_up, mlp_w_down):
    b, s, d = x.shape
    depth = mix_norm_w.shape[0]
    n_mixers = 2
    tm = min(TOKEN_TILE, s)
    n_meta_rows = b * N_META
    assert d % HEAD_DIM == 0 and d // HEAD_DIM == N_KV_HEADS * GROUP, x.shape
    assert s % tm == 0 and s % ATTN_BLOCK == 0 and s % REC_CHUNK == 0, x.shape
    assert meta_tokens.shape == (N_META, d), meta_tokens.shape

    h = x
    hm = jnp.broadcast_to(meta_tokens.astype(x.dtype)[None], (b, N_META, d))
    cc, ss = _rope_tables(N_META + s)
    cc_meta, ss_meta = jnp.tile(cc[:N_META], (b, 1)), jnp.tile(ss[:N_META], (b, 1))
    cc_real, ss_real = cc[N_META:], ss[N_META:]
    row = lambda w: w.reshape(1, -1)
    flat_meta = lambda t: t.reshape(1, n_meta_rows, t.shape[-1])
    per_batch = lambda t: t.reshape(b, N_META, t.shape[-1])

    for layer in range(depth):
        j = layer // n_mixers
        nw = row(mix_norm_w[layer])
        mlp = (row(mlp_norm_w[layer]), mlp_w_up[layer].astype(BF16),
               mlp_w_down[layer].astype(BF16))
        if layer % n_mixers == 0:
            w_in = attn_w_in[j].astype(BF16)
            w_out = attn_w_out[j].astype(BF16)
            gains = (row(attn_q_norm_w[j]), row(attn_k_norm_w[j]))
            q, kv = _attn_in(h, nw, w_in, *gains, cc_real, ss_real, tm)
            qm, kvm = map(per_batch, _attn_in(
                flat_meta(hm), nw, w_in, *gains, cc_meta, ss_meta, n_meta_rows))
            sink = attn_sink[j].astype(F32)
            o = _attn(q, kv, kvm, sink)
            om = _attn_meta(qm, kvm, kv, sink)
            h = _post(h, o, w_out, *mlp, tm)
            hm = per_batch(_post(flat_meta(hm), flat_meta(om), w_out, *mlp, n_meta_rows))
        else:
            w_in = rec_w_in[j].astype(BF16)
            w_out = rec_w_out[j].astype(BF16)
            gain = row(rec_out_norm_w[j])
            qv, gf, gb, gate = _rec_in(h, nw, w_in, rec_lb_fwd, rec_lb_bwd, layer,
                                         math.gcd(REC_IN_TOKEN_TILE, s))
            qvm, gfm, gbm, gate_m = _rec_in_meta(hm, nw, w_in, rec_lb_fwd, rec_lb_bwd, layer)
            o_f, o_b, o_fm, o_bm = _scan(qv, gf, gb, qvm, gfm, gbm)
            h = _post(h, (o_f, o_b, gate), w_out, *mlp, tm, rec_extra=gain)
            meta_rows = lambda t: flat_meta(t[:, REC_CHUNK - N_META:])
            hm = per_batch(_post(flat_meta(hm), (meta_rows(o_fm), meta_rows(o_bm),
                                                 flat_meta(gate_m)),
                                 w_out, *mlp, n_meta_rows, rec_extra=gain))
    return h
```

```python
import functools
import math

import numpy as np
import jax
import jax.numpy as jnp
from jax import lax
from jax.experimental import pallas as pl
from jax.experimental.pallas import tpu as pltpu

N_META = 16
HEAD_DIM = 128
N_KV_HEADS = 2
GROUP = 4
WINDOW = 128
ATTN_BLOCK = 128
ROPE_THETA = 10000.0
REC_CHUNK = 64
EPS = 1e-6
LOG2E = 1.4426950408889634

F32 = jnp.float32
BF16 = jnp.bfloat16

V7X_VMEM_BYTES = 64 * 1024 * 1024
V7X_VMEM_LIMIT_BYTES = V7X_VMEM_BYTES * 7 // 8
TOKEN_TILE = 512
FF_CHUNK = 1024
SUB_TILE = 256
REC_IN_TOKEN_TILE = 1024
REC_IN_SUB_TILE = 128
ATTN_BLOCKS_PER_STEP = 4


def _params(n_axes):
    return pltpu.CompilerParams(
        dimension_semantics=("arbitrary",) * n_axes,
        vmem_limit_bytes=V7X_VMEM_LIMIT_BYTES,
    )


def _resident(shape):
    zeros = (0,) * len(shape)
    return pl.BlockSpec(shape, lambda *_: zeros, pipeline_mode=pl.Buffered(1))


def _rms(x, w):
    ms = jnp.mean(x * x, axis=-1, keepdims=True)
    return x * lax.rsqrt(ms + EPS) * w


def _dot(a, b):
    return jnp.dot(a, b, preferred_element_type=F32)


def _dot_nt(a, b):
    return lax.dot_general(a, b, (((1,), (1,)), ((), ())), preferred_element_type=F32)


def _dot_tn(a, b):
    return lax.dot_general(a, b, (((0,), (0,)), ((), ())), preferred_element_type=F32)


def _attn_in_kernel(h_ref, nw_ref, w_ref, qg_ref, kg_ref, cc_ref, ss_ref, q_ref, kv_ref):
    n_q = q_ref.shape[-1] // HEAD_DIM
    n_kv = kv_ref.shape[-1] // (2 * HEAD_DIM)
    tm = h_ref.shape[1]
    sub = min(tm, SUB_TILE)
    half = HEAD_DIM // 2
    q_gain = qg_ref[...] * (HEAD_DIM ** -0.5 * LOG2E)
    k_gain = kg_ref[...]

    for r in range(tm // sub):
        rows = slice(r * sub, (r + 1) * sub)
        hn = _rms(h_ref[0, rows, :], nw_ref[...]).astype(BF16)
        qkv = _dot(hn, w_ref[...])
        cc = cc_ref[rows, :]
        ss = ss_ref[rows, :]
        tabs_q = (q_gain * cc, pltpu.roll(q_gain, half, 1) * ss)
        tabs_k = (k_gain * cc, pltpu.roll(k_gain, half, 1) * ss)

        def norm_rope(x, tabs):
            y = x * lax.rsqrt(jnp.mean(x * x, axis=-1, keepdims=True) + EPS)
            return y * tabs[0] + pltpu.roll(y, half, 1) * tabs[1]

        for j in range(n_q):
            sl = slice(j * HEAD_DIM, (j + 1) * HEAD_DIM)
            q_ref[0, rows, sl] = norm_rope(qkv[:, sl], tabs_q).astype(BF16)
        for j in range(n_kv):
            sl = slice(j * HEAD_DIM, (j + 1) * HEAD_DIM)
            src = slice((n_q + j) * HEAD_DIM, (n_q + j + 1) * HEAD_DIM)
            kv_ref[0, rows, sl] = norm_rope(qkv[:, src], tabs_k).astype(BF16)
        kv_ref[0, rows, n_kv * HEAD_DIM:] = qkv[:, (n_q + n_kv) * HEAD_DIM:].astype(BF16)


def _attn_in(h, nw, w_in, q_gain, k_gain, cc, ss, tm):
    bx, sx, d = h.shape
    n_in = w_in.shape[1]
    dq = d
    dkv = (n_in - dq) // 2
    grid = (bx, sx // tm)
    tok = lambda width: pl.BlockSpec((1, tm, width), lambda b, i: (b, i, 0))
    return pl.pallas_call(
        _attn_in_kernel,
        grid=grid,
        in_specs=[
            tok(d),
            _resident((1, d)),
            _resident((d, n_in)),
            _resident((1, HEAD_DIM)),
            _resident((1, HEAD_DIM)),
            pl.BlockSpec((tm, HEAD_DIM), lambda b, i: (i, 0)),
            pl.BlockSpec((tm, HEAD_DIM), lambda b, i: (i, 0)),
        ],
        out_specs=[tok(dq), tok(2 * dkv)],
        out_shape=[
            jax.ShapeDtypeStruct((bx, sx, dq), BF16),
            jax.ShapeDtypeStruct((bx, sx, 2 * dkv), BF16),
        ],
        compiler_params=_params(2),
        name="attn_in",
    )(h, nw, w_in, q_gain, k_gain, cc, ss)


def _softmax_pv(scores, values, sink_col):
    m = sink_col
    for s in scores:
        m = jnp.maximum(m, jnp.max(s, axis=-1, keepdims=True))
    den = jnp.exp2(sink_col - m)
    acc = None
    for s, v in zip(scores, values):
        p = jnp.exp2(s - m)
        den = den + jnp.sum(p, axis=-1, keepdims=True)
        pv = _dot(p.astype(BF16), v)
        acc = pv if acc is None else acc + pv
    return acc / den


def _attn_kernel(sink_ref, q_ref, kvp_ref, kvc_ref, kvn_ref, kvm_ref, o_ref, *, n_steps):
    step = pl.program_id(1)
    blk = ATTN_BLOCK
    n_blk = q_ref.shape[1] // blk
    v_off = N_KV_HEADS * HEAD_DIM
    a = lax.broadcasted_iota(jnp.int32, (GROUP * blk, 3 * blk), 0) % blk
    c = lax.broadcasted_iota(jnp.int32, (GROUP * blk, 3 * blk), 1)
    in_prev = (c < blk) & (c >= a)
    in_own = (c >= blk) & (c < 2 * blk)
    in_next = (c >= 2 * blk) & (c - 2 * blk <= a)
    interior = in_prev | in_own | in_next
    neg_inf = jnp.float32(-jnp.inf)
    lane = lax.broadcasted_iota(jnp.int32, (GROUP * blk, blk), 1)
    meta_lane = lane < N_META
    meta_fill = []
    for hk in range(N_KV_HEADS):
        sink_tile = jnp.concatenate(
            [jnp.full((blk, blk), sink_ref[hk * GROUP + g] * LOG2E, F32) for g in range(GROUP)],
            axis=0)
        meta_fill.append(jnp.where(lane == N_META, sink_tile, neg_inf))
    kv_blocks = ([kvp_ref.at[0]] + [kvc_ref.at[0, pl.ds(i * blk, blk), :] for i in range(n_blk)]
                 + [kvn_ref.at[0]])
    pad = jnp.zeros((blk - N_META, HEAD_DIM), BF16)

    def scores(i, hk):
        rows = slice(i * blk, (i + 1) * blk)
        mask = interior
        if i == 0:
            mask = (in_prev & (step > 0)) | in_own | in_next
        if i == n_blk - 1:
            mask = mask & (in_prev | in_own | (step < n_steps - 1))
        ks = slice(hk * HEAD_DIM, (hk + 1) * HEAD_DIM)
        keys = jnp.concatenate([r[:, ks] for r in kv_blocks[i:i + 3]], axis=0)
        q4 = jnp.concatenate(
            [q_ref[0, rows, h * HEAD_DIM:(h + 1) * HEAD_DIM]
             for h in range(hk * GROUP, (hk + 1) * GROUP)], axis=0)
        s_band = jnp.where(mask, _dot_nt(q4, keys), neg_inf)
        km = jnp.concatenate([kvm_ref[0, :, ks], pad], axis=0)
        s_meta = jnp.where(meta_lane, _dot_nt(q4, km), meta_fill[hk])
        return [s_band[:, t * blk:(t + 1) * blk] for t in range(3)] + [s_meta]

    def finish(i, hk, tiles):
        rows = slice(i * blk, (i + 1) * blk)
        vs = slice(v_off + hk * HEAD_DIM, v_off + (hk + 1) * HEAD_DIM)
        vals = jnp.concatenate([r[:, vs] for r in kv_blocks[i:i + 3]] + [kvm_ref[0, :, vs], pad],
                               axis=0)
        m = jnp.max(functools.reduce(jnp.maximum, tiles), axis=-1, keepdims=True)
        p = jnp.concatenate([jnp.exp2(t - m) for t in tiles], axis=1).astype(BF16)
        o4 = _dot(p, vals) / jnp.sum(p.astype(F32), axis=-1, keepdims=True)
        for g in range(GROUP):
            h = hk * GROUP + g
            o_ref[0, rows, h * HEAD_DIM:(h + 1) * HEAD_DIM] = o4[g * blk:(g + 1) * blk].astype(BF16)

    previous = None
    for i in range(n_blk):
        for hk in range(N_KV_HEADS):
            current = (i, hk, scores(i, hk))
            if previous is not None:
                finish(*previous)
            previous = current
    finish(*previous)


def _attn(q, kv, kvm, sink):
    b, s, dq = q.shape
    dkv2 = kv.shape[-1]
    blk = ATTN_BLOCK
    n_blk = min(ATTN_BLOCKS_PER_STEP, s // blk)
    n_steps = s // (n_blk * blk)
    last_blk = s // blk - 1
    prev = pl.BlockSpec((1, blk, dkv2), lambda bi, n: (bi, jnp.maximum(n_blk * n - 1, 0), 0))
    own = pl.BlockSpec((1, n_blk * blk, dkv2), lambda bi, n: (bi, n, 0))
    nxt = pl.BlockSpec((1, blk, dkv2),
                       lambda bi, n: (bi, jnp.minimum(n_blk * (n + 1), last_blk), 0))
    meta = pl.BlockSpec((1, N_META, dkv2), lambda bi, n: (bi, 0, 0))
    return pl.pallas_call(
        functools.partial(_attn_kernel, n_steps=n_steps),
        grid=(b, n_steps),
        in_specs=[
            pl.BlockSpec(memory_space=pltpu.SMEM),
            pl.BlockSpec((1, n_blk * blk, dq), lambda bi, n: (bi, n, 0)),
            prev, own, nxt, meta,
        ],
        out_specs=pl.BlockSpec((1, n_blk * blk, dq), lambda bi, n: (bi, n, 0)),
        out_shape=jax.ShapeDtypeStruct((b, s, dq), BF16),
        compiler_params=_params(2),
        name="band_attn",
    )(sink, q, kv, kv, kv, kvm)


def _attn_meta_kernel(sink_ref, qm_ref, kvm_ref, kvr_ref, o_ref):
    rows = GROUP * N_META
    v_off = N_KV_HEADS * HEAD_DIM
    i = lax.broadcasted_iota(jnp.int32, (rows, ATTN_BLOCK), 0) % N_META
    j = lax.broadcasted_iota(jnp.int32, (rows, ATTN_BLOCK), 1)
    allowed = (N_META + j) - i <= WINDOW
    neg_inf = jnp.float32(-jnp.inf)
    for hk in range(N_KV_HEADS):
        ks = slice(hk * HEAD_DIM, (hk + 1) * HEAD_DIM)
        vs = slice(v_off + hk * HEAD_DIM, v_off + (hk + 1) * HEAD_DIM)
        heads = [hk * GROUP + g for g in range(GROUP)]
        q4 = jnp.concatenate(
            [qm_ref[0, :, h * HEAD_DIM:(h + 1) * HEAD_DIM] for h in heads], axis=0)
        s_meta = _dot_nt(q4, kvm_ref[0, :, ks])
        s_real = jnp.where(allowed, _dot_nt(q4, kvr_ref[0, :, ks]), neg_inf)
        sink_col = jnp.concatenate(
            [jnp.full((N_META, 1), sink_ref[h] * LOG2E, F32) for h in heads], axis=0)
        o4 = _softmax_pv([s_meta, s_real], [kvm_ref[0, :, vs], kvr_ref[0, :, vs]], sink_col)
        for g, h in enumerate(heads):
            o_ref[0, :, h * HEAD_DIM:(h + 1) * HEAD_DIM] = (
                o4[g * N_META:(g + 1) * N_META].astype(BF16))


def _attn_meta(qm, kvm, kv, sink):
    b, _, dq = qm.shape
    dkv2 = kvm.shape[-1]
    meta = lambda width: pl.BlockSpec((1, N_META, width), lambda bi: (bi, 0, 0))
    first = pl.BlockSpec((1, ATTN_BLOCK, dkv2), lambda bi: (bi, 0, 0))
    return pl.pallas_call(
        _attn_meta_kernel,
        grid=(b,),
        in_specs=[pl.BlockSpec(memory_space=pltpu.SMEM), meta(dq), meta(dkv2), first],
        out_specs=meta(dq),
        out_shape=jax.ShapeDtypeStruct((b, N_META, dq), BF16),
        compiler_params=_params(1),
        name="meta_attn",
    )(sink, qm, kvm, kv)


def _post_kernel(*refs, rec, ff_chunk):
    if rec:
        h_ref, of_ref, ob_ref, gate_ref, og_ref, wo_ref, nw_ref, wu_ref, wd_ref, out_ref = refs
        o = of_ref[0].astype(F32) + ob_ref[0].astype(F32)
        n_heads = o.shape[-1] // HEAD_DIM
        o = jnp.concatenate(
            [_rms(o[:, j * HEAD_DIM:(j + 1) * HEAD_DIM], og_ref[...]) for j in range(n_heads)],
            axis=1)
        o = (o * gate_ref[0].astype(F32)).astype(BF16)
    else:
        h_ref, o_ref, wo_ref, nw_ref, wu_ref, wd_ref, out_ref = refs
        o = o_ref[0]
    h1 = h_ref[0] + _dot(o, wo_ref[...])
    hn = _rms(h1, nw_ref[...]).astype(BF16)
    d_ff = wu_ref.shape[1]
    acc = h1
    for c in range(d_ff // ff_chunk):
        sl = slice(c * ff_chunk, (c + 1) * ff_chunk)
        z = jnp.maximum(_dot(hn, wu_ref[:, sl]), 0.0)
        acc = acc + _dot((z * z).astype(BF16), wd_ref[sl, :])
    out_ref[0] = acc


def _post(h, mixer_out, w_out, nw, w_up, w_down, tm, rec_extra=None, in_place=False):
    bx, sx, d = h.shape
    d_ff = w_up.shape[1]
    tok = pl.BlockSpec((1, tm, d), lambda b, i: (b, i, 0))
    rec = rec_extra is not None
    if rec:
        ins = [h, *mixer_out, rec_extra, w_out, nw, w_up, w_down]
        specs = [tok, tok, tok, tok, _resident((1, HEAD_DIM))]
    else:
        ins = [h, mixer_out, w_out, nw, w_up, w_down]
        specs = [tok, tok]
    specs += [_resident((d, d)), _resident((1, d)), _resident((d, d_ff)), _resident((d_ff, d))]
    return pl.pallas_call(
        functools.partial(_post_kernel, rec=rec, ff_chunk=min(FF_CHUNK, d_ff)),
        grid=(bx, sx // tm),
        in_specs=specs,
        out_specs=tok,
        out_shape=jax.ShapeDtypeStruct((bx, sx, d), F32),
        input_output_aliases={0: 0} if in_place else {},
        compiler_params=_params(2),
        name="post_rec" if rec else "post_attn",
    )(*ins)


def _lower_bound(tab_ref, layer):
    t = tab_ref[...]
    e = jnp.exp(t - jnp.max(t, axis=0, keepdims=True))
    p = e / jnp.sum(e, axis=0, keepdims=True)
    return jnp.sum(p[1:layer + 1], axis=0, keepdims=True)


def _rec_in_rows(h, nw_ref, w_ref, lbf_ref, lbb_ref, layer):
    d = h.shape[-1]
    hn = _rms(h, nw_ref[...]).astype(BF16)
    proj = lambda j: _dot(hn, w_ref[:, j * d:(j + 1) * d])

    def log_forget(z, tab_ref):
        lb = _lower_bound(tab_ref, layer)
        return jnp.log2(lb + (1.0 - lb) * jax.nn.sigmoid(z))

    gf = log_forget(proj(2), lbf_ref)
    gb = log_forget(proj(3), lbb_ref)
    zq = proj(0)
    q = zq * jax.nn.sigmoid(zq)
    zg = proj(4)
    gate = zg * jax.nn.sigmoid(zg)
    v = proj(1)
    return q, v, gf, gb, gate


def _rec_in_kernel(h_ref, nw_ref, w_ref, lbf_ref, lbb_ref,
                   q_ref, v_ref, gf_ref, gb_ref, gate_ref, *, layer):
    tm = h_ref.shape[1]
    sub = min(tm, REC_IN_SUB_TILE)
    for r in range(tm // sub):
        rows = slice(r * sub, (r + 1) * sub)
        q, v, gf, gb, gate = _rec_in_rows(h_ref[0, rows, :], nw_ref, w_ref, lbf_ref, lbb_ref,
                                          layer)
        q_ref[0, rows, :] = q.astype(BF16)
        v_ref[0, rows, :] = v.astype(BF16)
        gf_ref[0, rows, :] = gf
        gb_ref[0, rows, :] = gb
        gate_ref[0, rows, :] = gate.astype(BF16)


def _rec_in(h, nw, w_in, lb_f, lb_b, layer, tm):
    b, s, d = h.shape
    depth = lb_f.shape[0]
    tok = pl.BlockSpec((1, tm, d), lambda bi, i: (bi, i, 0))
    return pl.pallas_call(
        functools.partial(_rec_in_kernel, layer=layer),
        grid=(b, s // tm),
        in_specs=[tok, _resident((1, d)), _resident((d, w_in.shape[1])),
                  _resident((depth, d)), _resident((depth, d))],
        out_specs=[tok, tok, tok, tok, tok],
        out_shape=[
            jax.ShapeDtypeStruct((b, s, d), BF16),
            jax.ShapeDtypeStruct((b, s, d), BF16),
            jax.ShapeDtypeStruct((b, s, d), F32),
            jax.ShapeDtypeStruct((b, s, d), F32),
            jax.ShapeDtypeStruct((b, s, d), BF16),
        ],
        compiler_params=_params(2),
        name="rec_in",
    )(h, nw, w_in, lb_f, lb_b)


def _rec_in_meta_kernel(hm_ref, nw_ref, w_ref, lbf_ref, lbb_ref,
                        q_ref, v_ref, gf_ref, gb_ref, gate_ref, *, layer):
    b, n_meta, d = hm_ref.shape
    h = hm_ref[...].reshape(b * n_meta, d)
    q, v, gf, gb, gate = _rec_in_rows(h, nw_ref, w_ref, lbf_ref, lbb_ref, layer)
    pad = REC_CHUNK - n_meta
    for ref, val in ((q_ref, q), (v_ref, v), (gf_ref, gf), (gb_ref, gb)):
        ref[:, :pad, :] = jnp.zeros((b, pad, d), ref.dtype)
        ref[:, pad:, :] = val.reshape(b, n_meta, d).astype(ref.dtype)
    gate_ref[...] = gate.reshape(b, n_meta, d).astype(BF16)


def _rec_in_meta(hm, nw, w_in, lb_f, lb_b, layer):
    b, n_meta, d = hm.shape
    depth = lb_f.shape[0]
    chunk = pl.BlockSpec((b, REC_CHUNK, d), lambda i: (0, 0, 0))
    whole = pl.BlockSpec((b, n_meta, d), lambda i: (0, 0, 0))
    return pl.pallas_call(
        functools.partial(_rec_in_meta_kernel, layer=layer),
        grid=(1,),
        in_specs=[whole, _resident((1, d)), _resident((d, w_in.shape[1])),
                  _resident((depth, d)), _resident((depth, d))],
        out_specs=[chunk, chunk, chunk, chunk, whole],
        out_shape=[
            jax.ShapeDtypeStruct((b, REC_CHUNK, d), BF16),
            jax.ShapeDtypeStruct((b, REC_CHUNK, d), BF16),
            jax.ShapeDtypeStruct((b, REC_CHUNK, d), F32),
            jax.ShapeDtypeStruct((b, REC_CHUNK, d), F32),
            jax.ShapeDtypeStruct((b, n_meta, d), BF16),
        ],
        compiler_params=_params(1),
        name="rec_in_meta",
    )(hm, nw, w_in, lb_f, lb_b)


_SUM_LEVEL_HALVES = (32, 16, 8, 4, 2)
_N_SUM_LEVELS = len(_SUM_LEVEL_HALVES)
_MASK_ADJACENT = _N_SUM_LEVELS
_MASK_DIAG = _N_SUM_LEVELS + 1
_ROWS_Q_INTER = _N_SUM_LEVELS * REC_CHUNK
_ROWS_K_STATE = (_N_SUM_LEVELS + 1) * REC_CHUNK
_N_SUM_ROWS = (_N_SUM_LEVELS + 2) * REC_CHUNK
SCAN_CHUNKS_PER_STEP = 4
SCAN_PIPELINE_LAG = 8
SUBLANE_TILE = 8


def _scan_constants(backward):
    c = REC_CHUNK
    u = np.arange(c)
    w = (u[None, :] >= u[:, None]) if backward else (u[None, :] <= u[:, None])
    masks = np.zeros((_N_SUM_LEVELS + 2, c, c), np.float32)
    for lvl, half in enumerate(_SUM_LEVEL_HALVES + (1,)):
        for t in range(c):
            start = (t // (2 * half)) * 2 * half
            if t - start >= half:
                masks[lvl, t, start:start + half] = 1.0
    masks[_MASK_DIAG] = np.eye(c, dtype=np.float32)
    if backward:
        masks = masks[:, ::-1, ::-1]
    w = np.concatenate([w, w], axis=1).astype(np.float32)
    return jnp.asarray(w, BF16), jnp.asarray(np.ascontiguousarray(masks), F32)


def _split_decays(cum, backward):
    c, d = cum.shape
    tile = SUBLANE_TILE
    sub = lax.broadcasted_iota(jnp.int32, (tile, d), 0)
    sign4 = jnp.where((sub < 4) != backward, -1.0, 1.0).astype(F32)
    sign2 = jnp.where(((sub % 4) < 2) != backward, -1.0, 1.0).astype(F32)
    row = lambda r: cum[r:r + 1]
    tiles = [cum[t:t + tile] for t in range(0, c, tile)]
    out = []
    for half in _SUM_LEVEL_HALVES:
        for i, x in enumerate(tiles):
            t0 = i * tile
            if half >= tile:
                start = (t0 // (2 * half)) * 2 * half
                r = start + half if backward else start + half - 1
                after = (t0 - start >= half) != backward
                out.append(x - row(r) if after else row(r) - x)
            elif half == 4:
                r = t0 + 4 if backward else t0 + 3
                out.append((x - row(r)) * sign4)
            else:
                r1, r2 = (t0 + 2, t0 + 6) if backward else (t0 + 1, t0 + 5)
                out.append((x - jnp.where(sub < 4, row(r1), row(r2))) * sign2)
    out.extend(tiles)
    last = 0 if backward else c - 1
    out.extend(row(last) - x for x in tiles)
    return jnp.concatenate(out, axis=0)


class _ChunkJob:
    def __init__(self, d, q_ref, v_ref, g_ref, o_ref, rows, slot):
        self.d, self.q_ref, self.v_ref, self.g_ref, self.o_ref = d, q_ref, v_ref, g_ref, o_ref
        self.rows, self.slot = rows, slot


def _scan_kernel(qf_ref, vf_ref, gf_ref, qb_ref, vb_ref, gb_ref, qm_ref, vm_ref, gfm_ref, gbm_ref,
                 wf_ref, wb_ref, mf_ref, mb_ref, of_ref, ob_ref, ofm_ref, obm_ref,
                 st_ref, e_ref):
    c = REC_CHUNK
    n_heads = qf_ref.shape[-1] // HEAD_DIM
    n_chunks = qf_ref.shape[1] // c
    step = pl.program_id(1)
    w_refs = (wf_ref, wb_ref)
    m_refs = (mf_ref, mb_ref)
    whole_rows = (c - 1, 0)

    @pl.when(step == 0)
    def _():
        st_ref[...] = jnp.zeros(st_ref.shape, F32)

    def decay_stage(job):
        g2 = job.g_ref[0, job.rows, :]
        g_hi = g2.astype(BF16)
        g_lo = (g2 - g_hi.astype(F32)).astype(BF16)
        cum = _dot(w_refs[job.d][...], jnp.concatenate([g_hi, g_lo], axis=0))
        e_ref[job.slot, job.d] = jnp.exp2(_split_decays(cum, backward=job.d == 1))

    def pair_stage(job, hd):
        d, m_ref = job.d, m_refs[job.d]
        sl = slice(hd * HEAD_DIM, (hd + 1) * HEAD_DIM)
        e = lambda r0: e_ref[job.slot, d, r0:r0 + c, sl]
        q = job.q_ref[0, job.rows, sl].astype(F32)
        f = jnp.exp2(job.g_ref[0, job.rows, sl])
        k = 1.0 - f
        row_id = lax.broadcasted_iota(jnp.int32, (c, HEAD_DIM), 0)
        near = _dot_nt(jnp.concatenate([q, q * f], axis=0).astype(BF16), k.astype(BF16))
        pair = m_ref[_MASK_DIAG] * near[:c] + m_ref[_MASK_ADJACENT] * near[c:]
        for lvl, half in enumerate(_SUM_LEVEL_HALVES):
            is_query = ((row_id // half) % 2 == 1) != (d == 1)
            both = (jnp.where(is_query, q, k) * e(lvl * c)).astype(BF16)
            pair = pair + m_ref[lvl] * _dot_nt(both, both)
        q_in = (q * e(_ROWS_Q_INTER)).astype(BF16)
        k_out = (k * e(_ROWS_K_STATE)).astype(BF16)
        return pair.astype(BF16), q_in, k_out

    def output_stage(job, hd, pair, q_in, k_out):
        d = job.d
        sl = slice(hd * HEAD_DIM, (hd + 1) * HEAD_DIM)
        v = job.v_ref[0, job.rows, sl]
        state = st_ref[d, hd]
        o = _dot(pair, v) + _dot(q_in, state.astype(BF16))
        job.o_ref[0, job.rows, sl] = o.astype(BF16)
        row = _ROWS_Q_INTER + whole_rows[d]
        t0 = (row // SUBLANE_TILE) * SUBLANE_TILE
        decay_col = e_ref[job.slot, d, t0:t0 + SUBLANE_TILE, sl].T[:, row - t0:row - t0 + 1]
        st_ref[d, hd] = state * decay_col + _dot_tn(k_out, v)

    def run(jobs):
        pending = []
        for group in jobs:
            for job in group:
                decay_stage(job)
            for hd in range(n_heads):
                for job in group:
                    pending.append((job, hd) + pair_stage(job, hd))
                    if len(pending) > SCAN_PIPELINE_LAG:
                        output_stage(*pending.pop(0))
        for item in pending:
            output_stage(*item)

    whole = slice(0, c)

    @pl.when(step == 0)
    def _():
        run([[_ChunkJob(0, qm_ref, vm_ref, gfm_ref, ofm_ref, whole, 0)]])

    groups = []
    for i in range(n_chunks):
        fwd_rows = slice(i * c, (i + 1) * c)
        bwd_rows = slice((n_chunks - 1 - i) * c, (n_chunks - i) * c)
        groups.append([_ChunkJob(0, qf_ref, vf_ref, gf_ref, of_ref, fwd_rows, i),
                       _ChunkJob(1, qb_ref, vb_ref, gb_ref, ob_ref, bwd_rows, i)])
    run(groups)

    @pl.when(step == pl.num_programs(1) - 1)
    def _():
        run([[_ChunkJob(1, qm_ref, vm_ref, gbm_ref, obm_ref, whole, 0)]])


def _scan(q, v, gf, gb, qm, vm, gfm, gbm):
    b, s, d = q.shape
    n_heads = d // HEAD_DIM
    chunks_per_step = min(SCAN_CHUNKS_PER_STEP, s // REC_CHUNK)
    rows = chunks_per_step * REC_CHUNK
    n_steps = s // rows
    wf, mf = _scan_constants(False)
    wb, mb = _scan_constants(True)
    fwd = pl.BlockSpec((1, rows, d), lambda bi, j: (bi, j, 0))
    bwd = pl.BlockSpec((1, rows, d), lambda bi, j: (bi, n_steps - 1 - j, 0))
    meta = pl.BlockSpec((1, REC_CHUNK, d), lambda bi, j: (bi, 0, 0))
    return pl.pallas_call(
        _scan_kernel,
        grid=(b, n_steps),
        in_specs=[fwd, fwd, fwd, bwd, bwd, bwd, meta, meta, meta, meta,
                  _resident(wf.shape), _resident(wb.shape),
                  _resident(mf.shape), _resident(mb.shape)],
        out_specs=[fwd, bwd, meta, meta],
        out_shape=[jax.ShapeDtypeStruct((b, s, d), BF16)] * 2
        + [jax.ShapeDtypeStruct((b, REC_CHUNK, d), BF16)] * 2,
        scratch_shapes=[
            pltpu.VMEM((2, n_heads, HEAD_DIM, HEAD_DIM), F32),
            pltpu.VMEM((chunks_per_step, 2, _N_SUM_ROWS, d), F32),
        ],
        compiler_params=_params(2),
        name="gla_scan",
    )(q, v, gf, q, v, gb, qm, vm, gfm, gbm, wf, wb, mf, mb)


def _rope_tables(length):
    inv_freq = ROPE_THETA ** (-jnp.arange(0, HEAD_DIM, 2, dtype=F32) / HEAD_DIM)
    ang = jnp.arange(length, dtype=F32)[:, None] * inv_freq[None, :]
    cos, sin = jnp.cos(ang), jnp.sin(ang)
    return jnp.concatenate([cos, cos], axis=-1), jnp.concatenate([-sin, sin], axis=-1)


def kernel(x, meta_tokens, mix_norm_w, mlp_norm_w, attn_w_in, attn_w_out, attn_q_norm_w,
           attn_k_norm_w, attn_sink, rec_w_in, rec_w_out, rec_lb_fwd, rec_lb_bwd,
           rec_out_norm_w, mlp_w_up, mlp_w_down):
    b, s, d = x.shape
    depth = mix_norm_w.shape[0]
    n_mixers = 2
    tm = min(TOKEN_TILE, s)
    n_meta_rows = b * N_META
    assert d % HEAD_DIM == 0 and d // HEAD_DIM == N_KV_HEADS * GROUP, x.shape
    assert s % tm == 0 and s % ATTN_BLOCK == 0 and s % REC_CHUNK == 0, x.shape
    assert meta_tokens.shape == (N_META, d), meta_tokens.shape

    h = x
    hm = jnp.broadcast_to(meta_tokens.astype(x.dtype)[None], (b, N_META, d))
    cc, ss = _rope_tables(N_META + s)
    cc_meta, ss_meta = jnp.tile(cc[:N_META], (b, 1)), jnp.tile(ss[:N_META], (b, 1))
    cc_real, ss_real = cc[N_META:], ss[N_META:]
    row = lambda w: w.reshape(1, -1)
    flat_meta = lambda t: t.reshape(1, n_meta_rows, t.shape[-1])
    per_batch = lambda t: t.reshape(b, N_META, t.shape[-1])

    for layer in range(depth):
        j = layer // n_mixers
        nw = row(mix_norm_w[layer])
        mlp = (row(mlp_norm_w[layer]), mlp_w_up[layer].astype(BF16),
               mlp_w_down[layer].astype(BF16))
        if layer % n_mixers == 0:
            w_in = attn_w_in[j].astype(BF16)
            w_out = attn_w_out[j].astype(BF16)
            gains = (row(attn_q_norm_w[j]), row(attn_k_norm_w[j]))
            q, kv = _attn_in(h, nw, w_in, *gains, cc_real, ss_real, tm)
            qm, kvm = map(per_batch, _attn_in(
                flat_meta(hm), nw, w_in, *gains, cc_meta, ss_meta, n_meta_rows))
            sink = attn_sink[j].astype(F32)
            o = _attn(q, kv, kvm, sink)
            om = _attn_meta(qm, kvm, kv, sink)
            h = _post(h, o, w_out, *mlp, tm, in_place=layer > 0)
            hm = per_batch(_post(flat_meta(hm), flat_meta(om), w_out, *mlp, n_meta_rows))
        else:
            w_in = rec_w_in[j].astype(BF16)
            w_out = rec_w_out[j].astype(BF16)
            gain = row(rec_out_norm_w[j])
            q, v, gf, gb, gate = _rec_in(h, nw, w_in, rec_lb_fwd, rec_lb_bwd, layer,
                                         math.gcd(REC_IN_TOKEN_TILE, s))
            qm, vm, gfm, gbm, gate_m = _rec_in_meta(hm, nw, w_in, rec_lb_fwd, rec_lb_bwd, layer)
            o_f, o_b, o_fm, o_bm = _scan(q, v, gf, gb, qm, vm, gfm, gbm)
            h = _post(h, (o_f, o_b, gate), w_out, *mlp, tm, rec_extra=gain, in_place=True)
            meta_rows = lambda t: flat_meta(t[:, REC_CHUNK - N_META:])
            hm = per_batch(_post(flat_meta(hm), (meta_rows(o_fm), meta_rows(o_bm),
                                                 flat_meta(gate_m)),
                                 w_out, *mlp, n_meta_rows, rec_extra=gain))
    return h
```
